```python
import math
import jax
import jax.numpy as jnp
from jax import lax
import numpy as np

D_MODEL = 1024
BATCH = 4
SEQ = 8192
DEPTH = 2

MIX_WIDTH = D_MODEL
BRANCH_W = MIX_WIDTH // 4
S5_GROUP = 16
S5_GROUPS = BRANCH_W // S5_GROUP
S5_STATE = 64
GMLP_HEADS = 4
GMLP_HEAD_DIM = BRANCH_W // GMLP_HEADS
GMLP_CHUNK = 128
GDN_HEAD_DIM = 64
GDN_HEADS = BRANCH_W // GDN_HEAD_DIM
GDN_CONV = 4
GDN_CHUNK = 64
SC_CONV = 3
IN_SIZES = (BRANCH_W,
            2 * BRANCH_W,
            3 * BRANCH_W,
            BRANCH_W,
            GDN_HEADS,
            GDN_HEADS,
            3 * BRANCH_W)
IN_COLS = sum(IN_SIZES)
D_FF = 2816
N_EXPERTS = 8
TOP_K = 2
D_FF_EXPERT = 3584
MOE_BLOCK = 512
N_DENSE = (DEPTH + 1) // 2
N_MOE = DEPTH // 2
EPS = 1e-6

kernel_name = 'hybrid_parallel_s5_gmlp_gdn_shortconv_moe'


def rmsnorm(x, g):
    xf = x.astype(jnp.float32)
    y = xf * lax.rsqrt(jnp.mean(xf * xf, axis=-1, keepdims=True) + EPS)
    return (y * g.astype(jnp.float32)).astype(x.dtype)


def layernorm(x, g, b):
    xf = x.astype(jnp.float32)
    xc = xf - jnp.mean(xf, axis=-1, keepdims=True)
    y = xc * lax.rsqrt(jnp.mean(xc * xc, axis=-1, keepdims=True) + EPS)
    return (y * g.astype(jnp.float32) + b.astype(jnp.float32)).astype(x.dtype)


def l2norm(x):
    return x * lax.rsqrt(jnp.sum(x * x, axis=-1, keepdims=True) + EPS)


def split_cols(p, sizes):
    out, start = [], 0
    for s in sizes:
        out.append(p[..., start:start + s])
        start += s
    return out


def causal_dwconv(x, w):
    k, c = w.shape
    return lax.conv_general_dilated(x, w[:, None, :].astype(x.dtype), window_strides=(1,),
                                    padding=[(k - 1, 0)], dimension_numbers=('NWC', 'WIO', 'NWC'),
                                    feature_group_count=c)


def s5_mixer(u, lam_re, lam_im, log_step, b_re, b_im, c_re, c_im, d_skip, w_glu, b_glu):
    f32 = jnp.float32
    bsz, seqlen, _ = u.shape
    uf = u.astype(f32)
    ug = uf.reshape(bsz, seqlen, S5_GROUPS, S5_GROUP)
    lr, li = lam_re.astype(f32), lam_im.astype(f32)
    dt = jnp.exp(log_step.astype(f32))[:, None]
    mag = jnp.exp(lr * dt)
    ar, ai = mag * jnp.cos(li * dt), mag * jnp.sin(li * dt)
    den = lr * lr + li * li
    fr = ((ar - 1.0) * lr + ai * li) / den
    fi = (ai * lr - (ar - 1.0) * li) / den
    br, bi = b_re.astype(f32), b_im.astype(f32)
    bbr = fr[..., None] * br - fi[..., None] * bi
    bbi = fr[..., None] * bi + fi[..., None] * br
    bu_r = jnp.einsum('blgp,gnp->blgn', ug, bbr)
    bu_i = jnp.einsum('blgp,gnp->blgn', ug, bbi)
    a_r = jnp.broadcast_to(ar, bu_r.shape)
    a_i = jnp.broadcast_to(ai, bu_i.shape)

    def combine(e1, e2):
        a1r, a1i, b1r, b1i = e1
        a2r, a2i, b2r, b2i = e2
        return (a2r * a1r - a2i * a1i, a2r * a1i + a2i * a1r,
                a2r * b1r - a2i * b1i + b2r, a2r * b1i + a2i * b1r + b2i)

    _, _, s_r, s_i = lax.associative_scan(combine, (a_r, a_i, bu_r, bu_i), axis=1)
    y = (jnp.einsum('blgn,gpn->blgp', s_r, c_re.astype(f32))
         - jnp.einsum('blgn,gpn->blgp', s_i, c_im.astype(f32)))
    y = y.reshape(bsz, seqlen, BRANCH_W) + d_skip.astype(f32) * uf
    y = jax.nn.gelu(y)
    y = y * jax.nn.sigmoid(y @ w_glu.astype(f32) + b_glu.astype(f32))
    return y.astype(u.dtype)


def gmlp_mixer(p, ln_g, ln_b, w_sp, b_sp):
    bsz, seqlen, _ = p.shape
    z = jax.nn.gelu(p)
    u, v = z[..., :BRANCH_W], z[..., BRANCH_W:]
    v = layernorm(v, ln_g, ln_b)
    n = seqlen // GMLP_CHUNK
    vc = v.reshape(bsz, n, GMLP_CHUNK, GMLP_HEADS, GMLP_HEAD_DIM)
    mask = jnp.tril(jnp.ones((GMLP_CHUNK, GMLP_CHUNK), dtype=bool))
    w = jnp.where(mask, w_sp, 0.0).astype(v.dtype)
    s = jnp.einsum('hts,bnshd->bnthd', w, vc) + b_sp.T[:, :, None].astype(v.dtype)
    return u * s.reshape(bsz, seqlen, BRANCH_W)


def gated_delta_rule(q, k, v, g, beta):
    bsz, seqlen, nh, dk = q.shape
    dv = v.shape[-1]
    c = GDN_CHUNK
    n = seqlen // c

    def chunked(t):
        return jnp.moveaxis(t.reshape(bsz, n, c, nh, -1), 3, 1)

    q = chunked(q) * (dk ** -0.5)
    k = chunked(k)
    v = chunked(v)
    g = chunked(g[..., None])[..., 0]
    beta = chunked(beta[..., None])[..., 0]
    G = jnp.cumsum(g, axis=-1)
    causal = jnp.tril(jnp.ones((c, c), dtype=bool))
    strict = jnp.tril(jnp.ones((c, c), dtype=bool), k=-1)
    decay = jnp.exp(jnp.where(causal, G[..., :, None] - G[..., None, :], -jnp.inf))
    a_mat = jnp.where(strict, beta[..., None] * jnp.einsum('bhnid,bhnjd->bhnij', k, k) * decay, 0.0)
    eye = jnp.eye(c, dtype=q.dtype)
    rhs = jnp.concatenate([v * beta[..., None], k * (beta * jnp.exp(G))[..., None]], axis=-1)
    sol = lax.linalg.triangular_solve(eye + a_mat, rhs, left_side=True, lower=True, unit_diagonal=True)
    w_v, w_k = sol[..., :dv], sol[..., dv:]
    qk = jnp.einsum('bhnid,bhnjd->bhnij', q, k) * decay
    q_dec = q * jnp.exp(G)[..., None]
    k_dec = k * jnp.exp(G[..., -1:] - G)[..., None]
    g_tot = jnp.exp(G[..., -1])

    def step(state, xs):
        wv_i, wk_i, qk_i, qd_i, kd_i, gt_i = xs
        v_new = wv_i - jnp.einsum('bhck,bhkv->bhcv', wk_i, state)
        o_i = jnp.einsum('bhck,bhkv->bhcv', qd_i, state) + jnp.einsum('bhcs,bhsv->bhcv', qk_i, v_new)
        state = state * gt_i[..., None, None] + jnp.einsum('bhck,bhcv->bhkv', kd_i, v_new)
        return state, o_i

    xs = tuple(jnp.moveaxis(t, 2, 0) for t in (w_v, w_k, qk, q_dec, k_dec, g_tot))
    s0 = jnp.zeros((bsz, nh, dk, dv), q.dtype)
    _, o = lax.scan(step, s0, xs)
    return jnp.transpose(o, (1, 0, 3, 2, 4)).reshape(bsz, seqlen, nh, dv)


def gdn_mixer(p_qkv, p_z, p_a, p_b, conv_w, a_log, dt_bias, norm_g):
    f32 = jnp.float32
    bsz, seqlen, _ = p_qkv.shape
    qkv = jax.nn.silu(causal_dwconv(p_qkv, conv_w)).astype(f32)
    shp = (bsz, seqlen, GDN_HEADS, GDN_HEAD_DIM)
    q = l2norm(qkv[..., :BRANCH_W].reshape(shp))
    k = l2norm(qkv[..., BRANCH_W:2 * BRANCH_W].reshape(shp))
    v = qkv[..., 2 * BRANCH_W:].reshape(shp)
    beta = jax.nn.sigmoid(p_b.astype(f32))
    g = -jnp.exp(a_log.astype(f32)) * jax.nn.softplus(p_a.astype(f32) + dt_bias.astype(f32))
    o = gated_delta_rule(q, k, v, g, beta)
    o = rmsnorm(o, norm_g) * jax.nn.silu(p_z.astype(f32).reshape(shp))
    return o.reshape(bsz, seqlen, BRANCH_W).astype(p_qkv.dtype)


def shortconv_mixer(p, conv_w):
    b_gate, c_gate, xin = p[..., :BRANCH_W], p[..., BRANCH_W:2 * BRANCH_W], p[..., 2 * BRANCH_W:]
    return b_gate * causal_dwconv(c_gate * xin, conv_w)


def hybrid_mixer(h, w_in, s5_lam_re, s5_lam_im, s5_log_step, s5_b_re, s5_b_im, s5_c_re, s5_c_im,
                 s5_d, s5_w_glu, s5_b_glu, s5_out_norm, sgu_ln_g, sgu_ln_b, sgu_w, sgu_b,
                 gmlp_out_norm, gdn_conv, gdn_a_log, gdn_dt_bias, gdn_norm, sc_conv, sc_out_norm, w_out):
    p = h @ w_in
    p_s5, p_gm, p_qkv, p_z, p_a, p_b, p_sc = split_cols(p, IN_SIZES)
    y_s5 = rmsnorm(s5_mixer(p_s5, s5_lam_re, s5_lam_im, s5_log_step, s5_b_re, s5_b_im,
                            s5_c_re, s5_c_im, s5_d, s5_w_glu, s5_b_glu), s5_out_norm)
    y_gm = rmsnorm(gmlp_mixer(p_gm, sgu_ln_g, sgu_ln_b, sgu_w, sgu_b), gmlp_out_norm)
    y_gdn = gdn_mixer(p_qkv, p_z, p_a, p_b, gdn_conv, gdn_a_log, gdn_dt_bias, gdn_norm)
    y_sc = rmsnorm(shortconv_mixer(p_sc, sc_conv), sc_out_norm)
    return jnp.concatenate([y_s5, y_gm, y_gdn, y_sc], axis=-1) @ w_out


def swiglu(x, w_gate, w_up, w_down):
    return (jax.nn.silu(x @ w_gate) * (x @ w_up)) @ w_down


def moe_swiglu(x2, w_router, w_gate, w_up, w_down):
    t = x2.shape[0]
    n_assign = t * TOP_K
    logits = (x2 @ w_router).astype(jnp.float32)
    top_logit, top_idx = lax.top_k(logits, TOP_K)
    gates = jax.nn.softmax(top_logit, axis=-1)
    flat_e = top_idx.reshape(-1)
    order = jnp.argsort(flat_e)
    sorted_e = flat_e[order]
    tok = order // TOP_K
    counts = jnp.bincount(flat_e, length=N_EXPERTS)
    starts = jnp.cumsum(counts) - counts
    padded = (counts + MOE_BLOCK - 1) // MOE_BLOCK * MOE_BLOCK
    pad_ends = jnp.cumsum(padded)
    pad_starts = pad_ends - padded
    dest = pad_starts[sorted_e] + (jnp.arange(n_assign) - starts[sorted_e])
    n_blocks = -(-n_assign // MOE_BLOCK) + N_EXPERTS
    buf = jnp.zeros((n_blocks * MOE_BLOCK, x2.shape[1]), x2.dtype).at[dest].set(x2[tok])
    block_e = jnp.minimum(jnp.searchsorted(pad_ends, jnp.arange(n_blocks) * MOE_BLOCK, side='right'),
                          N_EXPERTS - 1)

    def run(args):
        xb, e = args
        return swiglu(xb, w_gate[e], w_up[e], w_down[e])

    y_buf = lax.map(run, (buf.reshape(n_blocks, MOE_BLOCK, -1), block_e)).reshape(n_blocks * MOE_BLOCK, -1)
    y = y_buf[dest] * gates.reshape(-1)[order][:, None].astype(x2.dtype)
    return jnp.zeros_like(x2).at[tok].add(y)


def setup_inputs(seed: int = 0) -> dict:
    key = jax.random.key(seed)
    keys = iter(jax.random.split(key, 40))
    f32 = jnp.float32

    def nrm(shape, scale):
        return scale * jax.random.normal(next(keys), shape, f32)

    def gain(shape):
        return 1.0 + 0.02 * jax.random.normal(next(keys), shape, f32)

    L = DEPTH
    n_idx = jnp.arange(S5_STATE, dtype=f32)
    dt = jnp.exp(jax.random.uniform(next(keys), (L, GDN_HEADS), f32, math.log(1e-3), math.log(1e-1)))
    return {
        'x': nrm((BATCH, SEQ, D_MODEL), 1.0),
        'mix_norm': gain((L, D_MODEL)),
        'w_in': nrm((L, D_MODEL, IN_COLS), D_MODEL ** -0.5),
        's5_lam_re': -0.5 + nrm((L, S5_GROUPS, S5_STATE), 0.01),
        's5_lam_im': math.pi * n_idx + nrm((L, S5_GROUPS, S5_STATE), 0.01),
        's5_log_step': jax.random.uniform(next(keys), (L, S5_GROUPS), f32, math.log(1e-3), math.log(1e-1)),
        's5_b_re': nrm((L, S5_GROUPS, S5_STATE, S5_GROUP), (2 * S5_GROUP) ** -0.5),
        's5_b_im': nrm((L, S5_GROUPS, S5_STATE, S5_GROUP), (2 * S5_GROUP) ** -0.5),
        's5_c_re': nrm((L, S5_GROUPS, S5_GROUP, S5_STATE), 0.5),
        's5_c_im': nrm((L, S5_GROUPS, S5_GROUP, S5_STATE), 0.5),
        's5_d': nrm((L, BRANCH_W), 1.0),
        's5_w_glu': nrm((L, BRANCH_W, BRANCH_W), BRANCH_W ** -0.5),
        's5_b_glu': nrm((L, BRANCH_W), 0.02),
        's5_out_norm': gain((L, BRANCH_W)),
        'sgu_ln_g': gain((L, BRANCH_W)),
        'sgu_ln_b': nrm((L, BRANCH_W), 0.02),
        'sgu_w': nrm((L, GMLP_HEADS, GMLP_CHUNK, GMLP_CHUNK), 0.5 * GMLP_CHUNK ** -0.5),
        'sgu_b': 1.0 + nrm((L, GMLP_HEADS, GMLP_CHUNK), 0.1),
        'gmlp_out_norm': gain((L, BRANCH_W)),
        'gdn_conv': nrm((L, GDN_CONV, 3 * BRANCH_W), GDN_CONV ** -0.5),
        'gdn_a_log': jnp.log(jax.random.uniform(next(keys), (L, GDN_HEADS), f32, 1.0, 16.0)),
        'gdn_dt_bias': dt + jnp.log(-jnp.expm1(-dt)),
        'gdn_norm': gain((L, GDN_HEAD_DIM)),
        'sc_conv': nrm((L, SC_CONV, BRANCH_W), SC_CONV ** -0.5),
        'sc_out_norm': gain((L, BRANCH_W)),
        'w_out': nrm((L, MIX_WIDTH, D_MODEL), MIX_WIDTH ** -0.5),
        'ffn_norm': gain((L, D_MODEL)),
        'ffn_w_gate': nrm((N_DENSE, D_MODEL, D_FF), D_MODEL ** -0.5),
        'ffn_w_up': nrm((N_DENSE, D_MODEL, D_FF), D_MODEL ** -0.5),
        'ffn_w_down': nrm((N_DENSE, D_FF, D_MODEL), D_FF ** -0.5),
        'moe_router': nrm((N_MOE, D_MODEL, N_EXPERTS), D_MODEL ** -0.5),
        'moe_w_gate': nrm((N_MOE, N_EXPERTS, D_MODEL, D_FF_EXPERT), D_MODEL ** -0.5),
        'moe_w_up': nrm((N_MOE, N_EXPERTS, D_MODEL, D_FF_EXPERT), D_MODEL ** -0.5),
        'moe_w_down': nrm((N_MOE, N_EXPERTS, D_FF_EXPERT, D_MODEL), D_FF_EXPERT ** -0.5),
        'final_norm': gain((D_MODEL,)),
    }


def reference(x, mix_norm, w_in, s5_lam_re, s5_lam_im, s5_log_step, s5_b_re, s5_b_im, s5_c_re, s5_c_im,
              s5_d, s5_w_glu, s5_b_glu, s5_out_norm, sgu_ln_g, sgu_ln_b, sgu_w, sgu_b, gmlp_out_norm,
              gdn_conv, gdn_a_log, gdn_dt_bias, gdn_norm, sc_conv, sc_out_norm, w_out, ffn_norm,
              ffn_w_gate, ffn_w_up, ffn_w_down, moe_router, moe_w_gate, moe_w_up, moe_w_down, final_norm):
    bsz, seqlen, d = x.shape
    for l in range(DEPTH):
        h = rmsnorm(x, mix_norm[l])
        x = x + hybrid_mixer(h, w_in[l], s5_lam_re[l], s5_lam_im[l], s5_log_step[l], s5_b_re[l], s5_b_im[l],
                             s5_c_re[l], s5_c_im[l], s5_d[l], s5_w_glu[l], s5_b_glu[l], s5_out_norm[l],
                             sgu_ln_g[l], sgu_ln_b[l], sgu_w[l], sgu_b[l], gmlp_out_norm[l],
                             gdn_conv[l], gdn_a_log[l], gdn_dt_bias[l], gdn_norm[l],
                             sc_conv[l], sc_out_norm[l], w_out[l])
        h = rmsnorm(x, ffn_norm[l])
        i = l // 2
        if l % 2 == 0:
            x = x + swiglu(h, ffn_w_gate[i], ffn_w_up[i], ffn_w_down[i])
        else:
            x = x + moe_swiglu(h.reshape(bsz * seqlen, d), moe_router[i], moe_w_gate[i],
                               moe_w_up[i], moe_w_down[i]).reshape(bsz, seqlen, d)
    return rmsnorm(x, final_norm)
```

```python
import functools
import math

import jax
import jax.numpy as jnp
from jax import lax
from jax.experimental import pallas as pl
from jax.experimental.pallas import tpu as pltpu

F32 = jnp.float32
BF16 = jnp.bfloat16
HIGHEST = lax.Precision.HIGHEST

D_MODEL = 1024
DEPTH = 2
BRANCH_W = 256
S5_GROUP = 16
S5_GROUPS = 16
S5_STATE = 64
S5_NSTATE = S5_GROUPS * S5_STATE
GMLP_HEADS = 4
GMLP_HEAD_DIM = 64
GMLP_CHUNK = 128
GDN_HEAD_DIM = 64
GDN_HEADS = 4
GDN_CONV = 4
GDN_CHUNK = 64
SC_CONV = 3
D_FF = 2816
N_EXPERTS = 8
TOP_K = 2
D_FF_EXPERT = 3584
EPS = 1e-6

LANES = 128
SUBLANES = 8
VMEM_BYTES_V7X = 64 * 1024 * 1024

COL_GM = 0
COL_S5 = 512
COL_Z = 768
COL_Q = 1024
COL_K = 1280
COL_V = 1536
COL_SCB = 1792
COL_SCC = 2048
COL_SCX = 2304
COL_AB = 2560
P_COLS = 2688
AB_B_LANE = 64

IN_TM = 512
S5_CHUNK = 128
GM_TILE = 512
SC_TILE = 512
GDN_TILE = 256
OUT_TM = 512
FFN_TM = 512
FFN_TF = 256
MOE_TM = 512
MOE_TF = 512


def _cparams(sem, vmem_mb):
    return pltpu.CompilerParams(dimension_semantics=sem, vmem_limit_bytes=vmem_mb * 1024 * 1024)


def _rms(x, g):
    return x * lax.rsqrt(jnp.mean(x * x, axis=-1, keepdims=True) + EPS) * g


def _silu(x):
    return x * jax.nn.sigmoid(x)


def _bdot(a, b):
    return jnp.dot(a.astype(BF16), b.astype(BF16), preferred_element_type=F32)


def _bdot_nt(a, b):
    return lax.dot_general(a.astype(BF16), b.astype(BF16), (((1,), (1,)), ((), ())),
                           preferred_element_type=F32)


def _in_proj_kernel(x_ref, g_ref, w_ref, o_ref):
    h = _rms(x_ref[...], g_ref[...])
    o_ref[...] = _bdot(h, w_ref[...])


def _in_proj(x2, g, w):
    t = x2.shape[0]
    return pl.pallas_call(
        _in_proj_kernel,
        grid=(t // IN_TM,),
        in_specs=[pl.BlockSpec((IN_TM, D_MODEL), lambda i: (i, 0)),
                  pl.BlockSpec((1, D_MODEL), lambda i: (0, 0)),
                  pl.BlockSpec((D_MODEL, P_COLS), lambda i: (0, 0))],
        out_specs=pl.BlockSpec((IN_TM, P_COLS), lambda i: (i, 0)),
        out_shape=jax.ShapeDtypeStruct((t, P_COLS), F32),
        compiler_params=_cparams(("parallel",), 48),
        name="in_proj",
    )(x2, g, w)


def _s5_kernel(u_ref, wb_ref, pw_ref, wc_ref, d_ref, wglu_ref, bglu_ref, on_ref, o_ref, sr_ref, si_ref):
    n = S5_NSTATE

    @pl.when(pl.program_id(1) == 0)
    def _():
        sr_ref[...] = jnp.zeros_like(sr_ref)
        si_ref[...] = jnp.zeros_like(si_ref)

    u = u_ref[...]
    bu = _bdot(u, wb_ref[...])
    xr, xi = bu[:, :n], bu[:, n:]
    row = lax.broadcasted_iota(jnp.int32, (S5_CHUNK, n), 0)
    a1r, a1i = pw_ref[0:1, :n], pw_ref[0:1, n:]
    sr, si = sr_ref[...], si_ref[...]
    first = row == 0
    xr = xr + jnp.where(first, a1r * sr - a1i * si, 0.0)
    xi = xi + jnp.where(first, a1r * si + a1i * sr, 0.0)
    d = 1
    s = 0
    while d < S5_CHUNK:
        ar, ai = pw_ref[s:s + 1, :n], pw_ref[s:s + 1, n:]
        keep = row >= d
        shr = jnp.where(keep, pltpu.roll(xr, d, 0), 0.0)
        shi = jnp.where(keep, pltpu.roll(xi, d, 0), 0.0)
        xr, xi = xr + ar * shr - ai * shi, xi + ar * shi + ai * shr
        d *= 2
        s += 1
    sr_ref[...] = xr[S5_CHUNK - 1:, :]
    si_ref[...] = xi[S5_CHUNK - 1:, :]
    y = _bdot(xr, wc_ref[:n, :]) + _bdot(xi, wc_ref[n:, :])
    y = y + d_ref[...] * u
    y = jax.nn.gelu(y)
    y = y * jax.nn.sigmoid(_bdot(y, wglu_ref[...]) + bglu_ref[...])
    o_ref[...] = _rms(y, on_ref[...])


def _s5_params(lam_re, lam_im, log_step, b_re, b_im, c_re, c_im):
    g, n, p = S5_GROUPS, S5_STATE, S5_GROUP
    dt = jnp.exp(log_step)[:, None]
    mag = jnp.exp(lam_re * dt)
    ar, ai = mag * jnp.cos(lam_im * dt), mag * jnp.sin(lam_im * dt)
    den = lam_re * lam_re + lam_im * lam_im
    fr = ((ar - 1.0) * lam_re + ai * lam_im) / den
    fi = (ai * lam_re - (ar - 1.0) * lam_im) / den
    bbr = fr[..., None] * b_re - fi[..., None] * b_im
    bbi = fr[..., None] * b_im + fi[..., None] * b_re
    eye = jnp.eye(g, dtype=F32)
    wbr = jnp.einsum('gnp,gh->gphn', bbr, eye).reshape(g * p, g * n)
    wbi = jnp.einsum('gnp,gh->gphn', bbi, eye).reshape(g * p, g * n)
    wb = jnp.concatenate([wbr, wbi], axis=1)
    wcr = jnp.einsum('gpn,gh->gnhp', c_re, eye).reshape(g * n, g * p)
    wci = jnp.einsum('gpn,gh->gnhp', -c_im, eye).reshape(g * n, g * p)
    wc = jnp.concatenate([wcr, wci], axis=0)
    rows = []
    pr, pi = ar.reshape(1, g * n), ai.reshape(1, g * n)
    d = 1
    while d < S5_CHUNK:
        rows.append(jnp.concatenate([pr, pi], axis=1))
        pr, pi = pr * pr - pi * pi, 2.0 * pr * pi
        d *= 2
    while len(rows) < SUBLANES:
        rows.append(jnp.zeros_like(rows[0]))
    pw = jnp.concatenate(rows, axis=0)
    return wb.astype(BF16), wc.astype(BF16), pw


def _s5_mixer(p3, wb, wc, pw, d_skip, w_glu, b_glu, out_norm):
    b, l, _ = p3.shape
    n2 = 2 * S5_NSTATE
    const = lambda shape: pl.BlockSpec(shape, lambda i, j: (0,) * len(shape))
    return pl.pallas_call(
        _s5_kernel,
        grid=(b, l // S5_CHUNK),
        in_specs=[pl.BlockSpec((None, S5_CHUNK, BRANCH_W), lambda i, j: (i, j, COL_S5 // BRANCH_W)),
                  const((BRANCH_W, n2)), const((SUBLANES, n2)), const((n2, BRANCH_W)),
                  const((1, BRANCH_W)), const((BRANCH_W, BRANCH_W)), const((1, BRANCH_W)),
                  const((1, BRANCH_W))],
        out_specs=pl.BlockSpec((None, S5_CHUNK, BRANCH_W), lambda i, j: (i, j, 0)),
        out_shape=jax.ShapeDtypeStruct((b, l, BRANCH_W), F32),
        scratch_shapes=[pltpu.VMEM((1, S5_NSTATE), F32), pltpu.VMEM((1, S5_NSTATE), F32)],
        compiler_params=_cparams(("parallel", "arbitrary"), 40),
        name="s5_mixer",
    )(p3, wb, pw, wc, d_skip.reshape(1, -1), w_glu.astype(BF16), b_glu.reshape(1, -1),
      out_norm.reshape(1, -1))


def _gmlp_kernel(p_ref, lng_ref, lnb_ref, w_ref, bias_ref, on_ref, o_ref):
    z = jax.nn.gelu(p_ref[...])
    u, v = z[:, :BRANCH_W], z[:, BRANCH_W:]
    vc = v - jnp.mean(v, axis=-1, keepdims=True)
    v = vc * lax.rsqrt(jnp.mean(vc * vc, axis=-1, keepdims=True) + EPS) * lng_ref[...] + lnb_ref[...]
    ti = lax.broadcasted_iota(jnp.int32, (GMLP_CHUNK, GMLP_CHUNK), 0)
    si = lax.broadcasted_iota(jnp.int32, (GMLP_CHUNK, GMLP_CHUNK), 1)
    tril = ti >= si
    ws = [jnp.where(tril, w_ref[h], 0.0).astype(BF16) for h in range(GMLP_HEADS)]
    lane = lax.broadcasted_iota(jnp.int32, (GMLP_CHUNK, BRANCH_W), 1)
    bias = bias_ref[...]
    outs = []
    for c in range(GM_TILE // GMLP_CHUNK):
        vb = v[c * GMLP_CHUNK:(c + 1) * GMLP_CHUNK, :].astype(BF16)
        s = jnp.dot(ws[GMLP_HEADS - 1], vb, preferred_element_type=F32)
        for h in range(GMLP_HEADS - 2, -1, -1):
            sh = jnp.dot(ws[h], vb, preferred_element_type=F32)
            s = jnp.where(lane < (h + 1) * GMLP_HEAD_DIM, sh, s)
        outs.append(s + bias)
    s = jnp.concatenate(outs, axis=0)
    o_ref[...] = _rms(u * s, on_ref[...])


def _gmlp_mixer(p3, ln_g, ln_b, w_sp, b_sp, out_norm):
    b, l, _ = p3.shape
    bias = jnp.repeat(b_sp.T, GMLP_HEAD_DIM, axis=1)
    const = lambda shape: pl.BlockSpec(shape, lambda i, j: (0,) * len(shape))
    return pl.pallas_call(
        _gmlp_kernel,
        grid=(b, l // GM_TILE),
        in_specs=[pl.BlockSpec((None, GM_TILE, 2 * BRANCH_W), lambda i, j: (i, j, COL_GM // (2 * BRANCH_W))),
                  const((1, BRANCH_W)), const((1, BRANCH_W)),
                  const((GMLP_HEADS, GMLP_CHUNK, GMLP_CHUNK)), const((GMLP_CHUNK, BRANCH_W)),
                  const((1, BRANCH_W))],
        out_specs=pl.BlockSpec((None, GM_TILE, BRANCH_W), lambda i, j: (i, j, 0)),
        out_shape=jax.ShapeDtypeStruct((b, l, BRANCH_W), F32),
        compiler_params=_cparams(("parallel", "parallel"), 32),
        name="gmlp_mixer",
    )(p3, ln_g.reshape(1, -1), ln_b.reshape(1, -1), w_sp, bias, out_norm.reshape(1, -1))


def _shortconv_kernel(b_ref, c_ref, x_ref, w_ref, on_ref, o_ref, halo_ref):
    @pl.when(pl.program_id(1) == 0)
    def _():
        halo_ref[...] = jnp.zeros_like(halo_ref)

    cx = c_ref[...] * x_ref[...]
    ext = jnp.concatenate([halo_ref[...], cx], axis=0)
    halo_ref[...] = cx[SC_TILE - SUBLANES:, :]
    y = w_ref[SC_CONV - 1:SC_CONV, :] * cx
    for j in range(SC_CONV - 1):
        sh = SC_CONV - 1 - j
        y = y + w_ref[j:j + 1, :] * pltpu.roll(ext, sh, 0)[SUBLANES:, :]
    o_ref[...] = _rms(b_ref[...] * y, on_ref[...])


def _shortconv_mixer(p3, conv_w, out_norm):
    b, l, _ = p3.shape
    w = jnp.concatenate([conv_w, jnp.zeros((SUBLANES - SC_CONV, BRANCH_W), F32)], axis=0)
    col = lambda c: pl.BlockSpec((None, SC_TILE, BRANCH_W), lambda i, j: (i, j, c // BRANCH_W))
    const = lambda shape: pl.BlockSpec(shape, lambda i, j: (0,) * len(shape))
    return pl.pallas_call(
        _shortconv_kernel,
        grid=(b, l // SC_TILE),
        in_specs=[col(COL_SCB), col(COL_SCC), col(COL_SCX), const((SUBLANES, BRANCH_W)), const((1, BRANCH_W))],
        out_specs=pl.BlockSpec((None, SC_TILE, BRANCH_W), lambda i, j: (i, j, 0)),
        out_shape=jax.ShapeDtypeStruct((b, l, BRANCH_W), F32),
        scratch_shapes=[pltpu.VMEM((SUBLANES, BRANCH_W), F32)],
        compiler_params=_cparams(("parallel", "arbitrary"), 32),
        name="shortconv_mixer",
    )(p3, p3, p3, w, out_norm.reshape(1, -1))


def _gdn_kernel(q_ref, k_ref, v_ref, z_ref, ab_ref, cw_ref, alog_ref, dtb_ref, ng_ref, o_ref,
                halo_ref, state_ref):
    c_len, h_dim, nh, w = GDN_CHUNK, GDN_HEAD_DIM, GDN_HEADS, BRANCH_W
    n_chunks = GDN_TILE // c_len

    @pl.when(pl.program_id(1) == 0)
    def _():
        halo_ref[...] = jnp.zeros_like(halo_ref)
        state_ref[...] = jnp.zeros_like(state_ref)

    shift = int(math.log2(h_dim))
    r256 = lax.broadcasted_iota(jnp.int32, (w, w), 0)
    c256 = lax.broadcasted_iota(jnp.int32, (w, w), 1)
    same_head = (r256 >> shift) == (c256 >> shift)
    ebd = jnp.where(same_head, 1.0, 0.0).astype(BF16)
    causal = same_head & (r256 >= c256)
    strict = same_head & (r256 > c256)
    eye = jnp.where(r256 == c256, 1.0, 0.0)
    r128 = lax.broadcasted_iota(jnp.int32, (LANES, w), 0)
    c128 = lax.broadcasted_iota(jnp.int32, (LANES, w), 1)
    exp_a = jnp.where(r128 == (c128 >> shift), 1.0, 0.0)
    exp_b = jnp.where(r128 == (c128 >> shift) + AB_B_LANE, 1.0, 0.0)

    def stack(a):
        lane_head = (lax.broadcasted_iota(jnp.int32, a.shape, 1) >> shift) & (nh - 1)
        return jnp.concatenate([jnp.where(lane_head == h, a, 0.0) for h in range(nh)], axis=0)

    def unstack(a):
        return a[0:c_len] + a[c_len:2 * c_len] + a[2 * c_len:3 * c_len] + a[3 * c_len:4 * c_len]

    def conv_silu(ref, col):
        cur = ref[...]
        ext = jnp.concatenate([halo_ref[:, col * w:(col + 1) * w], cur], axis=0)
        halo_ref[:, col * w:(col + 1) * w] = cur[GDN_TILE - SUBLANES:, :]
        y = cw_ref[GDN_CONV - 1:GDN_CONV, col * w:(col + 1) * w] * cur
        for j in range(GDN_CONV - 1):
            sh = GDN_CONV - 1 - j
            y = y + cw_ref[j:j + 1, col * w:(col + 1) * w] * pltpu.roll(ext, sh, 0)[SUBLANES:, :]
        return _silu(y)

    q = conv_silu(q_ref, 0)
    k = conv_silu(k_ref, 1)
    v = conv_silu(v_ref, 2)
    q = q * lax.rsqrt(jnp.dot((q * q).astype(BF16), ebd, preferred_element_type=F32) + EPS) * (h_dim ** -0.5)
    k = k * lax.rsqrt(jnp.dot((k * k).astype(BF16), ebd, preferred_element_type=F32) + EPS)

    ab = ab_ref[...]
    beta = jax.nn.sigmoid(ab)
    xa = ab + dtb_ref[...]
    softplus = jnp.maximum(xa, 0.0) + jnp.log(1.0 + jnp.exp(-jnp.abs(xa)))
    g = -jnp.exp(alog_ref[...]) * softplus
    rt = lax.broadcasted_iota(jnp.int32, (GDN_TILE, GDN_TILE), 0)
    ct = lax.broadcasted_iota(jnp.int32, (GDN_TILE, GDN_TILE), 1)
    ltri = jnp.where((rt // c_len == ct // c_len) & (rt >= ct), 1.0, 0.0)
    gcum = jnp.dot(ltri, g, precision=HIGHEST, preferred_element_type=F32)
    gexp = jnp.dot(gcum, exp_a, precision=HIGHEST, preferred_element_type=F32)
    bexp = jnp.dot(beta, exp_b, precision=HIGHEST, preferred_element_type=F32)

    outs = []
    state = state_ref[...]
    for c in range(n_chunks):
        sl = slice(c * c_len, (c + 1) * c_len)
        qc, kc, vc, gc, bc = q[sl], k[sl], v[sl], gexp[sl], bexp[sl]
        glast = gc[c_len - 1:c_len, :]
        eg = jnp.exp(gc)
        q_dec = qc * eg
        k_dec = kc * jnp.exp(glast - gc)
        g_tot = jnp.exp(glast)
        rhs = jnp.concatenate([vc * bc, kc * bc * eg], axis=1)
        kst = stack(kc)
        qst = stack(qc)
        kk = _bdot_nt(kst, kst)
        qk = _bdot_nt(qst, kst)
        gcol = stack(gc)
        grow = gcol.T
        bcol = stack(bc)
        dec = jnp.exp(jnp.where(causal, gcol - grow, -jnp.inf))
        a_mat = jnp.where(strict, bcol * kk * dec, 0.0)
        qkd = qk * dec
        t_inv = eye - a_mat
        pw = a_mat
        n_sq = int(math.log2(c_len)) - 1
        for _ in range(n_sq):
            pw = _bdot(pw, pw)
            t_inv = t_inv + _bdot(t_inv, pw)
        sol = unstack(_bdot(t_inv, stack(rhs)))
        w_v, w_k = sol[:, :w], sol[:, w:]
        v_new = w_v - _bdot(w_k, state)
        o = _bdot(q_dec, state) + unstack(_bdot(qkd, stack(v_new)))
        state = state * g_tot + jnp.where(same_head, _bdot(k_dec.T, v_new), 0.0)
        outs.append(o)
    state_ref[...] = state
    o = jnp.concatenate(outs, axis=0)
    ms = jnp.dot((o * o).astype(BF16), ebd, preferred_element_type=F32) * (1.0 / h_dim)
    o = o * lax.rsqrt(ms + EPS) * ng_ref[...]
    o_ref[...] = o * _silu(z_ref[...])


def _gdn_mixer(p3, conv_w, a_log, dt_bias, norm_g):
    b, l, _ = p3.shape
    cw = jnp.concatenate([conv_w, jnp.zeros((SUBLANES - GDN_CONV, 3 * BRANCH_W), F32)], axis=0)
    lane_row = lambda vec: jnp.zeros((1, LANES), F32).at[0, :GDN_HEADS].set(vec)
    col = lambda c: pl.BlockSpec((None, GDN_TILE, BRANCH_W), lambda i, j: (i, j, c // BRANCH_W))
    const = lambda shape: pl.BlockSpec(shape, lambda i, j: (0,) * len(shape))
    return pl.pallas_call(
        _gdn_kernel,
        grid=(b, l // GDN_TILE),
        in_specs=[col(COL_Q), col(COL_K), col(COL_V), col(COL_Z),
                  pl.BlockSpec((None, GDN_TILE, LANES), lambda i, j: (i, j, COL_AB // LANES)),
                  const((SUBLANES, 3 * BRANCH_W)), const((1, LANES)), const((1, LANES)), const((1, BRANCH_W))],
        out_specs=pl.BlockSpec((None, GDN_TILE, BRANCH_W), lambda i, j: (i, j, 0)),
        out_shape=jax.ShapeDtypeStruct((b, l, BRANCH_W), F32),
        scratch_shapes=[pltpu.VMEM((SUBLANES, 3 * BRANCH_W), F32), pltpu.VMEM((BRANCH_W, BRANCH_W), F32)],
        compiler_params=_cparams(("parallel", "arbitrary"), 48),
        name="gdn_mixer",
    )(p3, p3, p3, p3, p3, cw, lane_row(a_log), lane_row(dt_bias), jnp.tile(norm_g, GDN_HEADS).reshape(1, -1))


def _out_proj_kernel(x_ref, y0_ref, y1_ref, y2_ref, y3_ref, w_ref, o_ref):
    acc = x_ref[...]
    for i, y_ref in enumerate((y0_ref, y1_ref, y2_ref, y3_ref)):
        acc = acc + _bdot(y_ref[...], w_ref[i * BRANCH_W:(i + 1) * BRANCH_W, :])
    o_ref[...] = acc


def _out_proj(x2, ys, w_out):
    t = x2.shape[0]
    row = lambda n: pl.BlockSpec((OUT_TM, n), lambda i: (i, 0))
    return pl.pallas_call(
        _out_proj_kernel,
        grid=(t // OUT_TM,),
        in_specs=[row(D_MODEL)] + [row(BRANCH_W)] * 4 + [pl.BlockSpec((D_MODEL, D_MODEL), lambda i: (0, 0))],
        out_specs=row(D_MODEL),
        out_shape=jax.ShapeDtypeStruct((t, D_MODEL), F32),
        compiler_params=_cparams(("parallel",), 32),
        name="out_proj",
    )(x2, *ys, w_out.astype(BF16))


def _ffn_kernel(x_ref, g_ref, wg_ref, wu_ref, wd_ref, o_ref):
    x = x_ref[...]
    hb = _rms(x, g_ref[...]).astype(BF16)

    def body(f, acc):
        a = jnp.dot(hb, wg_ref[f], preferred_element_type=F32)
        u = jnp.dot(hb, wu_ref[f], preferred_element_type=F32)
        return acc + jnp.dot((_silu(a) * u).astype(BF16), wd_ref[f], preferred_element_type=F32)

    o_ref[...] = lax.fori_loop(0, D_FF // FFN_TF, body, x)


def _ffn_dense(x2, g, w_gate, w_up, w_down):
    t = x2.shape[0]
    nf = D_FF // FFN_TF
    wg = w_gate.astype(BF16).reshape(D_MODEL, nf, FFN_TF).transpose(1, 0, 2)
    wu = w_up.astype(BF16).reshape(D_MODEL, nf, FFN_TF).transpose(1, 0, 2)
    wd = w_down.astype(BF16).reshape(nf, FFN_TF, D_MODEL)
    const3 = lambda shape: pl.BlockSpec(shape, lambda i: (0, 0, 0))
    return pl.pallas_call(
        _ffn_kernel,
        grid=(t // FFN_TM,),
        in_specs=[pl.BlockSpec((FFN_TM, D_MODEL), lambda i: (i, 0)),
                  pl.BlockSpec((1, D_MODEL), lambda i: (0, 0)),
                  const3((nf, D_MODEL, FFN_TF)), const3((nf, D_MODEL, FFN_TF)), const3((nf, FFN_TF, D_MODEL))],
        out_specs=pl.BlockSpec((FFN_TM, D_MODEL), lambda i: (i, 0)),
        out_shape=jax.ShapeDtypeStruct((t, D_MODEL), F32),
        compiler_params=_cparams(("parallel",), 56),
        name="ffn_dense",
    )(x2, g.reshape(1, -1), wg, wu, wd)


def _moe_kernel(be_ref, nu_ref, x_ref, wg_ref, wu_ref, wd_ref, o_ref, acc_ref):
    m, f = pl.program_id(0), pl.program_id(1)

    @pl.when(f == 0)
    def _():
        acc_ref[...] = jnp.zeros_like(acc_ref)

    @pl.when(m < nu_ref[0])
    def _():
        xb = x_ref[...]
        a = jnp.dot(xb, wg_ref[...], preferred_element_type=F32)
        u = jnp.dot(xb, wu_ref[...], preferred_element_type=F32)
        acc_ref[...] += jnp.dot((_silu(a) * u).astype(BF16), wd_ref[...], preferred_element_type=F32)

    @pl.when(f == pl.num_programs(1) - 1)
    def _():
        o_ref[...] = acc_ref[...]


def _moe_experts(buf, block_e, n_used, wg, wu, wd):
    rows = buf.shape[0]
    grid_spec = pltpu.PrefetchScalarGridSpec(
        num_scalar_prefetch=2,
        grid=(rows // MOE_TM, D_FF_EXPERT // MOE_TF),
        in_specs=[pl.BlockSpec((MOE_TM, D_MODEL), lambda m, f, be, nu: (m, 0)),
                  pl.BlockSpec((None, D_MODEL, MOE_TF), lambda m, f, be, nu: (be[m], 0, f)),
                  pl.BlockSpec((None, D_MODEL, MOE_TF), lambda m, f, be, nu: (be[m], 0, f)),
                  pl.BlockSpec((None, MOE_TF, D_MODEL), lambda m, f, be, nu: (be[m], f, 0))],
        out_specs=pl.BlockSpec((MOE_TM, D_MODEL), lambda m, f, be, nu: (m, 0)),
        scratch_shapes=[pltpu.VMEM((MOE_TM, D_MODEL), F32)],
    )
    return pl.pallas_call(
        _moe_kernel,
        grid_spec=grid_spec,
        out_shape=jax.ShapeDtypeStruct((rows, D_MODEL), F32),
        compiler_params=_cparams(("parallel", "arbitrary"), 48),
        name="moe_experts",
    )(block_e, n_used, buf, wg, wu, wd)


def _norm_kernel(x_ref, g_ref, o_ref):
    o_ref[...] = _rms(x_ref[...], g_ref[...]).astype(o_ref.dtype)


def _rmsnorm_rows(x2, g, dtype):
    t = x2.shape[0]
    return pl.pallas_call(
        _norm_kernel,
        grid=(t // 1024,),
        in_specs=[pl.BlockSpec((1024, D_MODEL), lambda i: (i, 0)), pl.BlockSpec((1, D_MODEL), lambda i: (0, 0))],
        out_specs=pl.BlockSpec((1024, D_MODEL), lambda i: (i, 0)),
        out_shape=jax.ShapeDtypeStruct((t, D_MODEL), dtype),
        compiler_params=_cparams(("parallel",), 32),
        name="rmsnorm_rows",
    )(x2, g.reshape(1, -1))


def _combine_kernel(x_ref, y0_ref, y1_ref, g0_ref, g1_ref, fn_ref, o_ref):
    x = x_ref[...] + g0_ref[...] * y0_ref[...] + g1_ref[...] * y1_ref[...]
    o_ref[...] = _rms(x, fn_ref[...])


def _moe_combine_norm(x2, y0, y1, g0, g1, final_norm):
    t = x2.shape[0]
    row = lambda n: pl.BlockSpec((512, n), lambda i: (i, 0))
    return pl.pallas_call(
        _combine_kernel,
        grid=(t // 512,),
        in_specs=[row(D_MODEL), row(D_MODEL), row(D_MODEL), row(1), row(1),
                  pl.BlockSpec((1, D_MODEL), lambda i: (0, 0))],
        out_specs=row(D_MODEL),
        out_shape=jax.ShapeDtypeStruct((t, D_MODEL), F32),
        compiler_params=_cparams(("parallel",), 32),
        name="moe_combine_norm",
    )(x2, y0, y1, g0, g1, final_norm.reshape(1, -1))


def _moe_layer(x2, ffn_norm, w_router, w_gate, w_up, w_down, final_norm):
    t = x2.shape[0]
    n_assign = t * TOP_K
    h = _rmsnorm_rows(x2, ffn_norm, F32)
    logits = jnp.dot(h, w_router, precision=HIGHEST)
    top_logit, top_idx = lax.top_k(logits, TOP_K)
    gates = jax.nn.softmax(top_logit, axis=-1)
    flat_e = top_idx.reshape(-1)
    order = jnp.argsort(flat_e)
    sorted_e = flat_e[order]
    tok = order // TOP_K
    counts = jnp.bincount(flat_e, length=N_EXPERTS)
    starts = jnp.cumsum(counts) - counts
    padded = (counts + MOE_TM - 1) // MOE_TM * MOE_TM
    pad_ends = jnp.cumsum(padded)
    pad_starts = pad_ends - padded
    dest = pad_starts[sorted_e] + (jnp.arange(n_assign) - starts[sorted_e])
    n_blocks = -(-n_assign // MOE_TM) + N_EXPERTS
    buf = jnp.zeros((n_blocks * MOE_TM, D_MODEL), BF16).at[dest].set(h.astype(BF16)[tok])
    block_e = jnp.minimum(jnp.searchsorted(pad_ends, jnp.arange(n_blocks) * MOE_TM, side='right'),
                          N_EXPERTS - 1).astype(jnp.int32)
    n_used = (pad_ends[-1] // MOE_TM).astype(jnp.int32).reshape(1)
    y_buf = _moe_experts(buf, block_e, n_used, w_gate.astype(BF16), w_up.astype(BF16), w_down.astype(BF16))
    dest_of = jnp.zeros((n_assign,), jnp.int32).at[order].set(dest.astype(jnp.int32)).reshape(t, TOP_K)
    y0 = y_buf[dest_of[:, 0]]
    y1 = y_buf[dest_of[:, 1]]
    return _moe_combine_norm(x2, y0, y1, gates[:, 0:1], gates[:, 1:2], final_norm)


def _permute_w_in(w_in):
    s5 = w_in[:, 0:256]
    gm = w_in[:, 256:768]
    qkv = w_in[:, 768:1536]
    z = w_in[:, 1536:1792]
    a = w_in[:, 1792:1796]
    b = w_in[:, 1796:1800]
    sc = w_in[:, 1800:2568]
    ab = jnp.zeros((D_MODEL, LANES), F32).at[:, 0:GDN_HEADS].set(a).at[:, AB_B_LANE:AB_B_LANE + GDN_HEADS].set(b)
    return jnp.concatenate([gm, s5, z, qkv, sc, ab], axis=1).astype(BF16)


def kernel(x, mix_norm, w_in, s5_lam_re, s5_lam_im, s5_log_step, s5_b_re, s5_b_im, s5_c_re, s5_c_im, s5_d, s5_w_glu, s5_b_glu, s5_out_norm, sgu_ln_g, sgu_ln_b, sgu_w, sgu_b, gmlp_out_norm, gdn_conv, gdn_a_log, gdn_dt_bias, gdn_norm, sc_conv, sc_out_norm, w_out, ffn_norm, ffn_w_gate, ffn_w_up, ffn_w_down, moe_router, moe_w_gate, moe_w_up, moe_w_down, final_norm):
    bsz, seqlen, d = x.shape
    t = bsz * seqlen
    x2 = x.reshape(t, d)
    out = None
    for l in range(DEPTH):
        p2 = _in_proj(x2, mix_norm[l].reshape(1, -1), _permute_w_in(w_in[l]))
        p3 = p2.reshape(bsz, seqlen, P_COLS)
        wb, wc, pw = _s5_params(s5_lam_re[l], s5_lam_im[l], s5_log_step[l], s5_b_re[l], s5_b_im[l],
                                s5_c_re[l], s5_c_im[l])
        y_s5 = _s5_mixer(p3, wb, wc, pw, s5_d[l], s5_w_glu[l], s5_b_glu[l], s5_out_norm[l])
        y_gm = _gmlp_mixer(p3, sgu_ln_g[l], sgu_ln_b[l], sgu_w[l], sgu_b[l], gmlp_out_norm[l])
        y_gdn = _gdn_mixer(p3, gdn_conv[l], gdn_a_log[l], gdn_dt_bias[l], gdn_norm[l])
        y_sc = _shortconv_mixer(p3, sc_conv[l], sc_out_norm[l])
        ys = [y.reshape(t, BRANCH_W) for y in (y_s5, y_gm, y_gdn, y_sc)]
        x2 = _out_proj(x2, ys, w_out[l])
        i = l // 2
        if l % 2 == 0:
            x2 = _ffn_dense(x2, ffn_norm[l], ffn_w_gate[i], ffn_w_up[i], ffn_w_down[i])
        else:
            out = _moe_layer(x2, ffn_norm[l], moe_router[i], moe_w_gate[i], moe_w_up[i], moe_w_down[i],
                             final_norm)
    return out.reshape(bsz, seqlen, d)
```

```python
import functools
import math

import jax
import jax.numpy as jnp
from jax import lax
from jax.experimental import pallas as pl
from jax.experimental.pallas import tpu as pltpu

F32 = jnp.float32
BF16 = jnp.bfloat16
HIGHEST = lax.Precision.HIGHEST

D_MODEL = 1024
DEPTH = 2
BRANCH_W = 256
S5_GROUP = 16
S5_GROUPS = 16
S5_STATE = 64
S5_NSTATE = S5_GROUPS * S5_STATE
GMLP_HEADS = 4
GMLP_HEAD_DIM = 64
GMLP_CHUNK = 128
GDN_HEAD_DIM = 64
GDN_HEADS = 4
GDN_CONV = 4
GDN_CHUNK = 64
SC_CONV = 3
D_FF = 2816
N_EXPERTS = 8
TOP_K = 2
D_FF_EXPERT = 3584
EPS = 1e-6

LANES = 128
SUBLANES = 8
VMEM_BYTES_V7X = 64 * 1024 * 1024

COL_GM = 0
COL_S5 = 512
COL_Z = 768
COL_Q = 1024
COL_K = 1280
COL_V = 1536
COL_SCB = 1792
COL_SCC = 2048
COL_SCX = 2304
COL_AB = 2560
P_COLS = 2688
AB_B_LANE = 64

IN_TM = 512
S5_CHUNK = 128
GM_TILE = 512
SC_TILE = 512
GDN_TILE = 256
OUT_TM = 512
FFN_TM = 512
FFN_TF = 256
MOE_TM = 512
MOE_TF = 512
ROUTE_TM = 512


def _cparams(sem, vmem_mb):
    return pltpu.CompilerParams(dimension_semantics=sem, vmem_limit_bytes=vmem_mb * 1024 * 1024)


def _rms(x, g):
    return x * lax.rsqrt(jnp.mean(x * x, axis=-1, keepdims=True) + EPS) * g


def _silu(x):
    return x * jax.nn.sigmoid(x)


def _bdot(a, b):
    return jnp.dot(a.astype(BF16), b.astype(BF16), preferred_element_type=F32)


def _bdot_nt(a, b):
    return lax.dot_general(a.astype(BF16), b.astype(BF16), (((1,), (1,)), ((), ())),
                           preferred_element_type=F32)


def _in_proj_kernel(x_ref, g_ref, w_ref, o_ref):
    h = _rms(x_ref[...], g_ref[...])
    o_ref[...] = _bdot(h, w_ref[...])


def _in_proj(x2, g, w):
    t = x2.shape[0]
    return pl.pallas_call(
        _in_proj_kernel,
        grid=(t // IN_TM,),
        in_specs=[pl.BlockSpec((IN_TM, D_MODEL), lambda i: (i, 0)),
                  pl.BlockSpec((1, D_MODEL), lambda i: (0, 0)),
                  pl.BlockSpec((D_MODEL, P_COLS), lambda i: (0, 0))],
        out_specs=pl.BlockSpec((IN_TM, P_COLS), lambda i: (i, 0)),
        out_shape=jax.ShapeDtypeStruct((t, P_COLS), F32),
        compiler_params=_cparams(("parallel",), 48),
        name="in_proj",
    )(x2, g, w)


def _s5_kernel(u_ref, wb_ref, pw_ref, wc_ref, d_ref, wglu_ref, bglu_ref, on_ref, o_ref, sr_ref, si_ref):
    n = S5_NSTATE

    @pl.when(pl.program_id(1) == 0)
    def _():
        sr_ref[...] = jnp.zeros_like(sr_ref)
        si_ref[...] = jnp.zeros_like(si_ref)

    u = u_ref[...]
    bu = _bdot(u, wb_ref[...])
    xr, xi = bu[:, :n], bu[:, n:]
    row = lax.broadcasted_iota(jnp.int32, (S5_CHUNK, n), 0)
    a1r, a1i = pw_ref[0:1, :n], pw_ref[0:1, n:]
    sr, si = sr_ref[...], si_ref[...]
    first = row == 0
    xr = xr + jnp.where(first, a1r * sr - a1i * si, 0.0)
    xi = xi + jnp.where(first, a1r * si + a1i * sr, 0.0)
    d = 1
    s = 0
    while d < S5_CHUNK:
        ar, ai = pw_ref[s:s + 1, :n], pw_ref[s:s + 1, n:]
        keep = row >= d
        shr = jnp.where(keep, pltpu.roll(xr, d, 0), 0.0)
        shi = jnp.where(keep, pltpu.roll(xi, d, 0), 0.0)
        xr, xi = xr + ar * shr - ai * shi, xi + ar * shi + ai * shr
        d *= 2
        s += 1
    sr_ref[...] = xr[S5_CHUNK - 1:, :]
    si_ref[...] = xi[S5_CHUNK - 1:, :]
    y = _bdot(xr, wc_ref[:n, :]) + _bdot(xi, wc_ref[n:, :])
    y = y + d_ref[...] * u
    y = jax.nn.gelu(y)
    y = y * jax.nn.sigmoid(_bdot(y, wglu_ref[...]) + bglu_ref[...])
    o_ref[...] = _rms(y, on_ref[...])


def _s5_params(lam_re, lam_im, log_step, b_re, b_im, c_re, c_im):
    g, n, p = S5_GROUPS, S5_STATE, S5_GROUP
    dt = jnp.exp(log_step)[:, None]
    mag = jnp.exp(lam_re * dt)
    ar, ai = mag * jnp.cos(lam_im * dt), mag * jnp.sin(lam_im * dt)
    den = lam_re * lam_re + lam_im * lam_im
    fr = ((ar - 1.0) * lam_re + ai * lam_im) / den
    fi = (ai * lam_re - (ar - 1.0) * lam_im) / den
    bbr = fr[..., None] * b_re - fi[..., None] * b_im
    bbi = fr[..., None] * b_im + fi[..., None] * b_re
    eye = jnp.eye(g, dtype=F32)
    wbr = jnp.einsum('gnp,gh->gphn', bbr, eye).reshape(g * p, g * n)
    wbi = jnp.einsum('gnp,gh->gphn', bbi, eye).reshape(g * p, g * n)
    wb = jnp.concatenate([wbr, wbi], axis=1)
    wcr = jnp.einsum('gpn,gh->gnhp', c_re, eye).reshape(g * n, g * p)
    wci = jnp.einsum('gpn,gh->gnhp', -c_im, eye).reshape(g * n, g * p)
    wc = jnp.concatenate([wcr, wci], axis=0)
    rows = []
    pr, pi = ar.reshape(1, g * n), ai.reshape(1, g * n)
    d = 1
    while d < S5_CHUNK:
        rows.append(jnp.concatenate([pr, pi], axis=1))
        pr, pi = pr * pr - pi * pi, 2.0 * pr * pi
        d *= 2
    while len(rows) < SUBLANES:
        rows.append(jnp.zeros_like(rows[0]))
    pw = jnp.concatenate(rows, axis=0)
    return wb.astype(BF16), wc.astype(BF16), pw


def _s5_mixer(p3, wb, wc, pw, d_skip, w_glu, b_glu, out_norm):
    b, l, _ = p3.shape
    n2 = 2 * S5_NSTATE
    const = lambda shape: pl.BlockSpec(shape, lambda i, j: (0,) * len(shape))
    return pl.pallas_call(
        _s5_kernel,
        grid=(b, l // S5_CHUNK),
        in_specs=[pl.BlockSpec((None, S5_CHUNK, BRANCH_W), lambda i, j: (i, j, COL_S5 // BRANCH_W)),
                  const((BRANCH_W, n2)), const((SUBLANES, n2)), const((n2, BRANCH_W)),
                  const((1, BRANCH_W)), const((BRANCH_W, BRANCH_W)), const((1, BRANCH_W)),
                  const((1, BRANCH_W))],
        out_specs=pl.BlockSpec((None, S5_CHUNK, BRANCH_W), lambda i, j: (i, j, 0)),
        out_shape=jax.ShapeDtypeStruct((b, l, BRANCH_W), F32),
        scratch_shapes=[pltpu.VMEM((1, S5_NSTATE), F32), pltpu.VMEM((1, S5_NSTATE), F32)],
        compiler_params=_cparams(("parallel", "arbitrary"), 40),
        name="s5_mixer",
    )(p3, wb, pw, wc, d_skip.reshape(1, -1), w_glu.astype(BF16), b_glu.reshape(1, -1),
      out_norm.reshape(1, -1))


def _gmlp_kernel(p_ref, lng_ref, lnb_ref, w_ref, bias_ref, on_ref, o_ref):
    z = jax.nn.gelu(p_ref[...])
    u, v = z[:, :BRANCH_W], z[:, BRANCH_W:]
    vc = v - jnp.mean(v, axis=-1, keepdims=True)
    v = vc * lax.rsqrt(jnp.mean(vc * vc, axis=-1, keepdims=True) + EPS) * lng_ref[...] + lnb_ref[...]
    ti = lax.broadcasted_iota(jnp.int32, (GMLP_CHUNK, GMLP_CHUNK), 0)
    si = lax.broadcasted_iota(jnp.int32, (GMLP_CHUNK, GMLP_CHUNK), 1)
    tril = ti >= si
    ws = [jnp.where(tril, w_ref[h], 0.0).astype(BF16) for h in range(GMLP_HEADS)]
    lane = lax.broadcasted_iota(jnp.int32, (GMLP_CHUNK, BRANCH_W), 1)
    bias = bias_ref[...]
    outs = []
    for c in range(GM_TILE // GMLP_CHUNK):
        vb = v[c * GMLP_CHUNK:(c + 1) * GMLP_CHUNK, :].astype(BF16)
        s = jnp.dot(ws[GMLP_HEADS - 1], vb, preferred_element_type=F32)
        for h in range(GMLP_HEADS - 2, -1, -1):
            sh = jnp.dot(ws[h], vb, preferred_element_type=F32)
            s = jnp.where(lane < (h + 1) * GMLP_HEAD_DIM, sh, s)
        outs.append(s + bias)
    s = jnp.concatenate(outs, axis=0)
    o_ref[...] = _rms(u * s, on_ref[...])


def _gmlp_mixer(p3, ln_g, ln_b, w_sp, b_sp, out_norm):
    b, l, _ = p3.shape
    bias = jnp.repeat(b_sp.T, GMLP_HEAD_DIM, axis=1)
    const = lambda shape: pl.BlockSpec(shape, lambda i, j: (0,) * len(shape))
    return pl.pallas_call(
        _gmlp_kernel,
        grid=(b, l // GM_TILE),
        in_specs=[pl.BlockSpec((None, GM_TILE, 2 * BRANCH_W), lambda i, j: (i, j, COL_GM // (2 * BRANCH_W))),
                  const((1, BRANCH_W)), const((1, BRANCH_W)),
                  const((GMLP_HEADS, GMLP_CHUNK, GMLP_CHUNK)), const((GMLP_CHUNK, BRANCH_W)),
                  const((1, BRANCH_W))],
        out_specs=pl.BlockSpec((None, GM_TILE, BRANCH_W), lambda i, j: (i, j, 0)),
        out_shape=jax.ShapeDtypeStruct((b, l, BRANCH_W), F32),
        compiler_params=_cparams(("parallel", "parallel"), 32),
        name="gmlp_mixer",
    )(p3, ln_g.reshape(1, -1), ln_b.reshape(1, -1), w_sp, bias, out_norm.reshape(1, -1))


def _shortconv_kernel(b_ref, c_ref, x_ref, w_ref, on_ref, o_ref, halo_ref):
    @pl.when(pl.program_id(1) == 0)
    def _():
        halo_ref[...] = jnp.zeros_like(halo_ref)

    cx = c_ref[...] * x_ref[...]
    ext = jnp.concatenate([halo_ref[...], cx], axis=0)
    halo_ref[...] = cx[SC_TILE - SUBLANES:, :]
    y = w_ref[SC_CONV - 1:SC_CONV, :] * cx
    for j in range(SC_CONV - 1):
        sh = SC_CONV - 1 - j
        y = y + w_ref[j:j + 1, :] * pltpu.roll(ext, sh, 0)[SUBLANES:, :]
    o_ref[...] = _rms(b_ref[...] * y, on_ref[...])


def _shortconv_mixer(p3, conv_w, out_norm):
    b, l, _ = p3.shape
    w = jnp.concatenate([conv_w, jnp.zeros((SUBLANES - SC_CONV, BRANCH_W), F32)], axis=0)
    col = lambda c: pl.BlockSpec((None, SC_TILE, BRANCH_W), lambda i, j: (i, j, c // BRANCH_W))
    const = lambda shape: pl.BlockSpec(shape, lambda i, j: (0,) * len(shape))
    return pl.pallas_call(
        _shortconv_kernel,
        grid=(b, l // SC_TILE),
        in_specs=[col(COL_SCB), col(COL_SCC), col(COL_SCX), const((SUBLANES, BRANCH_W)), const((1, BRANCH_W))],
        out_specs=pl.BlockSpec((None, SC_TILE, BRANCH_W), lambda i, j: (i, j, 0)),
        out_shape=jax.ShapeDtypeStruct((b, l, BRANCH_W), F32),
        scratch_shapes=[pltpu.VMEM((SUBLANES, BRANCH_W), F32)],
        compiler_params=_cparams(("parallel", "arbitrary"), 32),
        name="shortconv_mixer",
    )(p3, p3, p3, w, out_norm.reshape(1, -1))


def _gdn_kernel(q_ref, k_ref, v_ref, z_ref, ab_ref, cw_ref, alog_ref, dtb_ref, ng_ref, o_ref,
                halo_ref, state_ref):
    c_len, h_dim, nh, w = GDN_CHUNK, GDN_HEAD_DIM, GDN_HEADS, BRANCH_W
    n_chunks = GDN_TILE // c_len

    @pl.when(pl.program_id(1) == 0)
    def _():
        halo_ref[...] = jnp.zeros_like(halo_ref)
        state_ref[...] = jnp.zeros_like(state_ref)

    shift = int(math.log2(h_dim))
    r256 = lax.broadcasted_iota(jnp.int32, (w, w), 0)
    c256 = lax.broadcasted_iota(jnp.int32, (w, w), 1)
    same_head = (r256 >> shift) == (c256 >> shift)
    ebd = jnp.where(same_head, 1.0, 0.0).astype(BF16)
    causal = same_head & (r256 >= c256)
    strict = same_head & (r256 > c256)
    eye = jnp.where(r256 == c256, 1.0, 0.0)
    r128 = lax.broadcasted_iota(jnp.int32, (LANES, w), 0)
    c128 = lax.broadcasted_iota(jnp.int32, (LANES, w), 1)
    exp_a = jnp.where(r128 == (c128 >> shift), 1.0, 0.0)
    exp_b = jnp.where(r128 == (c128 >> shift) + AB_B_LANE, 1.0, 0.0)

    def stack(a):
        lane_head = (lax.broadcasted_iota(jnp.int32, a.shape, 1) >> shift) & (nh - 1)
        return jnp.concatenate([jnp.where(lane_head == h, a, 0.0) for h in range(nh)], axis=0)

    def unstack(a):
        return a[0:c_len] + a[c_len:2 * c_len] + a[2 * c_len:3 * c_len] + a[3 * c_len:4 * c_len]

    def conv_silu(ref, col):
        cur = ref[...]
        ext = jnp.concatenate([halo_ref[:, col * w:(col + 1) * w], cur], axis=0)
        halo_ref[:, col * w:(col + 1) * w] = cur[GDN_TILE - SUBLANES:, :]
        y = cw_ref[GDN_CONV - 1:GDN_CONV, col * w:(col + 1) * w] * cur
        for j in range(GDN_CONV - 1):
            sh = GDN_CONV - 1 - j
            y = y + cw_ref[j:j + 1, col * w:(col + 1) * w] * pltpu.roll(ext, sh, 0)[SUBLANES:, :]
        return _silu(y)

    q = conv_silu(q_ref, 0)
    k = conv_silu(k_ref, 1)
    v = conv_silu(v_ref, 2)
    q = q * lax.rsqrt(jnp.dot((q * q).astype(BF16), ebd, preferred_element_type=F32) + EPS) * (h_dim ** -0.5)
    k = k * lax.rsqrt(jnp.dot((k * k).astype(BF16), ebd, preferred_element_type=F32) + EPS)

    ab = ab_ref[...]
    beta = jax.nn.sigmoid(ab)
    xa = ab + dtb_ref[...]
    softplus = jnp.maximum(xa, 0.0) + jnp.log(1.0 + jnp.exp(-jnp.abs(xa)))
    g = -jnp.exp(alog_ref[...]) * softplus
    rt = lax.broadcasted_iota(jnp.int32, (GDN_TILE, GDN_TILE), 0)
    ct = lax.broadcasted_iota(jnp.int32, (GDN_TILE, GDN_TILE), 1)
    ltri = jnp.where((rt // c_len == ct // c_len) & (rt >= ct), 1.0, 0.0)
    gcum = jnp.dot(ltri, g, precision=HIGHEST, preferred_element_type=F32)
    gexp = jnp.dot(gcum, exp_a, precision=HIGHEST, preferred_element_type=F32)
    bexp = jnp.dot(beta, exp_b, precision=HIGHEST, preferred_element_type=F32)

    outs = []
    state = state_ref[...]
    for c in range(n_chunks):
        sl = slice(c * c_len, (c + 1) * c_len)
        qc, kc, vc, gc, bc = q[sl], k[sl], v[sl], gexp[sl], bexp[sl]
        glast = gc[c_len - 1:c_len, :]
        eg = jnp.exp(gc)
        q_dec = qc * eg
        k_dec = kc * jnp.exp(glast - gc)
        g_tot = jnp.exp(glast)
        rhs = jnp.concatenate([vc * bc, kc * bc * eg], axis=1)
        kst = stack(kc)
        qst = stack(qc)
        kk = _bdot_nt(kst, kst)
        qk = _bdot_nt(qst, kst)
        gcol = stack(gc)
        grow = gcol.T
        bcol = stack(bc)
        dec = jnp.exp(jnp.where(causal, gcol - grow, -jnp.inf))
        a_mat = jnp.where(strict, bcol * kk * dec, 0.0)
        qkd = qk * dec
        t_inv = eye - a_mat
        pw = a_mat
        n_sq = int(math.log2(c_len)) - 1
        for _ in range(n_sq):
            pw = _bdot(pw, pw)
            t_inv = t_inv + _bdot(t_inv, pw)
        sol = unstack(_bdot(t_inv, stack(rhs)))
        w_v, w_k = sol[:, :w], sol[:, w:]
        v_new = w_v - _bdot(w_k, state)
        o = _bdot(q_dec, state) + unstack(_bdot(qkd, stack(v_new)))
        state = state * g_tot + jnp.where(same_head, _bdot(k_dec.T, v_new), 0.0)
        outs.append(o)
    state_ref[...] = state
    o = jnp.concatenate(outs, axis=0)
    ms = jnp.dot((o * o).astype(BF16), ebd, preferred_element_type=F32) * (1.0 / h_dim)
    o = o * lax.rsqrt(ms + EPS) * ng_ref[...]
    o_ref[...] = o * _silu(z_ref[...])


def _gdn_mixer(p3, conv_w, a_log, dt_bias, norm_g):
    b, l, _ = p3.shape
    cw = jnp.concatenate([conv_w, jnp.zeros((SUBLANES - GDN_CONV, 3 * BRANCH_W), F32)], axis=0)
    lane_row = lambda vec: jnp.zeros((1, LANES), F32).at[0, :GDN_HEADS].set(vec)
    col = lambda c: pl.BlockSpec((None, GDN_TILE, BRANCH_W), lambda i, j: (i, j, c // BRANCH_W))
    const = lambda shape: pl.BlockSpec(shape, lambda i, j: (0,) * len(shape))
    return pl.pallas_call(
        _gdn_kernel,
        grid=(b, l // GDN_TILE),
        in_specs=[col(COL_Q), col(COL_K), col(COL_V), col(COL_Z),
                  pl.BlockSpec((None, GDN_TILE, LANES), lambda i, j: (i, j, COL_AB // LANES)),
                  const((SUBLANES, 3 * BRANCH_W)), const((1, LANES)), const((1, LANES)), const((1, BRANCH_W))],
        out_specs=pl.BlockSpec((None, GDN_TILE, BRANCH_W), lambda i, j: (i, j, 0)),
        out_shape=jax.ShapeDtypeStruct((b, l, BRANCH_W), F32),
        scratch_shapes=[pltpu.VMEM((SUBLANES, 3 * BRANCH_W), F32), pltpu.VMEM((BRANCH_W, BRANCH_W), F32)],
        compiler_params=_cparams(("parallel", "arbitrary"), 48),
        name="gdn_mixer",
    )(p3, p3, p3, p3, p3, cw, lane_row(a_log), lane_row(dt_bias), jnp.tile(norm_g, GDN_HEADS).reshape(1, -1))


def _out_proj_kernel(x_ref, y0_ref, y1_ref, y2_ref, y3_ref, w_ref, o_ref):
    acc = x_ref[...]
    for i, y_ref in enumerate((y0_ref, y1_ref, y2_ref, y3_ref)):
        acc = acc + _bdot(y_ref[...], w_ref[i * BRANCH_W:(i + 1) * BRANCH_W, :])
    o_ref[...] = acc


def _out_proj(x2, ys, w_out):
    t = x2.shape[0]
    row = lambda n: pl.BlockSpec((OUT_TM, n), lambda i: (i, 0))
    return pl.pallas_call(
        _out_proj_kernel,
        grid=(t // OUT_TM,),
        in_specs=[row(D_MODEL)] + [row(BRANCH_W)] * 4 + [pl.BlockSpec((D_MODEL, D_MODEL), lambda i: (0, 0))],
        out_specs=row(D_MODEL),
        out_shape=jax.ShapeDtypeStruct((t, D_MODEL), F32),
        compiler_params=_cparams(("parallel",), 32),
        name="out_proj",
    )(x2, *ys, w_out.astype(BF16))


def _ffn_kernel(x_ref, g_ref, wg_ref, wu_ref, wd_ref, o_ref):
    x = x_ref[...]
    hb = _rms(x, g_ref[...]).astype(BF16)

    def body(f, acc):
        a = jnp.dot(hb, wg_ref[f], preferred_element_type=F32)
        u = jnp.dot(hb, wu_ref[f], preferred_element_type=F32)
        return acc + jnp.dot((_silu(a) * u).astype(BF16), wd_ref[f], preferred_element_type=F32)

    o_ref[...] = lax.fori_loop(0, D_FF // FFN_TF, body, x)


def _ffn_dense(x2, g, w_gate, w_up, w_down):
    t = x2.shape[0]
    nf = D_FF // FFN_TF
    wg = w_gate.astype(BF16).reshape(D_MODEL, nf, FFN_TF).transpose(1, 0, 2)
    wu = w_up.astype(BF16).reshape(D_MODEL, nf, FFN_TF).transpose(1, 0, 2)
    wd = w_down.astype(BF16).reshape(nf, FFN_TF, D_MODEL)
    const3 = lambda shape: pl.BlockSpec(shape, lambda i: (0, 0, 0))
    return pl.pallas_call(
        _ffn_kernel,
        grid=(t // FFN_TM,),
        in_specs=[pl.BlockSpec((FFN_TM, D_MODEL), lambda i: (i, 0)),
                  pl.BlockSpec((1, D_MODEL), lambda i: (0, 0)),
                  const3((nf, D_MODEL, FFN_TF)), const3((nf, D_MODEL, FFN_TF)), const3((nf, FFN_TF, D_MODEL))],
        out_specs=pl.BlockSpec((FFN_TM, D_MODEL), lambda i: (i, 0)),
        out_shape=jax.ShapeDtypeStruct((t, D_MODEL), F32),
        compiler_params=_cparams(("parallel",), 56),
        name="ffn_dense",
    )(x2, g.reshape(1, -1), wg, wu, wd)


def _moe_kernel(be_ref, nu_ref, x_ref, g_ref, wg_ref, wu_ref, wd_ref, o_ref, acc_ref, xb_ref):
    m, f = pl.program_id(0), pl.program_id(1)

    @pl.when(f == 0)
    def _():
        acc_ref[...] = jnp.zeros_like(acc_ref)
        xb_ref[...] = _rms(x_ref[...], g_ref[...]).astype(BF16)

    @pl.when(m < nu_ref[0])
    def _():
        xb = xb_ref[...]
        a = jnp.dot(xb, wg_ref[...], preferred_element_type=F32)
        u = jnp.dot(xb, wu_ref[...], preferred_element_type=F32)
        acc_ref[...] += jnp.dot((_silu(a) * u).astype(BF16), wd_ref[...], preferred_element_type=F32)

    @pl.when(f == pl.num_programs(1) - 1)
    def _():
        o_ref[...] = acc_ref[...]


def _moe_experts(buf, g, block_e, n_used, wg, wu, wd):
    rows = buf.shape[0]
    grid_spec = pltpu.PrefetchScalarGridSpec(
        num_scalar_prefetch=2,
        grid=(rows // MOE_TM, D_FF_EXPERT // MOE_TF),
        in_specs=[pl.BlockSpec((MOE_TM, D_MODEL), lambda m, f, be, nu: (m, 0)),
                  pl.BlockSpec((1, D_MODEL), lambda m, f, be, nu: (0, 0)),
                  pl.BlockSpec((None, D_MODEL, MOE_TF), lambda m, f, be, nu: (be[m], 0, f)),
                  pl.BlockSpec((None, D_MODEL, MOE_TF), lambda m, f, be, nu: (be[m], 0, f)),
                  pl.BlockSpec((None, MOE_TF, D_MODEL), lambda m, f, be, nu: (be[m], f, 0))],
        out_specs=pl.BlockSpec((MOE_TM, D_MODEL), lambda m, f, be, nu: (m, 0)),
        scratch_shapes=[pltpu.VMEM((MOE_TM, D_MODEL), F32), pltpu.VMEM((MOE_TM, D_MODEL), BF16)],
    )
    return pl.pallas_call(
        _moe_kernel,
        grid_spec=grid_spec,
        out_shape=jax.ShapeDtypeStruct((rows, D_MODEL), F32),
        compiler_params=_cparams(("parallel", "arbitrary"), 48),
        name="moe_experts",
    )(block_e, n_used, buf, g.reshape(1, -1), wg, wu, wd)


def _route_kernel(x_ref, g_ref, wt_ref, wp_ref, e_ref, r_ref, gc_ref, cnt_ref, run_ref):
    tm = ROUTE_TM

    @pl.when(pl.program_id(0) == 0)
    def _():
        run_ref[...] = jnp.zeros_like(run_ref)

    h = _rms(x_ref[...], g_ref[...])
    lt = lax.dot_general(wt_ref[...], h, (((1,), (1,)), ((), ())), precision=HIGHEST,
                         preferred_element_type=F32)
    sub = lax.broadcasted_iota(jnp.int32, (N_EXPERTS, tm), 0)
    m1 = jnp.max(lt, axis=0, keepdims=True)
    i1 = jnp.min(jnp.where(lt == m1, sub, N_EXPERTS), axis=0, keepdims=True)
    lt2 = jnp.where(sub == i1, -jnp.inf, lt)
    m2 = jnp.max(lt2, axis=0, keepdims=True)
    i2 = jnp.min(jnp.where(lt2 == m2, sub, N_EXPERTS), axis=0, keepdims=True)
    oh0 = jnp.where(sub == i1, 1.0, 0.0)
    oh1 = jnp.where(sub == i2, 1.0, 0.0)
    cnt = oh0 + oh1
    ti = lax.broadcasted_iota(jnp.int32, (tm, tm), 0)
    tj = lax.broadcasted_iota(jnp.int32, (tm, tm), 1)
    upper = jnp.where(ti < tj, 1.0, 0.0).astype(BF16)
    pre = jnp.dot(cnt.astype(BF16), upper, preferred_element_type=F32) + run_ref[:, 0:1]
    r0 = jnp.sum(oh0 * pre, axis=0, keepdims=True)
    r1 = jnp.sum(oh1 * pre, axis=0, keepdims=True)
    e_ref[...] = jnp.concatenate([i1, i2], axis=0)
    r_ref[...] = jnp.concatenate([r0, r1], axis=0).astype(jnp.int32)
    run = run_ref[...] + jnp.sum(cnt, axis=1, keepdims=True)
    run_ref[...] = run
    cnt_ref[...] = run
    lr = jnp.dot(h, wp_ref[...], precision=HIGHEST, preferred_element_type=F32)
    lane = lax.broadcasted_iota(jnp.int32, (tm, LANES), 1)
    lr = jnp.where(lane < N_EXPERTS, lr, -jnp.inf)
    t1 = jnp.max(lr, axis=1, keepdims=True)
    j1 = jnp.min(jnp.where(lr == t1, lane, LANES), axis=1, keepdims=True)
    t2 = jnp.max(jnp.where(lane == j1, -jnp.inf, lr), axis=1, keepdims=True)
    ex = jnp.exp(t2 - t1)
    g0 = 1.0 / (1.0 + ex)
    gc_ref[...] = jnp.where(lane == 0, g0, jnp.where(lane == 1, ex * g0, 0.0))


def _moe_route(x2, ffn_norm, w_router):
    t = x2.shape[0]
    tm = ROUTE_TM
    wp = jnp.zeros((D_MODEL, LANES), F32).at[:, :N_EXPERTS].set(w_router)
    return pl.pallas_call(
        _route_kernel,
        grid=(t // tm,),
        in_specs=[pl.BlockSpec((tm, D_MODEL), lambda i: (i, 0)),
                  pl.BlockSpec((1, D_MODEL), lambda i: (0, 0)),
                  pl.BlockSpec((N_EXPERTS, D_MODEL), lambda i: (0, 0)),
                  pl.BlockSpec((D_MODEL, LANES), lambda i: (0, 0))],
        out_specs=[pl.BlockSpec((TOP_K, tm), lambda i: (0, i)),
                   pl.BlockSpec((TOP_K, tm), lambda i: (0, i)),
                   pl.BlockSpec((tm, LANES), lambda i: (i, 0)),
                   pl.BlockSpec((N_EXPERTS, LANES), lambda i: (0, 0))],
        out_shape=[jax.ShapeDtypeStruct((TOP_K, t), jnp.int32),
                   jax.ShapeDtypeStruct((TOP_K, t), jnp.int32),
                   jax.ShapeDtypeStruct((t, LANES), F32),
                   jax.ShapeDtypeStruct((N_EXPERTS, LANES), F32)],
        scratch_shapes=[pltpu.VMEM((N_EXPERTS, LANES), F32)],
        compiler_params=_cparams(("arbitrary",), 32),
        name="moe_route",
    )(x2, ffn_norm.reshape(1, -1), w_router.T, wp)


def _row_copy(src_ref, src_row, dst_ref, dst_row, sem):
    return pltpu.make_async_copy(src_ref.at[pl.ds(src_row, 1), :], dst_ref.at[pl.ds(dst_row, 1), :], sem)


def _dispatch_kernel(dest_hbm, x_ref, init_hbm, out_hbm, idx_ref, idx_sem, row_sem):
    del init_hbm
    tm = ROUTE_TM
    load = pltpu.make_async_copy(dest_hbm.at[pl.program_id(0)], idx_ref, idx_sem)
    load.start()
    load.wait()

    def issue(t, carry):
        for k in range(TOP_K):
            _row_copy(x_ref, t, out_hbm, idx_ref[k, t], row_sem).start()
        return carry

    lax.fori_loop(0, tm, issue, 0, unroll=8)

    def drain(t, carry):
        for k in range(TOP_K):
            _row_copy(x_ref, t, out_hbm, idx_ref[k, t], row_sem).wait()
        return carry

    lax.fori_loop(0, tm, drain, 0, unroll=8)


def _moe_dispatch(x2, dest3, n_rows):
    t = x2.shape[0]
    tm = ROUTE_TM
    return pl.pallas_call(
        _dispatch_kernel,
        grid=(t // tm,),
        in_specs=[pl.BlockSpec(memory_space=pl.ANY),
                  pl.BlockSpec((tm, D_MODEL), lambda i: (i, 0)),
                  pl.BlockSpec(memory_space=pl.ANY)],
        out_specs=pl.BlockSpec(memory_space=pl.ANY),
        out_shape=jax.ShapeDtypeStruct((n_rows, D_MODEL), F32),
        scratch_shapes=[pltpu.SMEM((TOP_K, tm), jnp.int32), pltpu.SemaphoreType.DMA, pltpu.SemaphoreType.DMA],
        input_output_aliases={2: 0},
        compiler_params=_cparams(("arbitrary",), 32),
        name="moe_dispatch",
    )(dest3, x2, jnp.zeros((n_rows, D_MODEL), F32))


def _combine_kernel(dest_hbm, x_ref, gc_ref, fn_ref, y_hbm, o_ref, idx_ref, ybuf_ref, idx_sem, row_sem):
    tm = ROUTE_TM
    load = pltpu.make_async_copy(dest_hbm.at[pl.program_id(0)], idx_ref, idx_sem)
    load.start()
    load.wait()

    def issue(t, carry):
        for k in range(TOP_K):
            _row_copy(y_hbm, idx_ref[k, t], ybuf_ref.at[k], t, row_sem).start()
        return carry

    lax.fori_loop(0, tm, issue, 0, unroll=8)

    def drain(t, carry):
        for k in range(TOP_K):
            _row_copy(y_hbm, idx_ref[k, t], ybuf_ref.at[k], t, row_sem).wait()
        return carry

    lax.fori_loop(0, tm, drain, 0, unroll=8)
    gc = gc_ref[...]
    x = x_ref[...] + gc[:, 0:1] * ybuf_ref[0] + gc[:, 1:2] * ybuf_ref[1]
    o_ref[...] = _rms(x, fn_ref[...])


def _moe_combine_norm(x2, y_buf, dest3, gcol, final_norm):
    t = x2.shape[0]
    tm = ROUTE_TM
    return pl.pallas_call(
        _combine_kernel,
        grid=(t // tm,),
        in_specs=[pl.BlockSpec(memory_space=pl.ANY),
                  pl.BlockSpec((tm, D_MODEL), lambda i: (i, 0)),
                  pl.BlockSpec((tm, LANES), lambda i: (i, 0)),
                  pl.BlockSpec((1, D_MODEL), lambda i: (0, 0)),
                  pl.BlockSpec(memory_space=pl.ANY)],
        out_specs=pl.BlockSpec((tm, D_MODEL), lambda i: (i, 0)),
        out_shape=jax.ShapeDtypeStruct((t, D_MODEL), F32),
        scratch_shapes=[pltpu.SMEM((TOP_K, tm), jnp.int32), pltpu.VMEM((TOP_K, tm, D_MODEL), F32),
                        pltpu.SemaphoreType.DMA, pltpu.SemaphoreType.DMA],
        compiler_params=_cparams(("arbitrary",), 32),
        name="moe_combine_norm",
    )(dest3, x2, gcol, final_norm.reshape(1, -1), y_buf)


def _moe_layer(x2, ffn_norm, w_router, w_gate, w_up, w_down, final_norm):
    t = x2.shape[0]
    tm = ROUTE_TM
    e01, r01, gcol, cnt = _moe_route(x2, ffn_norm, w_router)
    counts = cnt[:, 0].astype(jnp.int32)
    padded = (counts + MOE_TM - 1) // MOE_TM * MOE_TM
    pad_ends = jnp.cumsum(padded)
    pad_starts = pad_ends - padded
    n_blocks = -(-t * TOP_K // MOE_TM) + N_EXPERTS
    blk_start = jnp.arange(n_blocks, dtype=jnp.int32) * MOE_TM
    block_e = jnp.minimum(jnp.sum(blk_start[:, None] >= pad_ends[None, :], axis=1), N_EXPERTS - 1).astype(jnp.int32)
    n_used = (pad_ends[-1] // MOE_TM).astype(jnp.int32).reshape(1)
    dest = r01 + jnp.sum(jnp.where(e01[None] == jnp.arange(N_EXPERTS)[:, None, None], pad_starts[:, None, None], 0),
                         axis=0)
    dest3 = dest.reshape(TOP_K, t // tm, tm).transpose(1, 0, 2)
    xs = _moe_dispatch(x2, dest3, n_blocks * MOE_TM)
    y_buf = _moe_experts(xs, ffn_norm, block_e, n_used, w_gate.astype(BF16), w_up.astype(BF16),
                         w_down.astype(BF16))
    return _moe_combine_norm(x2, y_buf, dest3, gcol, final_norm)


def _permute_w_in(w_in):
    s5 = w_in[:, 0:256]
    gm = w_in[:, 256:768]
    qkv = w_in[:, 768:1536]
    z = w_in[:, 1536:1792]
    a = w_in[:, 1792:1796]
    b = w_in[:, 1796:1800]
    sc = w_in[:, 1800:2568]
    ab = jnp.zeros((D_MODEL, LANES), F32).at[:, 0:GDN_HEADS].set(a).at[:, AB_B_LANE:AB_B_LANE + GDN_HEADS].set(b)
    return jnp.concatenate([gm, s5, z, qkv, sc, ab], axis=1).astype(BF16)


def kernel(x, mix_norm, w_in, s5_lam_re, s5_lam_im, s5_log_step, s5_b_re, s5_b_im, s5_c_re, s5_c_im, s5_d, s5_w_glu, s5_b_glu, s5_out_norm, sgu_ln_g, sgu_ln_b, sgu_w, sgu_b, gmlp_out_norm, gdn_conv, gdn_a_log, gdn_dt_bias, gdn_norm, sc_conv, sc_out_norm, w_out, ffn_norm, ffn_w_gate, ffn_w_up, ffn_w_down, moe_router, moe_w_gate, moe_w_up, moe_w_down, final_norm):
    bsz, seqlen, d = x.shape
    t = bsz * seqlen
    x2 = x.reshape(t, d)
    out = None
    for l in range(DEPTH):
        p2 = _in_proj(x2, mix_norm[l].reshape(1, -1), _permute_w_in(w_in[l]))
        p3 = p2.reshape(bsz, seqlen, P_COLS)
        wb, wc, pw = _s5_params(s5_lam_re[l], s5_lam_im[l], s5_log_step[l], s5_b_re[l], s5_b_im[l],
                                s5_c_re[l], s5_c_im[l])
        y_s5 = _s5_mixer(p3, wb, wc, pw, s5_d[l], s5_w_glu[l], s5_b_glu[l], s5_out_norm[l])
        y_gm = _gmlp_mixer(p3, sgu_ln_g[l], sgu_ln_b[l], sgu_w[l], sgu_b[l], gmlp_out_norm[l])
        y_gdn = _gdn_mixer(p3, gdn_conv[l], gdn_a_log[l], gdn_dt_bias[l], gdn_norm[l])
        y_sc = _shortconv_mixer(p3, sc_conv[l], sc_out_norm[l])
        ys = [y.reshape(t, BRANCH_W) for y in (y_s5, y_gm, y_gdn, y_sc)]
        x2 = _out_proj(x2, ys, w_out[l])
        i = l // 2
        if l % 2 == 0:
            x2 = _ffn_dense(x2, ffn_norm[l], ffn_w_gate[i], ffn_w_up[i], ffn_w_down[i])
        else:
            out = _moe_layer(x2, ffn_norm[l], moe_router[i], moe_w_gate[i], moe_w_up[i], moe_w_down[i],
                             final_norm)
    return out.reshape(bsz, seqlen, d)
```

```python
import functools
import math

import jax
import jax.numpy as jnp
from jax import lax
from jax.experimental import pallas as pl
from jax.experimental.pallas import tpu as pltpu

F32 = jnp.float32
BF16 = jnp.bfloat16
HIGHEST = lax.Precision.HIGHEST

D_MODEL = 1024
DEPTH = 2
BRANCH_W = 256
S5_GROUP = 16
S5_GROUPS = 16
S5_STATE = 64
S5_NSTATE = S5_GROUPS * S5_STATE
GMLP_HEADS = 4
GMLP_HEAD_DIM = 64
GMLP_CHUNK = 128
GDN_HEAD_DIM = 64
GDN_HEADS = 4
GDN_CONV = 4
GDN_CHUNK = 64
SC_CONV = 3
D_FF = 2816
N_EXPERTS = 8
TOP_K = 2
D_FF_EXPERT = 3584
EPS = 1e-6

LANES = 128
SUBLANES = 8
VMEM_BYTES_V7X = 64 * 1024 * 1024

COL_GM = 0
COL_S5 = 512
COL_Z = 768
COL_Q = 1024
COL_K = 1280
COL_V = 1536
COL_SCB = 1792
COL_SCC = 2048
COL_SCX = 2304
COL_AB = 2560
P_COLS = 2688
AB_B_LANE = 64

IN_TM = 512
S5_CHUNK = 128
GM_TILE = 512
SC_TILE = 512
GDN_TILE = 256
OUT_TM = 512
FFN_TM = 512
FFN_TF = 256
MOE_TM = 512
MOE_TF = 512
ROUTE_TM = 512


def _cparams(sem, vmem_mb):
    return pltpu.CompilerParams(dimension_semantics=sem, vmem_limit_bytes=vmem_mb * 1024 * 1024)


def _rms(x, g):
    return x * lax.rsqrt(jnp.mean(x * x, axis=-1, keepdims=True) + EPS) * g


def _silu(x):
    return x * jax.nn.sigmoid(x)


def _bdot(a, b):
    return jnp.dot(a.astype(BF16), b.astype(BF16), preferred_element_type=F32)


def _split_dot(a, b, passes, data_on_left):
    data = a if data_on_left else b
    acc = None
    for _ in range(passes):
        piece = data.astype(BF16)
        term = (jnp.dot(piece, b, preferred_element_type=F32) if data_on_left
                else jnp.dot(a, piece, preferred_element_type=F32))
        acc = term if acc is None else acc + term
        data = data - piece.astype(F32)
    return acc


def _bdot_nt(a, b):
    return lax.dot_general(a.astype(BF16), b.astype(BF16), (((1,), (1,)), ((), ())),
                           preferred_element_type=F32)


def _in_proj_kernel(x_ref, g_ref, w_ref, o_ref):
    h = _rms(x_ref[...], g_ref[...])
    o_ref[...] = _bdot(h, w_ref[...])


def _in_proj(x2, g, w):
    t = x2.shape[0]
    return pl.pallas_call(
        _in_proj_kernel,
        grid=(t // IN_TM,),
        in_specs=[pl.BlockSpec((IN_TM, D_MODEL), lambda i: (i, 0)),
                  pl.BlockSpec((1, D_MODEL), lambda i: (0, 0)),
                  pl.BlockSpec((D_MODEL, P_COLS), lambda i: (0, 0))],
        out_specs=pl.BlockSpec((IN_TM, P_COLS), lambda i: (i, 0)),
        out_shape=jax.ShapeDtypeStruct((t, P_COLS), F32),
        compiler_params=_cparams(("parallel",), 48),
        name="in_proj",
    )(x2, g, w)


def _s5_kernel(u_ref, wb_ref, pw_ref, wc_ref, d_ref, wglu_ref, bglu_ref, on_ref, o_ref, sr_ref, si_ref):
    n = S5_NSTATE

    @pl.when(pl.program_id(1) == 0)
    def _():
        sr_ref[...] = jnp.zeros_like(sr_ref)
        si_ref[...] = jnp.zeros_like(si_ref)

    u = u_ref[...]
    bu = _bdot(u, wb_ref[...])
    n_steps = int(math.log2(SUBLANES))
    cr, ci = sr_ref[...], si_ref[...]
    grs, gis = [], []
    for j in range(S5_CHUNK // SUBLANES):
        rows = slice(j * SUBLANES, (j + 1) * SUBLANES)
        gr, gi = bu[rows, :n], bu[rows, n:]
        for s in range(n_steps):
            tr, ti = pw_ref[s, :, :n], pw_ref[s, :, n:]
            rr, ri = pltpu.roll(gr, 1 << s, 0), pltpu.roll(gi, 1 << s, 0)
            gr, gi = gr + tr * rr - ti * ri, gi + tr * ri + ti * rr
        pr, pi = pw_ref[n_steps, :, :n], pw_ref[n_steps, :, n:]
        br, bi = jnp.broadcast_to(cr, (SUBLANES, n)), jnp.broadcast_to(ci, (SUBLANES, n))
        gr, gi = gr + pr * br - pi * bi, gi + pr * bi + pi * br
        cr, ci = gr[SUBLANES - 1:, :], gi[SUBLANES - 1:, :]
        grs.append(gr)
        gis.append(gi)
    sr_ref[...] = cr
    si_ref[...] = ci
    xr, xi = jnp.concatenate(grs, axis=0), jnp.concatenate(gis, axis=0)
    y = _bdot(xr, wc_ref[:n, :]) + _bdot(xi, wc_ref[n:, :])
    y = y + d_ref[...] * u
    y = jax.nn.gelu(y)
    y = y * jax.nn.sigmoid(_bdot(y, wglu_ref[...]) + bglu_ref[...])
    o_ref[...] = _rms(y, on_ref[...])


def _s5_params(lam_re, lam_im, log_step, b_re, b_im, c_re, c_im):
    g, n, p = S5_GROUPS, S5_STATE, S5_GROUP
    dt = jnp.exp(log_step)[:, None]
    mag = jnp.exp(lam_re * dt)
    ar, ai = mag * jnp.cos(lam_im * dt), mag * jnp.sin(lam_im * dt)
    den = lam_re * lam_re + lam_im * lam_im
    fr = ((ar - 1.0) * lam_re + ai * lam_im) / den
    fi = (ai * lam_re - (ar - 1.0) * lam_im) / den
    bbr = fr[..., None] * b_re - fi[..., None] * b_im
    bbi = fr[..., None] * b_im + fi[..., None] * b_re
    eye = jnp.eye(g, dtype=F32)
    wbr = jnp.einsum('gnp,gh->gphn', bbr, eye).reshape(g * p, g * n)
    wbi = jnp.einsum('gnp,gh->gphn', bbi, eye).reshape(g * p, g * n)
    wb = jnp.concatenate([wbr, wbi], axis=1)
    wcr = jnp.einsum('gpn,gh->gnhp', c_re, eye).reshape(g * n, g * p)
    wci = jnp.einsum('gpn,gh->gnhp', -c_im, eye).reshape(g * n, g * p)
    wc = jnp.concatenate([wcr, wci], axis=0)
    a1r, a1i = ar.reshape(1, g * n), ai.reshape(1, g * n)
    r_idx = jnp.arange(SUBLANES)[:, None]
    tables = []
    pr, pi = a1r, a1i
    for s in range(int(math.log2(SUBLANES))):
        keep = r_idx >= (1 << s)
        tables.append(jnp.concatenate([jnp.where(keep, pr, 0.0), jnp.where(keep, pi, 0.0)], axis=1))
        pr, pi = pr * pr - pi * pi, 2.0 * pr * pi
    rows_r, rows_i = [a1r], [a1i]
    for _ in range(SUBLANES - 1):
        qr, qi = rows_r[-1], rows_i[-1]
        rows_r.append(qr * a1r - qi * a1i)
        rows_i.append(qr * a1i + qi * a1r)
    tables.append(jnp.concatenate([jnp.concatenate(rows_r, axis=0), jnp.concatenate(rows_i, axis=0)], axis=1))
    pw = jnp.stack(tables, axis=0)
    return wb.astype(BF16), wc.astype(BF16), pw


def _s5_mixer(p3, wb, wc, pw, d_skip, w_glu, b_glu, out_norm):
    b, l, _ = p3.shape
    n2 = 2 * S5_NSTATE
    const = lambda shape: pl.BlockSpec(shape, lambda i, j: (0,) * len(shape))
    return pl.pallas_call(
        _s5_kernel,
        grid=(b, l // S5_CHUNK),
        in_specs=[pl.BlockSpec((None, S5_CHUNK, BRANCH_W), lambda i, j: (i, j, COL_S5 // BRANCH_W)),
                  const((BRANCH_W, n2)), const((int(math.log2(SUBLANES)) + 1, SUBLANES, n2)), const((n2, BRANCH_W)),
                  const((1, BRANCH_W)), const((BRANCH_W, BRANCH_W)), const((1, BRANCH_W)),
                  const((1, BRANCH_W))],
        out_specs=pl.BlockSpec((None, S5_CHUNK, BRANCH_W), lambda i, j: (i, j, 0)),
        out_shape=jax.ShapeDtypeStruct((b, l, BRANCH_W), F32),
        scratch_shapes=[pltpu.VMEM((1, S5_NSTATE), F32), pltpu.VMEM((1, S5_NSTATE), F32)],
        compiler_params=_cparams(("parallel", "arbitrary"), 40),
        name="s5_mixer",
    )(p3, wb, pw, wc, d_skip.reshape(1, -1), w_glu.astype(BF16), b_glu.reshape(1, -1),
      out_norm.reshape(1, -1))


def _gmlp_kernel(p_ref, lng_ref, lnb_ref, w_ref, bias_ref, on_ref, o_ref):
    z = jax.nn.gelu(p_ref[...])
    u, v = z[:, :BRANCH_W], z[:, BRANCH_W:]
    vc = v - jnp.mean(v, axis=-1, keepdims=True)
    v = vc * lax.rsqrt(jnp.mean(vc * vc, axis=-1, keepdims=True) + EPS) * lng_ref[...] + lnb_ref[...]
    ti = lax.broadcasted_iota(jnp.int32, (GMLP_CHUNK, GMLP_CHUNK), 0)
    si = lax.broadcasted_iota(jnp.int32, (GMLP_CHUNK, GMLP_CHUNK), 1)
    tril = ti >= si
    ws = [jnp.where(tril, w_ref[h], 0.0).astype(BF16) for h in range(GMLP_HEADS)]
    lane = lax.broadcasted_iota(jnp.int32, (GMLP_CHUNK, BRANCH_W), 1)
    bias = bias_ref[...]
    outs = []
    for c in range(GM_TILE // GMLP_CHUNK):
        vb = v[c * GMLP_CHUNK:(c + 1) * GMLP_CHUNK, :].astype(BF16)
        s = jnp.dot(ws[GMLP_HEADS - 1], vb, preferred_element_type=F32)
        for h in range(GMLP_HEADS - 2, -1, -1):
            sh = jnp.dot(ws[h], vb, preferred_element_type=F32)
            s = jnp.where(lane < (h + 1) * GMLP_HEAD_DIM, sh, s)
        outs.append(s + bias)
    s = jnp.concatenate(outs, axis=0)
    o_ref[...] = _rms(u * s, on_ref[...])


def _gmlp_mixer(p3, ln_g, ln_b, w_sp, b_sp, out_norm):
    b, l, _ = p3.shape
    bias = jnp.repeat(b_sp.T, GMLP_HEAD_DIM, axis=1)
    const = lambda shape: pl.BlockSpec(shape, lambda i, j: (0,) * len(shape))
    return pl.pallas_call(
        _gmlp_kernel,
        grid=(b, l // GM_TILE),
        in_specs=[pl.BlockSpec((None, GM_TILE, 2 * BRANCH_W), lambda i, j: (i, j, COL_GM // (2 * BRANCH_W))),
                  const((1, BRANCH_W)), const((1, BRANCH_W)),
                  const((GMLP_HEADS, GMLP_CHUNK, GMLP_CHUNK)), const((GMLP_CHUNK, BRANCH_W)),
                  const((1, BRANCH_W))],
        out_specs=pl.BlockSpec((None, GM_TILE, BRANCH_W), lambda i, j: (i, j, 0)),
        out_shape=jax.ShapeDtypeStruct((b, l, BRANCH_W), F32),
        compiler_params=_cparams(("parallel", "parallel"), 32),
        name="gmlp_mixer",
    )(p3, ln_g.reshape(1, -1), ln_b.reshape(1, -1), w_sp, bias, out_norm.reshape(1, -1))


def _shortconv_kernel(b_ref, c_ref, x_ref, w_ref, on_ref, o_ref, halo_ref):
    @pl.when(pl.program_id(1) == 0)
    def _():
        halo_ref[...] = jnp.zeros_like(halo_ref)

    cx = c_ref[...] * x_ref[...]
    ext = jnp.concatenate([halo_ref[...], cx], axis=0)
    halo_ref[...] = cx[SC_TILE - SUBLANES:, :]
    y = w_ref[SC_CONV - 1:SC_CONV, :] * cx
    for j in range(SC_CONV - 1):
        sh = SC_CONV - 1 - j
        y = y + w_ref[j:j + 1, :] * pltpu.roll(ext, sh, 0)[SUBLANES:, :]
    o_ref[...] = _rms(b_ref[...] * y, on_ref[...])


def _shortconv_mixer(p3, conv_w, out_norm):
    b, l, _ = p3.shape
    w = jnp.concatenate([conv_w, jnp.zeros((SUBLANES - SC_CONV, BRANCH_W), F32)], axis=0)
    col = lambda c: pl.BlockSpec((None, SC_TILE, BRANCH_W), lambda i, j: (i, j, c // BRANCH_W))
    const = lambda shape: pl.BlockSpec(shape, lambda i, j: (0,) * len(shape))
    return pl.pallas_call(
        _shortconv_kernel,
        grid=(b, l // SC_TILE),
        in_specs=[col(COL_SCB), col(COL_SCC), col(COL_SCX), const((SUBLANES, BRANCH_W)), const((1, BRANCH_W))],
        out_specs=pl.BlockSpec((None, SC_TILE, BRANCH_W), lambda i, j: (i, j, 0)),
        out_shape=jax.ShapeDtypeStruct((b, l, BRANCH_W), F32),
        scratch_shapes=[pltpu.VMEM((SUBLANES, BRANCH_W), F32)],
        compiler_params=_cparams(("parallel", "arbitrary"), 32),
        name="shortconv_mixer",
    )(p3, p3, p3, w, out_norm.reshape(1, -1))


def _gdn_kernel(q_ref, k_ref, v_ref, z_ref, ab_ref, cw_ref, alog_ref, dtb_ref, ng_ref, o_ref,
                halo_ref, state_ref):
    c_len, h_dim, nh, w = GDN_CHUNK, GDN_HEAD_DIM, GDN_HEADS, BRANCH_W
    n_chunks = GDN_TILE // c_len

    @pl.when(pl.program_id(1) == 0)
    def _():
        halo_ref[...] = jnp.zeros_like(halo_ref)
        state_ref[...] = jnp.zeros_like(state_ref)

    shift = int(math.log2(h_dim))
    r256 = lax.broadcasted_iota(jnp.int32, (w, w), 0)
    c256 = lax.broadcasted_iota(jnp.int32, (w, w), 1)
    same_head = (r256 >> shift) == (c256 >> shift)
    ebd = jnp.where(same_head, 1.0, 0.0).astype(BF16)
    causal = same_head & (r256 >= c256)
    strict = same_head & (r256 > c256)
    eye = jnp.where(r256 == c256, 1.0, 0.0)
    r128 = lax.broadcasted_iota(jnp.int32, (LANES, w), 0)
    c128 = lax.broadcasted_iota(jnp.int32, (LANES, w), 1)
    exp_a = jnp.where(r128 == (c128 >> shift), 1.0, 0.0).astype(BF16)
    exp_b = jnp.where(r128 == (c128 >> shift) + AB_B_LANE, 1.0, 0.0).astype(BF16)

    def stack(a):
        lane_head = (lax.broadcasted_iota(jnp.int32, a.shape, 1) >> shift) & (nh - 1)
        return jnp.concatenate([jnp.where(lane_head == h, a, 0.0) for h in range(nh)], axis=0)

    def unstack(a):
        return a[0:c_len] + a[c_len:2 * c_len] + a[2 * c_len:3 * c_len] + a[3 * c_len:4 * c_len]

    def conv_silu(ref, col):
        cur = ref[...]
        ext = jnp.concatenate([halo_ref[:, col * w:(col + 1) * w], cur], axis=0)
        halo_ref[:, col * w:(col + 1) * w] = cur[GDN_TILE - SUBLANES:, :]
        y = cw_ref[GDN_CONV - 1:GDN_CONV, col * w:(col + 1) * w] * cur
        for j in range(GDN_CONV - 1):
            sh = GDN_CONV - 1 - j
            y = y + cw_ref[j:j + 1, col * w:(col + 1) * w] * pltpu.roll(ext, sh, 0)[SUBLANES:, :]
        return _silu(y)

    q = conv_silu(q_ref, 0)
    k = conv_silu(k_ref, 1)
    v = conv_silu(v_ref, 2)
    q = q * lax.rsqrt(jnp.dot((q * q).astype(BF16), ebd, preferred_element_type=F32) + EPS) * (h_dim ** -0.5)
    k = k * lax.rsqrt(jnp.dot((k * k).astype(BF16), ebd, preferred_element_type=F32) + EPS)

    ab = ab_ref[...]
    beta = jax.nn.sigmoid(ab)
    xa = ab + dtb_ref[...]
    softplus = jnp.maximum(xa, 0.0) + jnp.log(1.0 + jnp.exp(-jnp.abs(xa)))
    g = -jnp.exp(alog_ref[...]) * softplus
    rt = lax.broadcasted_iota(jnp.int32, (GDN_TILE, GDN_TILE), 0)
    ct = lax.broadcasted_iota(jnp.int32, (GDN_TILE, GDN_TILE), 1)
    ltri = jnp.where(((rt >> shift) == (ct >> shift)) & (rt >= ct), 1.0, 0.0).astype(BF16)
    gcum = _split_dot(ltri, g, 3, data_on_left=False)
    gexp = _split_dot(gcum, exp_a, 2, data_on_left=True)
    bexp = _split_dot(beta, exp_b, 2, data_on_left=True)

    outs = []
    state = state_ref[...]
    q_decs, k_decs, g_tots, rhss, a_mats, qkds = [], [], [], [], [], []
    for c in range(n_chunks):
        sl = slice(c * c_len, (c + 1) * c_len)
        qc, kc, vc, gc, bc = q[sl], k[sl], v[sl], gexp[sl], bexp[sl]
        glast = gc[c_len - 1:c_len, :]
        eg = jnp.exp(gc)
        q_decs.append(qc * eg)
        k_decs.append(kc * jnp.exp(glast - gc))
        g_tots.append(jnp.exp(glast))
        kst = stack(kc).astype(BF16)
        kk = _bdot_nt(kst, kst)
        qk = _bdot_nt(stack(qc), kst)
        gcol = stack(gc)
        bcol = stack(bc)
        dec = jnp.exp(jnp.where(causal, gcol - gcol.T, -jnp.inf))
        a_mats.append(jnp.where(strict, bcol * kk * dec, 0.0))
        qkds.append((qk * dec).astype(BF16))
        tile4 = lambda a: jnp.concatenate([a] * nh, axis=0)
        rhss.append(jnp.concatenate([tile4(vc) * bcol, tile4(kc * eg) * bcol], axis=1).astype(BF16))
    t_invs = [eye - a for a in a_mats]
    pws = a_mats
    for _ in range(int(math.log2(c_len)) - 1):
        pws = [_bdot(p, p) for p in pws]
        t_invs = [t + _bdot(t, p) for t, p in zip(t_invs, pws)]
    sols = [unstack(_bdot(t, r)) for t, r in zip(t_invs, rhss)]
    for c in range(n_chunks):
        w_v, w_k = sols[c][:, :w], sols[c][:, w:]
        v_new = w_v - _bdot(w_k, state)
        o = _bdot(q_decs[c], state) + unstack(_bdot(qkds[c], stack(v_new)))
        state = state * g_tots[c] + jnp.where(same_head, _bdot(k_decs[c].T, v_new), 0.0)
        outs.append(o)
    state_ref[...] = state
    o = jnp.concatenate(outs, axis=0)
    ms = jnp.dot((o * o).astype(BF16), ebd, preferred_element_type=F32) * (1.0 / h_dim)
    o = o * lax.rsqrt(ms + EPS) * ng_ref[...]
    o_ref[...] = o * _silu(z_ref[...])


def _gdn_mixer(p3, conv_w, a_log, dt_bias, norm_g):
    b, l, _ = p3.shape
    cw = jnp.concatenate([conv_w, jnp.zeros((SUBLANES - GDN_CONV, 3 * BRANCH_W), F32)], axis=0)
    lane_row = lambda vec: jnp.zeros((1, LANES), F32).at[0, :GDN_HEADS].set(vec)
    col = lambda c: pl.BlockSpec((None, GDN_TILE, BRANCH_W), lambda i, j: (i, j, c // BRANCH_W))
    const = lambda shape: pl.BlockSpec(shape, lambda i, j: (0,) * len(shape))
    return pl.pallas_call(
        _gdn_kernel,
        grid=(b, l // GDN_TILE),
        in_specs=[col(COL_Q), col(COL_K), col(COL_V), col(COL_Z),
                  pl.BlockSpec((None, GDN_TILE, LANES), lambda i, j: (i, j, COL_AB // LANES)),
                  const((SUBLANES, 3 * BRANCH_W)), const((1, LANES)), const((1, LANES)), const((1, BRANCH_W))],
        out_specs=pl.BlockSpec((None, GDN_TILE, BRANCH_W), lambda i, j: (i, j, 0)),
        out_shape=jax.ShapeDtypeStruct((b, l, BRANCH_W), F32),
        scratch_shapes=[pltpu.VMEM((SUBLANES, 3 * BRANCH_W), F32), pltpu.VMEM((BRANCH_W, BRANCH_W), F32)],
        compiler_params=_cparams(("parallel", "arbitrary"), 48),
        name="gdn_mixer",
    )(p3, p3, p3, p3, p3, cw, lane_row(a_log), lane_row(dt_bias), jnp.tile(norm_g, GDN_HEADS).reshape(1, -1))


def _out_proj_kernel(x_ref, y0_ref, y1_ref, y2_ref, y3_ref, w_ref, o_ref):
    acc = x_ref[...]
    for i, y_ref in enumerate((y0_ref, y1_ref, y2_ref, y3_ref)):
        acc = acc + _bdot(y_ref[...], w_ref[i * BRANCH_W:(i + 1) * BRANCH_W, :])
    o_ref[...] = acc


def _out_proj(x2, ys, w_out):
    t = x2.shape[0]
    row = lambda n: pl.BlockSpec((OUT_TM, n), lambda i: (i, 0))
    return pl.pallas_call(
        _out_proj_kernel,
        grid=(t // OUT_TM,),
        in_specs=[row(D_MODEL)] + [row(BRANCH_W)] * 4 + [pl.BlockSpec((D_MODEL, D_MODEL), lambda i: (0, 0))],
        out_specs=row(D_MODEL),
        out_shape=jax.ShapeDtypeStruct((t, D_MODEL), F32),
        compiler_params=_cparams(("parallel",), 32),
        name="out_proj",
    )(x2, *ys, w_out.astype(BF16))


def _ffn_kernel(x_ref, g_ref, wg_ref, wu_ref, wd_ref, o_ref):
    x = x_ref[...]
    hb = _rms(x, g_ref[...]).astype(BF16)
    acc = x
    for f in range(D_FF // FFN_TF):
        cols = slice(f * FFN_TF, (f + 1) * FFN_TF)
        a = jnp.dot(hb, wg_ref[:, cols], preferred_element_type=F32)
        u = jnp.dot(hb, wu_ref[:, cols], preferred_element_type=F32)
        acc = acc + jnp.dot((_silu(a) * u).astype(BF16), wd_ref[cols, :], preferred_element_type=F32)
    o_ref[...] = acc


def _ffn_dense(x2, g, w_gate, w_up, w_down):
    t = x2.shape[0]
    resident = lambda shape: pl.BlockSpec(shape, lambda i: (0, 0), pipeline_mode=pl.Buffered(1))
    return pl.pallas_call(
        _ffn_kernel,
        grid=(t // FFN_TM,),
        in_specs=[pl.BlockSpec((FFN_TM, D_MODEL), lambda i: (i, 0)),
                  pl.BlockSpec((1, D_MODEL), lambda i: (0, 0)),
                  resident((D_MODEL, D_FF)), resident((D_MODEL, D_FF)), resident((D_FF, D_MODEL))],
        out_specs=pl.BlockSpec((FFN_TM, D_MODEL), lambda i: (i, 0)),
        out_shape=jax.ShapeDtypeStruct((t, D_MODEL), F32),
        compiler_params=_cparams(("parallel",), 48),
        name="ffn_dense",
    )(x2, g.reshape(1, -1), w_gate.astype(BF16), w_up.astype(BF16), w_down.astype(BF16))


def _moe_kernel(be_ref, nu_ref, x_ref, g_ref, wg_ref, wu_ref, wd_ref, o_ref):
    del be_ref

    @pl.when(pl.program_id(0) < nu_ref[0])
    def _():
        xb = _rms(x_ref[...], g_ref[...]).astype(BF16)
        acc = None
        for f in range(D_FF_EXPERT // MOE_TF):
            cols = slice(f * MOE_TF, (f + 1) * MOE_TF)
            a = jnp.dot(xb, wg_ref[:, cols], preferred_element_type=F32)
            u = jnp.dot(xb, wu_ref[:, cols], preferred_element_type=F32)
            y = jnp.dot((_silu(a) * u).astype(BF16), wd_ref[cols, :], preferred_element_type=F32)
            acc = y if acc is None else acc + y
        o_ref[...] = acc

    @pl.when(pl.program_id(0) >= nu_ref[0])
    def _():
        o_ref[...] = jnp.zeros_like(o_ref)


def _moe_experts(buf, g, block_e, n_used, wg, wu, wd):
    rows = buf.shape[0]
    expert = lambda shape: pl.BlockSpec((None,) + shape, lambda m, be, nu: (be[m], 0, 0),
                                        pipeline_mode=pl.Buffered(1))
    grid_spec = pltpu.PrefetchScalarGridSpec(
        num_scalar_prefetch=2,
        grid=(rows // MOE_TM,),
        in_specs=[pl.BlockSpec((MOE_TM, D_MODEL), lambda m, be, nu: (m, 0)),
                  pl.BlockSpec((1, D_MODEL), lambda m, be, nu: (0, 0)),
                  expert((D_MODEL, D_FF_EXPERT)), expert((D_MODEL, D_FF_EXPERT)), expert((D_FF_EXPERT, D_MODEL))],
        out_specs=pl.BlockSpec((MOE_TM, D_MODEL), lambda m, be, nu: (m, 0)),
    )
    return pl.pallas_call(
        _moe_kernel,
        grid_spec=grid_spec,
        out_shape=jax.ShapeDtypeStruct((rows, D_MODEL), F32),
        compiler_params=_cparams(("arbitrary",), 56),
        name="moe_experts",
    )(block_e, n_used, buf, g.reshape(1, -1), wg, wu, wd)


def _route_kernel(x_ref, g_ref, wt_ref, wp_ref, e_ref, r_ref, gc_ref, cnt_ref, run_ref):
    tm = ROUTE_TM

    @pl.when(pl.program_id(0) == 0)
    def _():
        run_ref[...] = jnp.zeros_like(run_ref)

    h = _rms(x_ref[...], g_ref[...])
    lt = lax.dot_general(wt_ref[...], h, (((1,), (1,)), ((), ())), precision=HIGHEST,
                         preferred_element_type=F32)
    sub = lax.broadcasted_iota(jnp.int32, (N_EXPERTS, tm), 0)
    m1 = jnp.max(lt, axis=0, keepdims=True)
    i1 = jnp.min(jnp.where(lt == m1, sub, N_EXPERTS), axis=0, keepdims=True)
    lt2 = jnp.where(sub == i1, -jnp.inf, lt)
    m2 = jnp.max(lt2, axis=0, keepdims=True)
    i2 = jnp.min(jnp.where(lt2 == m2, sub, N_EXPERTS), axis=0, keepdims=True)
    oh0 = jnp.where(sub == i1, 1.0, 0.0)
    oh1 = jnp.where(sub == i2, 1.0, 0.0)
    cnt = oh0 + oh1
    ti = lax.broadcasted_iota(jnp.int32, (tm, tm), 0)
    tj = lax.broadcasted_iota(jnp.int32, (tm, tm), 1)
    upper = jnp.where(ti < tj, 1.0, 0.0).astype(BF16)
    pre = jnp.dot(cnt.astype(BF16), upper, preferred_element_type=F32) + run_ref[:, 0:1]
    r0 = jnp.sum(oh0 * pre, axis=0, keepdims=True)
    r1 = jnp.sum(oh1 * pre, axis=0, keepdims=True)
    e_ref[...] = jnp.concatenate([i1, i2], axis=0)
    r_ref[...] = jnp.concatenate([r0, r1], axis=0).astype(jnp.int32)
    run = run_ref[...] + jnp.sum(cnt, axis=1, keepdims=True)
    run_ref[...] = run
    cnt_ref[...] = run
    lr = jnp.dot(h, wp_ref[...], precision=HIGHEST, preferred_element_type=F32)
    lane = lax.broadcasted_iota(jnp.int32, (tm, LANES), 1)
    lr = jnp.where(lane < N_EXPERTS, lr, -jnp.inf)
    t1 = jnp.max(lr, axis=1, keepdims=True)
    j1 = jnp.min(jnp.where(lr == t1, lane, LANES), axis=1, keepdims=True)
    t2 = jnp.max(jnp.where(lane == j1, -jnp.inf, lr), axis=1, keepdims=True)
    ex = jnp.exp(t2 - t1)
    g0 = 1.0 / (1.0 + ex)
    gc_ref[...] = jnp.where(lane == 0, g0, jnp.where(lane == 1, ex * g0, 0.0))


def _moe_route(x2, ffn_norm, w_router):
    t = x2.shape[0]
    tm = ROUTE_TM
    wp = jnp.zeros((D_MODEL, LANES), F32).at[:, :N_EXPERTS].set(w_router)
    return pl.pallas_call(
        _route_kernel,
        grid=(t // tm,),
        in_specs=[pl.BlockSpec((tm, D_MODEL), lambda i: (i, 0)),
                  pl.BlockSpec((1, D_MODEL), lambda i: (0, 0)),
                  pl.BlockSpec((N_EXPERTS, D_MODEL), lambda i: (0, 0)),
                  pl.BlockSpec((D_MODEL, LANES), lambda i: (0, 0))],
        out_specs=[pl.BlockSpec((TOP_K, tm), lambda i: (0, i)),
                   pl.BlockSpec((TOP_K, tm), lambda i: (0, i)),
                   pl.BlockSpec((tm, LANES), lambda i: (i, 0)),
                   pl.BlockSpec((N_EXPERTS, LANES), lambda i: (0, 0))],
        out_shape=[jax.ShapeDtypeStruct((TOP_K, t), jnp.int32),
                   jax.ShapeDtypeStruct((TOP_K, t), jnp.int32),
                   jax.ShapeDtypeStruct((t, LANES), F32),
                   jax.ShapeDtypeStruct((N_EXPERTS, LANES), F32)],
        scratch_shapes=[pltpu.VMEM((N_EXPERTS, LANES), F32)],
        compiler_params=_cparams(("arbitrary",), 32),
        name="moe_route",
    )(x2, ffn_norm.reshape(1, -1), w_router.T, wp)


def _row_copy(src_ref, src_row, dst_ref, dst_row, sem):
    return pltpu.make_async_copy(src_ref.at[pl.ds(src_row, 1), :], dst_ref.at[pl.ds(dst_row, 1), :], sem)


def _dispatch_kernel(dest_hbm, x_ref, init_hbm, out_hbm, idx_ref, idx_sem, row_sem):
    del init_hbm
    tm = ROUTE_TM
    load = pltpu.make_async_copy(dest_hbm.at[pl.program_id(0)], idx_ref, idx_sem)
    load.start()
    load.wait()

    def issue(t, carry):
        for k in range(TOP_K):
            _row_copy(x_ref, t, out_hbm, idx_ref[k, t], row_sem).start()
        return carry

    lax.fori_loop(0, tm, issue, 0, unroll=8)

    def drain(t, carry):
        for k in range(TOP_K):
            _row_copy(x_ref, t, out_hbm, idx_ref[k, t], row_sem).wait()
        return carry

    lax.fori_loop(0, tm, drain, 0, unroll=8)


def _moe_dispatch(x2, dest3, n_rows):
    t = x2.shape[0]
    tm = ROUTE_TM
    return pl.pallas_call(
        _dispatch_kernel,
        grid=(t // tm,),
        in_specs=[pl.BlockSpec(memory_space=pl.ANY),
                  pl.BlockSpec((tm, D_MODEL), lambda i: (i, 0)),
                  pl.BlockSpec(memory_space=pl.ANY)],
        out_specs=pl.BlockSpec(memory_space=pl.ANY),
        out_shape=jax.ShapeDtypeStruct((n_rows, D_MODEL), F32),
        scratch_shapes=[pltpu.SMEM((TOP_K, tm), jnp.int32), pltpu.SemaphoreType.DMA, pltpu.SemaphoreType.DMA],
        input_output_aliases={2: 0},
        compiler_params=_cparams(("arbitrary",), 32),
        name="moe_dispatch",
    )(dest3, x2, jnp.zeros((n_rows, D_MODEL), F32))


def _combine_kernel(dest_hbm, x_ref, gc_ref, fn_ref, y_hbm, o_ref, idx_ref, ybuf_ref, idx_sem, row_sem):
    tm = ROUTE_TM
    load = pltpu.make_async_copy(dest_hbm.at[pl.program_id(0)], idx_ref, idx_sem)
    load.start()
    load.wait()

    def issue(t, carry):
        for k in range(TOP_K):
            _row_copy(y_hbm, idx_ref[k, t], ybuf_ref.at[k], t, row_sem).start()
        return carry

    lax.fori_loop(0, tm, issue, 0, unroll=8)

    def drain(t, carry):
        for k in range(TOP_K):
            _row_copy(y_hbm, idx_ref[k, t], ybuf_ref.at[k], t, row_sem).wait()
        return carry

    lax.fori_loop(0, tm, drain, 0, unroll=8)
    gc = gc_ref[...]
    x = x_ref[...] + gc[:, 0:1] * ybuf_ref[0] + gc[:, 1:2] * ybuf_ref[1]
    o_ref[...] = _rms(x, fn_ref[...])


def _moe_combine_norm(x2, y_buf, dest3, gcol, final_norm):
    t = x2.shape[0]
    tm = ROUTE_TM
    return pl.pallas_call(
        _combine_kernel,
        grid=(t // tm,),
        in_specs=[pl.BlockSpec(memory_space=pl.ANY),
                  pl.BlockSpec((tm, D_MODEL), lambda i: (i, 0)),
                  pl.BlockSpec((tm, LANES), lambda i: (i, 0)),
                  pl.BlockSpec((1, D_MODEL), lambda i: (0, 0)),
                  pl.BlockSpec(memory_space=pl.ANY)],
        out_specs=pl.BlockSpec((tm, D_MODEL), lambda i: (i, 0)),
        out_shape=jax.ShapeDtypeStruct((t, D_MODEL), F32),
        scratch_shapes=[pltpu.SMEM((TOP_K, tm), jnp.int32), pltpu.VMEM((TOP_K, tm, D_MODEL), F32),
                        pltpu.SemaphoreType.DMA, pltpu.SemaphoreType.DMA],
        compiler_params=_cparams(("arbitrary",), 32),
        name="moe_combine_norm",
    )(dest3, x2, gcol, final_norm.reshape(1, -1), y_buf)


def _moe_layer(x2, ffn_norm, w_router, w_gate, w_up, w_down, final_norm):
    t = x2.shape[0]
    tm = ROUTE_TM
    e01, r01, gcol, cnt = _moe_route(x2, ffn_norm, w_router)
    counts = cnt[:, 0].astype(jnp.int32)
    padded = (counts + MOE_TM - 1) // MOE_TM * MOE_TM
    pad_ends = jnp.cumsum(padded)
    pad_starts = pad_ends - padded
    n_blocks = -(-t * TOP_K // MOE_TM) + N_EXPERTS
    blk_start = jnp.arange(n_blocks, dtype=jnp.int32) * MOE_TM
    block_e = jnp.minimum(jnp.sum(blk_start[:, None] >= pad_ends[None, :], axis=1), N_EXPERTS - 1).astype(jnp.int32)
    n_used = (pad_ends[-1] // MOE_TM).astype(jnp.int32).reshape(1)
    dest = r01 + jnp.sum(jnp.where(e01[None] == jnp.arange(N_EXPERTS)[:, None, None], pad_starts[:, None, None], 0),
                         axis=0)
    dest3 = dest.reshape(TOP_K, t // tm, tm).transpose(1, 0, 2)
    xs = _moe_dispatch(x2, dest3, n_blocks * MOE_TM)
    y_buf = _moe_experts(xs, ffn_norm, block_e, n_used, w_gate.astype(BF16), w_up.astype(BF16),
                         w_down.astype(BF16))
    return _moe_combine_norm(x2, y_buf, dest3, gcol, final_norm)


def _permute_w_in(w_in):
    s5 = w_in[:, 0:256]
    gm = w_in[:, 256:768]
    qkv = w_in[:, 768:1536]
    z = w_in[:, 1536:1792]
    a = w_in[:, 1792:1796]
    b = w_in[:, 1796:1800]
    sc = w_in[:, 1800:2568]
    ab = jnp.zeros((D_MODEL, LANES), F32).at[:, 0:GDN_HEADS].set(a).at[:, AB_B_LANE:AB_B_LANE + GDN_HEADS].set(b)
    return jnp.concatenate([gm, s5, z, qkv, sc, ab], axis=1).astype(BF16)


def kernel(x, mix_norm, w_in, s5_lam_re, s5_lam_im, s5_log_step, s5_b_re, s5_b_im, s5_c_re, s5_c_im, s5_d, s5_w_glu, s5_b_glu, s5_out_norm, sgu_ln_g, sgu_ln_b, sgu_w, sgu_b, gmlp_out_norm, gdn_conv, gdn_a_log, gdn_dt_bias, gdn_norm, sc_conv, sc_out_norm, w_out, ffn_norm, ffn_w_gate, ffn_w_up, ffn_w_down, moe_router, moe_w_gate, moe_w_up, moe_w_down, final_norm):
    bsz, seqlen, d = x.shape
    t = bsz * seqlen
    x2 = x.reshape(t, d)
    out = None
    for l in range(DEPTH):
        p2 = _in_proj(x2, mix_norm[l].reshape(1, -1), _permute_w_in(w_in[l]))
        p3 = p2.reshape(bsz, seqlen, P_COLS)
        wb, wc, pw = _s5_params(s5_lam_re[l], s5_lam_im[l], s5_log_step[l], s5_b_re[l], s5_b_im[l],
                                s5_c_re[l], s5_c_im[l])
        y_s5 = _s5_mixer(p3, wb, wc, pw, s5_d[l], s5_w_glu[l], s5_b_glu[l], s5_out_norm[l])
        y_gm = _gmlp_mixer(p3, sgu_ln_g[l], sgu_ln_b[l], sgu_w[l], sgu_b[l], gmlp_out_norm[l])
        y_gdn = _gdn_mixer(p3, gdn_conv[l], gdn_a_log[l], gdn_dt_bias[l], gdn_norm[l])
        y_sc = _shortconv_mixer(p3, sc_conv[l], sc_out_norm[l])
        ys = [y.reshape(t, BRANCH_W) for y in (y_s5, y_gm, y_gdn, y_sc)]
        x2 = _out_proj(x2, ys, w_out[l])
        i = l // 2
        if l % 2 == 0:
            x2 = _ffn_dense(x2, ffn_norm[l], ffn_w_gate[i], ffn_w_up[i], ffn_w_down[i])
        else:
            out = _moe_layer(x2, ffn_norm[l], moe_router[i], moe_w_gate[i], moe_w_up[i], moe_w_down[i],
                             final_norm)
    return out.reshape(bsz, seqlen, d)
```

```python
import functools
import math

import jax
import jax.numpy as jnp
from jax import lax
from jax.experimental import pallas as pl
from jax.experimental.pallas import tpu as pltpu

F32 = jnp.float32
BF16 = jnp.bfloat16
HIGHEST = lax.Precision.HIGHEST

D_MODEL = 1024
DEPTH = 2
BRANCH_W = 256
S5_GROUP = 16
S5_GROUPS = 16
S5_STATE = 64
S5_NSTATE = S5_GROUPS * S5_STATE
GMLP_HEADS = 4
GMLP_HEAD_DIM = 64
GMLP_CHUNK = 128
GDN_HEAD_DIM = 64
GDN_HEADS = 4
GDN_CONV = 4
GDN_CHUNK = 64
SC_CONV = 3
D_FF = 2816
N_EXPERTS = 8
TOP_K = 2
D_FF_EXPERT = 3584
EPS = 1e-6

LANES = 128
SUBLANES = 8
VMEM_BYTES_V7X = 64 * 1024 * 1024

COL_GM = 0
COL_S5 = 512
COL_Z = 768
COL_Q = 1024
COL_K = 1280
COL_V = 1536
COL_SCB = 1792
COL_SCC = 2048
COL_SCX = 2304
COL_AB = 2560
P_COLS = 2688
AB_B_LANE = 64

IN_TM = 512
S5_CHUNK = 128
GM_TILE = 512
SC_TILE = 512
GDN_TILE = 256
OUT_TM = 512
FFN_TM = 512
FFN_TF = 256
MOE_TM = 512
MOE_TF = 512
ROUTE_TM = 512


def _cparams(sem, vmem_mb):
    return pltpu.CompilerParams(dimension_semantics=sem, vmem_limit_bytes=vmem_mb * 1024 * 1024)


def _rms(x, g):
    return x * lax.rsqrt(jnp.mean(x * x, axis=-1, keepdims=True) + EPS) * g


def _silu(x):
    return x * jax.nn.sigmoid(x)


def _bdot(a, b):
    return jnp.dot(a.astype(BF16), b.astype(BF16), preferred_element_type=F32)


def _split_dot(a, b, passes, data_on_left):
    data = a if data_on_left else b
    acc = None
    for _ in range(passes):
        piece = data.astype(BF16)
        term = (jnp.dot(piece, b, preferred_element_type=F32) if data_on_left
                else jnp.dot(a, piece, preferred_element_type=F32))
        acc = term if acc is None else acc + term
        data = data - piece.astype(F32)
    return acc


def _bdot_nt(a, b):
    return lax.dot_general(a.astype(BF16), b.astype(BF16), (((1,), (1,)), ((), ())),
                           preferred_element_type=F32)


def _in_proj_kernel(x_ref, g_ref, w_ref, o_ref):
    h = _rms(x_ref[...], g_ref[...])
    o_ref[...] = _bdot(h, w_ref[...])


def _in_proj(x2, g, w):
    t = x2.shape[0]
    return pl.pallas_call(
        _in_proj_kernel,
        grid=(t // IN_TM,),
        in_specs=[pl.BlockSpec((IN_TM, D_MODEL), lambda i: (i, 0)),
                  pl.BlockSpec((1, D_MODEL), lambda i: (0, 0)),
                  pl.BlockSpec((D_MODEL, P_COLS), lambda i: (0, 0))],
        out_specs=pl.BlockSpec((IN_TM, P_COLS), lambda i: (i, 0)),
        out_shape=jax.ShapeDtypeStruct((t, P_COLS), F32),
        compiler_params=_cparams(("parallel",), 48),
        name="in_proj",
    )(x2, g, w)


def _s5_kernel(u_ref, wb_ref, pw_ref, wc_ref, d_ref, wglu_ref, bglu_ref, on_ref, o_ref, sr_ref, si_ref):
    n = S5_NSTATE

    @pl.when(pl.program_id(1) == 0)
    def _():
        sr_ref[...] = jnp.zeros_like(sr_ref)
        si_ref[...] = jnp.zeros_like(si_ref)

    u = u_ref[...]
    bu = _bdot(u, wb_ref[...])
    n_steps = int(math.log2(SUBLANES))
    cr, ci = sr_ref[...], si_ref[...]
    grs, gis = [], []
    for j in range(S5_CHUNK // SUBLANES):
        rows = slice(j * SUBLANES, (j + 1) * SUBLANES)
        gr, gi = bu[rows, :n], bu[rows, n:]
        for s in range(n_steps):
            tr, ti = pw_ref[s, :, :n], pw_ref[s, :, n:]
            rr, ri = pltpu.roll(gr, 1 << s, 0), pltpu.roll(gi, 1 << s, 0)
            gr, gi = gr + tr * rr - ti * ri, gi + tr * ri + ti * rr
        pr, pi = pw_ref[n_steps, :, :n], pw_ref[n_steps, :, n:]
        br, bi = jnp.broadcast_to(cr, (SUBLANES, n)), jnp.broadcast_to(ci, (SUBLANES, n))
        gr, gi = gr + pr * br - pi * bi, gi + pr * bi + pi * br
        cr, ci = gr[SUBLANES - 1:, :], gi[SUBLANES - 1:, :]
        grs.append(gr)
        gis.append(gi)
    sr_ref[...] = cr
    si_ref[...] = ci
    xr, xi = jnp.concatenate(grs, axis=0), jnp.concatenate(gis, axis=0)
    y = _bdot(xr, wc_ref[:n, :]) + _bdot(xi, wc_ref[n:, :])
    y = y + d_ref[...] * u
    y = jax.nn.gelu(y)
    y = y * jax.nn.sigmoid(_bdot(y, wglu_ref[...]) + bglu_ref[...])
    o_ref[...] = _rms(y, on_ref[...])


def _s5_params(lam_re, lam_im, log_step, b_re, b_im, c_re, c_im):
    g, n, p = S5_GROUPS, S5_STATE, S5_GROUP
    dt = jnp.exp(log_step)[:, None]
    mag = jnp.exp(lam_re * dt)
    ar, ai = mag * jnp.cos(lam_im * dt), mag * jnp.sin(lam_im * dt)
    den = lam_re * lam_re + lam_im * lam_im
    fr = ((ar - 1.0) * lam_re + ai * lam_im) / den
    fi = (ai * lam_re - (ar - 1.0) * lam_im) / den
    bbr = fr[..., None] * b_re - fi[..., None] * b_im
    bbi = fr[..., None] * b_im + fi[..., None] * b_re
    eye = jnp.eye(g, dtype=F32)
    wbr = jnp.einsum('gnp,gh->gphn', bbr, eye).reshape(g * p, g * n)
    wbi = jnp.einsum('gnp,gh->gphn', bbi, eye).reshape(g * p, g * n)
    wb = jnp.concatenate([wbr, wbi], axis=1)
    wcr = jnp.einsum('gpn,gh->gnhp', c_re, eye).reshape(g * n, g * p)
    wci = jnp.einsum('gpn,gh->gnhp', -c_im, eye).reshape(g * n, g * p)
    wc = jnp.concatenate([wcr, wci], axis=0)
    a1r, a1i = ar.reshape(1, g * n), ai.reshape(1, g * n)
    r_idx = jnp.arange(SUBLANES)[:, None]
    tables = []
    pr, pi = a1r, a1i
    for s in range(int(math.log2(SUBLANES))):
        keep = r_idx >= (1 << s)
        tables.append(jnp.concatenate([jnp.where(keep, pr, 0.0), jnp.where(keep, pi, 0.0)], axis=1))
        pr, pi = pr * pr - pi * pi, 2.0 * pr * pi
    rows_r, rows_i = [a1r], [a1i]
    for _ in range(SUBLANES - 1):
        qr, qi = rows_r[-1], rows_i[-1]
        rows_r.append(qr * a1r - qi * a1i)
        rows_i.append(qr * a1i + qi * a1r)
    tables.append(jnp.concatenate([jnp.concatenate(rows_r, axis=0), jnp.concatenate(rows_i, axis=0)], axis=1))
    pw = jnp.stack(tables, axis=0)
    return wb.astype(BF16), wc.astype(BF16), pw


def _s5_mixer(p3, wb, wc, pw, d_skip, w_glu, b_glu, out_norm):
    b, l, _ = p3.shape
    n2 = 2 * S5_NSTATE
    const = lambda shape: pl.BlockSpec(shape, lambda i, j: (0,) * len(shape))
    return pl.pallas_call(
        _s5_kernel,
        grid=(b, l // S5_CHUNK),
        in_specs=[pl.BlockSpec((None, S5_CHUNK, BRANCH_W), lambda i, j: (i, j, COL_S5 // BRANCH_W)),
                  const((BRANCH_W, n2)), const((int(math.log2(SUBLANES)) + 1, SUBLANES, n2)), const((n2, BRANCH_W)),
                  const((1, BRANCH_W)), const((BRANCH_W, BRANCH_W)), const((1, BRANCH_W)),
                  const((1, BRANCH_W))],
        out_specs=pl.BlockSpec((None, S5_CHUNK, BRANCH_W), lambda i, j: (i, j, 0)),
        out_shape=jax.ShapeDtypeStruct((b, l, BRANCH_W), F32),
        scratch_shapes=[pltpu.VMEM((1, S5_NSTATE), F32), pltpu.VMEM((1, S5_NSTATE), F32)],
        compiler_params=_cparams(("parallel", "arbitrary"), 40),
        name="s5_mixer",
    )(p3, wb, pw, wc, d_skip.reshape(1, -1), w_glu.astype(BF16), b_glu.reshape(1, -1),
      out_norm.reshape(1, -1))


def _gmlp_kernel(p_ref, lng_ref, lnb_ref, w_ref, bias_ref, on_ref, o_ref):
    z = jax.nn.gelu(p_ref[...])
    u, v = z[:, :BRANCH_W], z[:, BRANCH_W:]
    vc = v - jnp.mean(v, axis=-1, keepdims=True)
    v = vc * lax.rsqrt(jnp.mean(vc * vc, axis=-1, keepdims=True) + EPS) * lng_ref[...] + lnb_ref[...]
    ti = lax.broadcasted_iota(jnp.int32, (GMLP_CHUNK, GMLP_CHUNK), 0)
    si = lax.broadcasted_iota(jnp.int32, (GMLP_CHUNK, GMLP_CHUNK), 1)
    tril = ti >= si
    ws = [jnp.where(tril, w_ref[h], 0.0).astype(BF16) for h in range(GMLP_HEADS)]
    lane = lax.broadcasted_iota(jnp.int32, (GMLP_CHUNK, BRANCH_W), 1)
    bias = bias_ref[...]
    outs = []
    for c in range(GM_TILE // GMLP_CHUNK):
        vb = v[c * GMLP_CHUNK:(c + 1) * GMLP_CHUNK, :].astype(BF16)
        s = jnp.dot(ws[GMLP_HEADS - 1], vb, preferred_element_type=F32)
        for h in range(GMLP_HEADS - 2, -1, -1):
            sh = jnp.dot(ws[h], vb, preferred_element_type=F32)
            s = jnp.where(lane < (h + 1) * GMLP_HEAD_DIM, sh, s)
        outs.append(s + bias)
    s = jnp.concatenate(outs, axis=0)
    o_ref[...] = _rms(u * s, on_ref[...])


def _gmlp_mixer(p3, ln_g, ln_b, w_sp, b_sp, out_norm):
    b, l, _ = p3.shape
    bias = jnp.repeat(b_sp.T, GMLP_HEAD_DIM, axis=1)
    const = lambda shape: pl.BlockSpec(shape, lambda i, j: (0,) * len(shape))
    return pl.pallas_call(
        _gmlp_kernel,
        grid=(b, l // GM_TILE),
        in_specs=[pl.BlockSpec((None, GM_TILE, 2 * BRANCH_W), lambda i, j: (i, j, COL_GM // (2 * BRANCH_W))),
                  const((1, BRANCH_W)), const((1, BRANCH_W)),
                  const((GMLP_HEADS, GMLP_CHUNK, GMLP_CHUNK)), const((GMLP_CHUNK, BRANCH_W)),
                  const((1, BRANCH_W))],
        out_specs=pl.BlockSpec((None, GM_TILE, BRANCH_W), lambda i, j: (i, j, 0)),
        out_shape=jax.ShapeDtypeStruct((b, l, BRANCH_W), F32),
        compiler_params=_cparams(("parallel", "parallel"), 32),
        name="gmlp_mixer",
    )(p3, ln_g.reshape(1, -1), ln_b.reshape(1, -1), w_sp, bias, out_norm.reshape(1, -1))


def _shortconv_kernel(b_ref, c_ref, x_ref, w_ref, on_ref, o_ref, halo_ref):
    @pl.when(pl.program_id(1) == 0)
    def _():
        halo_ref[...] = jnp.zeros_like(halo_ref)

    cx = c_ref[...] * x_ref[...]
    ext = jnp.concatenate([halo_ref[...], cx], axis=0)
    halo_ref[...] = cx[SC_TILE - SUBLANES:, :]
    y = w_ref[SC_CONV - 1:SC_CONV, :] * cx
    for j in range(SC_CONV - 1):
        sh = SC_CONV - 1 - j
        y = y + w_ref[j:j + 1, :] * pltpu.roll(ext, sh, 0)[SUBLANES:, :]
    o_ref[...] = _rms(b_ref[...] * y, on_ref[...])


def _shortconv_mixer(p3, conv_w, out_norm):
    b, l, _ = p3.shape
    w = jnp.concatenate([conv_w, jnp.zeros((SUBLANES - SC_CONV, BRANCH_W), F32)], axis=0)
    col = lambda c: pl.BlockSpec((None, SC_TILE, BRANCH_W), lambda i, j: (i, j, c // BRANCH_W))
    const = lambda shape: pl.BlockSpec(shape, lambda i, j: (0,) * len(shape))
    return pl.pallas_call(
        _shortconv_kernel,
        grid=(b, l // SC_TILE),
        in_specs=[col(COL_SCB), col(COL_SCC), col(COL_SCX), const((SUBLANES, BRANCH_W)), const((1, BRANCH_W))],
        out_specs=pl.BlockSpec((None, SC_TILE, BRANCH_W), lambda i, j: (i, j, 0)),
        out_shape=jax.ShapeDtypeStruct((b, l, BRANCH_W), F32),
        scratch_shapes=[pltpu.VMEM((SUBLANES, BRANCH_W), F32)],
        compiler_params=_cparams(("parallel", "arbitrary"), 32),
        name="shortconv_mixer",
    )(p3, p3, p3, w, out_norm.reshape(1, -1))


def _gdn_kernel(q_ref, k_ref, v_ref, z_ref, ab_ref, cw_ref, alog_ref, dtb_ref, ng_ref, o_ref,
                halo_ref, state_ref):
    c_len, h_dim, nh, w = GDN_CHUNK, GDN_HEAD_DIM, GDN_HEADS, BRANCH_W
    n_chunks = GDN_TILE // c_len

    @pl.when(pl.program_id(1) == 0)
    def _():
        halo_ref[...] = jnp.zeros_like(halo_ref)
        state_ref[...] = jnp.zeros_like(state_ref)

    shift = int(math.log2(h_dim))
    r256 = lax.broadcasted_iota(jnp.int32, (w, w), 0)
    c256 = lax.broadcasted_iota(jnp.int32, (w, w), 1)
    same_head = (r256 >> shift) == (c256 >> shift)
    ebd = jnp.where(same_head, 1.0, 0.0).astype(BF16)
    causal = same_head & (r256 >= c256)
    strict = same_head & (r256 > c256)
    eye = jnp.where(r256 == c256, 1.0, 0.0)
    r128 = lax.broadcasted_iota(jnp.int32, (LANES, w), 0)
    c128 = lax.broadcasted_iota(jnp.int32, (LANES, w), 1)
    exp_a = jnp.where(r128 == (c128 >> shift), 1.0, 0.0).astype(BF16)
    exp_b = jnp.where(r128 == (c128 >> shift) + AB_B_LANE, 1.0, 0.0).astype(BF16)

    def stack(a):
        lane_head = (lax.broadcasted_iota(jnp.int32, a.shape, 1) >> shift) & (nh - 1)
        return jnp.concatenate([jnp.where(lane_head == h, a, 0.0) for h in range(nh)], axis=0)

    def unstack(a):
        return a[0:c_len] + a[c_len:2 * c_len] + a[2 * c_len:3 * c_len] + a[3 * c_len:4 * c_len]

    def conv_silu(ref, col):
        cur = ref[...]
        ext = jnp.concatenate([halo_ref[:, col * w:(col + 1) * w], cur], axis=0)
        halo_ref[:, col * w:(col + 1) * w] = cur[GDN_TILE - SUBLANES:, :]
        y = cw_ref[GDN_CONV - 1:GDN_CONV, col * w:(col + 1) * w] * cur
        for j in range(GDN_CONV - 1):
            sh = GDN_CONV - 1 - j
            y = y + cw_ref[j:j + 1, col * w:(col + 1) * w] * pltpu.roll(ext, sh, 0)[SUBLANES:, :]
        return _silu(y)

    q = conv_silu(q_ref, 0)
    k = conv_silu(k_ref, 1)
    v = conv_silu(v_ref, 2)
    q = q * lax.rsqrt(jnp.dot((q * q).astype(BF16), ebd, preferred_element_type=F32) + EPS) * (h_dim ** -0.5)
    k = k * lax.rsqrt(jnp.dot((k * k).astype(BF16), ebd, preferred_element_type=F32) + EPS)

    ab = ab_ref[...]
    beta = jax.nn.sigmoid(ab)
    xa = ab + dtb_ref[...]
    softplus = jnp.maximum(xa, 0.0) + jnp.log(1.0 + jnp.exp(-jnp.abs(xa)))
    g = -jnp.exp(alog_ref[...]) * softplus
    rt = lax.broadcasted_iota(jnp.int32, (GDN_TILE, GDN_TILE), 0)
    ct = lax.broadcasted_iota(jnp.int32, (GDN_TILE, GDN_TILE), 1)
    ltri = jnp.where(((rt >> shift) == (ct >> shift)) & (rt >= ct), 1.0, 0.0).astype(BF16)
    gcum = _split_dot(ltri, g, 3, data_on_left=False)
    gexp = _split_dot(gcum, exp_a, 2, data_on_left=True)
    bexp = _split_dot(beta, exp_b, 2, data_on_left=True)

    outs = []
    state = state_ref[...]
    q_decs, k_decs, g_tots, rhss, a_mats, qkds = [], [], [], [], [], []
    for c in range(n_chunks):
        sl = slice(c * c_len, (c + 1) * c_len)
        qc, kc, vc, gc, bc = q[sl], k[sl], v[sl], gexp[sl], bexp[sl]
        glast = gc[c_len - 1:c_len, :]
        eg = jnp.exp(gc)
        q_decs.append(qc * eg)
        k_decs.append(kc * jnp.exp(glast - gc))
        g_tots.append(jnp.exp(glast))
        kst = stack(kc).astype(BF16)
        kk = _bdot_nt(kst, kst)
        qk = _bdot_nt(stack(qc), kst)
        gcol = stack(gc)
        bcol = stack(bc)
        dec = jnp.exp(jnp.where(causal, gcol - gcol.T, -jnp.inf))
        a_mats.append(jnp.where(strict, bcol * kk * dec, 0.0))
        qkds.append((qk * dec).astype(BF16))
        tile4 = lambda a: jnp.concatenate([a] * nh, axis=0)
        rhss.append(jnp.concatenate([tile4(vc) * bcol, tile4(kc * eg) * bcol], axis=1).astype(BF16))
    t_invs = [eye - a for a in a_mats]
    pws = a_mats
    for _ in range(int(math.log2(c_len)) - 1):
        pws = [_bdot(p, p) for p in pws]
        t_invs = [t + _bdot(t, p) for t, p in zip(t_invs, pws)]
    sols = [unstack(_bdot(t, r)) for t, r in zip(t_invs, rhss)]
    for c in range(n_chunks):
        w_v, w_k = sols[c][:, :w], sols[c][:, w:]
        v_new = w_v - _bdot(w_k, state)
        o = _bdot(q_decs[c], state) + unstack(_bdot(qkds[c], stack(v_new)))
        state = state * g_tots[c] + jnp.where(same_head, _bdot(k_decs[c].T, v_new), 0.0)
        outs.append(o)
    state_ref[...] = state
    o = jnp.concatenate(outs, axis=0)
    ms = jnp.dot((o * o).astype(BF16), ebd, preferred_element_type=F32) * (1.0 / h_dim)
    o = o * lax.rsqrt(ms + EPS) * ng_ref[...]
    o_ref[...] = o * _silu(z_ref[...])


def _gdn_mixer(p3, conv_w, a_log, dt_bias, norm_g):
    b, l, _ = p3.shape
    cw = jnp.concatenate([conv_w, jnp.zeros((SUBLANES - GDN_CONV, 3 * BRANCH_W), F32)], axis=0)
    lane_row = lambda vec: jnp.zeros((1, LANES), F32).at[0, :GDN_HEADS].set(vec)
    col = lambda c: pl.BlockSpec((None, GDN_TILE, BRANCH_W), lambda i, j: (i, j, c // BRANCH_W))
    const = lambda shape: pl.BlockSpec(shape, lambda i, j: (0,) * len(shape))
    return pl.pallas_call(
        _gdn_kernel,
        grid=(b, l // GDN_TILE),
        in_specs=[col(COL_Q), col(COL_K), col(COL_V), col(COL_Z),
                  pl.BlockSpec((None, GDN_TILE, LANES), lambda i, j: (i, j, COL_AB // LANES)),
                  const((SUBLANES, 3 * BRANCH_W)), const((1, LANES)), const((1, LANES)), const((1, BRANCH_W))],
        out_specs=pl.BlockSpec((None, GDN_TILE, BRANCH_W), lambda i, j: (i, j, 0)),
        out_shape=jax.ShapeDtypeStruct((b, l, BRANCH_W), F32),
        scratch_shapes=[pltpu.VMEM((SUBLANES, 3 * BRANCH_W), F32), pltpu.VMEM((BRANCH_W, BRANCH_W), F32)],
        compiler_params=_cparams(("parallel", "arbitrary"), 48),
        name="gdn_mixer",
    )(p3, p3, p3, p3, p3, cw, lane_row(a_log), lane_row(dt_bias), jnp.tile(norm_g, GDN_HEADS).reshape(1, -1))


def _out_proj_kernel(x_ref, y0_ref, y1_ref, y2_ref, y3_ref, w_ref, o_ref):
    acc = x_ref[...]
    for i, y_ref in enumerate((y0_ref, y1_ref, y2_ref, y3_ref)):
        acc = acc + _bdot(y_ref[...], w_ref[i * BRANCH_W:(i + 1) * BRANCH_W, :])
    o_ref[...] = acc


def _out_proj(x2, ys, w_out):
    t = x2.shape[0]
    row = lambda n: pl.BlockSpec((OUT_TM, n), lambda i: (i, 0))
    return pl.pallas_call(
        _out_proj_kernel,
        grid=(t // OUT_TM,),
        in_specs=[row(D_MODEL)] + [row(BRANCH_W)] * 4 + [pl.BlockSpec((D_MODEL, D_MODEL), lambda i: (0, 0))],
        out_specs=row(D_MODEL),
        out_shape=jax.ShapeDtypeStruct((t, D_MODEL), F32),
        compiler_params=_cparams(("parallel",), 32),
        name="out_proj",
    )(x2, *ys, w_out.astype(BF16))


def _ffn_kernel(x_ref, g_ref, wg_ref, wu_ref, wd_ref, o_ref):
    x = x_ref[...]
    hb = _rms(x, g_ref[...]).astype(BF16)
    acc = x
    for f in range(D_FF // FFN_TF):
        cols = slice(f * FFN_TF, (f + 1) * FFN_TF)
        a = jnp.dot(hb, wg_ref[:, cols], preferred_element_type=F32)
        u = jnp.dot(hb, wu_ref[:, cols], preferred_element_type=F32)
        acc = acc + jnp.dot((_silu(a) * u).astype(BF16), wd_ref[cols, :], preferred_element_type=F32)
    o_ref[...] = acc


def _ffn_dense(x2, g, w_gate, w_up, w_down):
    t = x2.shape[0]
    resident = lambda shape: pl.BlockSpec(shape, lambda i: (0, 0), pipeline_mode=pl.Buffered(1))
    return pl.pallas_call(
        _ffn_kernel,
        grid=(t // FFN_TM,),
        in_specs=[pl.BlockSpec((FFN_TM, D_MODEL), lambda i: (i, 0)),
                  pl.BlockSpec((1, D_MODEL), lambda i: (0, 0)),
                  resident((D_MODEL, D_FF)), resident((D_MODEL, D_FF)), resident((D_FF, D_MODEL))],
        out_specs=pl.BlockSpec((FFN_TM, D_MODEL), lambda i: (i, 0)),
        out_shape=jax.ShapeDtypeStruct((t, D_MODEL), F32),
        compiler_params=_cparams(("parallel",), 48),
        name="ffn_dense",
    )(x2, g.reshape(1, -1), w_gate.astype(BF16), w_up.astype(BF16), w_down.astype(BF16))


def _moe_kernel(be_ref, nu_ref, x_ref, g_ref, wg_ref, wu_ref, wd_ref, o_ref):
    del be_ref

    @pl.when(pl.program_id(0) < nu_ref[0])
    def _():
        xb = _rms(x_ref[...], g_ref[...]).astype(BF16)
        acc = None
        for f in range(D_FF_EXPERT // MOE_TF):
            cols = slice(f * MOE_TF, (f + 1) * MOE_TF)
            a = jnp.dot(xb, wg_ref[:, cols], preferred_element_type=F32)
            u = jnp.dot(xb, wu_ref[:, cols], preferred_element_type=F32)
            y = jnp.dot((_silu(a) * u).astype(BF16), wd_ref[cols, :], preferred_element_type=F32)
            acc = y if acc is None else acc + y
        o_ref[...] = acc

    @pl.when(pl.program_id(0) >= nu_ref[0])
    def _():
        o_ref[...] = jnp.zeros_like(o_ref)


def _moe_experts(buf, g, block_e, n_used, wg, wu, wd):
    rows = buf.shape[0]
    expert = lambda shape: pl.BlockSpec((None,) + shape, lambda m, be, nu: (be[m], 0, 0),
                                        pipeline_mode=pl.Buffered(1))
    grid_spec = pltpu.PrefetchScalarGridSpec(
        num_scalar_prefetch=2,
        grid=(rows // MOE_TM,),
        in_specs=[pl.BlockSpec((MOE_TM, D_MODEL), lambda m, be, nu: (jnp.minimum(m, nu[0] - 1), 0)),
                  pl.BlockSpec((1, D_MODEL), lambda m, be, nu: (0, 0)),
                  expert((D_MODEL, D_FF_EXPERT)), expert((D_MODEL, D_FF_EXPERT)), expert((D_FF_EXPERT, D_MODEL))],
        out_specs=pl.BlockSpec((MOE_TM, D_MODEL), lambda m, be, nu: (m, 0)),
    )
    return pl.pallas_call(
        _moe_kernel,
        grid_spec=grid_spec,
        out_shape=jax.ShapeDtypeStruct((rows, D_MODEL), F32),
        compiler_params=_cparams(("arbitrary",), 56),
        name="moe_experts",
    )(block_e, n_used, buf, g.reshape(1, -1), wg, wu, wd)


def _route_kernel(x_ref, g_ref, wt_ref, e_ref, r_ref, gc_ref, cnt_ref, run_ref):
    tm = ROUTE_TM

    @pl.when(pl.program_id(0) == 0)
    def _():
        run_ref[...] = jnp.zeros_like(run_ref)

    h = _rms(x_ref[...], g_ref[...])
    lt = lax.dot_general(wt_ref[...], h, (((1,), (1,)), ((), ())), precision=HIGHEST,
                         preferred_element_type=F32)
    sub = lax.broadcasted_iota(jnp.int32, (N_EXPERTS, tm), 0)
    m1 = jnp.max(lt, axis=0, keepdims=True)
    i1 = jnp.min(jnp.where(lt == m1, sub, N_EXPERTS), axis=0, keepdims=True)
    lt2 = jnp.where(sub == i1, -jnp.inf, lt)
    m2 = jnp.max(lt2, axis=0, keepdims=True)
    i2 = jnp.min(jnp.where(lt2 == m2, sub, N_EXPERTS), axis=0, keepdims=True)
    oh0 = jnp.where(sub == i1, 1.0, 0.0)
    oh1 = jnp.where(sub == i2, 1.0, 0.0)
    cnt = oh0 + oh1
    ti = lax.broadcasted_iota(jnp.int32, (tm, tm), 0)
    tj = lax.broadcasted_iota(jnp.int32, (tm, tm), 1)
    upper = jnp.where(ti < tj, 1.0, 0.0).astype(BF16)
    pre = jnp.dot(cnt.astype(BF16), upper, preferred_element_type=F32) + run_ref[:, 0:1]
    r0 = jnp.sum(oh0 * pre, axis=0, keepdims=True)
    r1 = jnp.sum(oh1 * pre, axis=0, keepdims=True)
    e_ref[...] = jnp.concatenate([i1, i2], axis=0)
    r_ref[...] = jnp.concatenate([r0, r1], axis=0).astype(jnp.int32)
    run = run_ref[...] + jnp.sum(cnt, axis=1, keepdims=True)
    run_ref[...] = run
    cnt_ref[...] = run
    ex = jnp.exp(m2 - m1)
    g0 = 1.0 / (1.0 + ex)
    gates = jnp.where(sub == 0, g0, jnp.where(sub == 1, ex * g0, 0.0))
    er = lax.broadcasted_iota(jnp.int32, (N_EXPERTS, LANES), 0)
    ec = lax.broadcasted_iota(jnp.int32, (N_EXPERTS, LANES), 1)
    eye = jnp.where(er == ec, 1.0, 0.0).astype(BF16)
    acc = None
    for _ in range(3):
        piece = gates.astype(BF16)
        term = lax.dot_general(piece, eye, (((0,), (0,)), ((), ())), preferred_element_type=F32)
        acc = term if acc is None else acc + term
        gates = gates - piece.astype(F32)
    gc_ref[...] = acc


def _moe_route(x2, ffn_norm, w_router):
    t = x2.shape[0]
    tm = ROUTE_TM
    return pl.pallas_call(
        _route_kernel,
        grid=(t // tm,),
        in_specs=[pl.BlockSpec((tm, D_MODEL), lambda i: (i, 0)),
                  pl.BlockSpec((1, D_MODEL), lambda i: (0, 0)),
                  pl.BlockSpec((N_EXPERTS, D_MODEL), lambda i: (0, 0))],
        out_specs=[pl.BlockSpec((TOP_K, tm), lambda i: (0, i)),
                   pl.BlockSpec((TOP_K, tm), lambda i: (0, i)),
                   pl.BlockSpec((tm, LANES), lambda i: (i, 0)),
                   pl.BlockSpec((N_EXPERTS, LANES), lambda i: (0, 0))],
        out_shape=[jax.ShapeDtypeStruct((TOP_K, t), jnp.int32),
                   jax.ShapeDtypeStruct((TOP_K, t), jnp.int32),
                   jax.ShapeDtypeStruct((t, LANES), F32),
                   jax.ShapeDtypeStruct((N_EXPERTS, LANES), F32)],
        scratch_shapes=[pltpu.VMEM((N_EXPERTS, LANES), F32)],
        compiler_params=_cparams(("arbitrary",), 32),
        name="moe_route",
    )(x2, ffn_norm.reshape(1, -1), w_router.T)


def _row_copy(src_ref, src_row, dst_ref, dst_row, sem):
    return pltpu.make_async_copy(src_ref.at[pl.ds(src_row, 1), :], dst_ref.at[pl.ds(dst_row, 1), :], sem)


def _dispatch_kernel(plo_ref, pn_ref, dest_hbm, x_ref, out_hbm, idx_ref, zero_ref, idx_sem, row_sem, pad_sem):
    tm = ROUTE_TM
    load = pltpu.make_async_copy(dest_hbm.at[pl.program_id(0)], idx_ref, idx_sem)
    load.start()

    @pl.when(pl.program_id(0) == 0)
    def _():
        zero_ref[...] = jnp.zeros_like(zero_ref)
        for e in range(N_EXPERTS):
            def fill(r, carry, e=e):
                _row_copy(zero_ref, 0, out_hbm, plo_ref[e] + r, pad_sem).start()
                return carry
            lax.fori_loop(0, pn_ref[e], fill, 0)
        for e in range(N_EXPERTS):
            def done(r, carry, e=e):
                _row_copy(zero_ref, 0, out_hbm, plo_ref[e] + r, pad_sem).wait()
                return carry
            lax.fori_loop(0, pn_ref[e], done, 0)
        first_free = (plo_ref[N_EXPERTS - 1] + pn_ref[N_EXPERTS - 1]) // MOE_TM

        def block_copy(b):
            return pltpu.make_async_copy(zero_ref, out_hbm.at[pl.ds(pl.multiple_of(b * MOE_TM, MOE_TM), MOE_TM), :],
                                         pad_sem)

        def fill_block(b, carry):
            block_copy(b).start()
            return carry

        def done_block(b, carry):
            block_copy(b).wait()
            return carry

        lax.fori_loop(first_free, out_hbm.shape[0] // MOE_TM, fill_block, 0)
        lax.fori_loop(first_free, out_hbm.shape[0] // MOE_TM, done_block, 0)

    load.wait()
    for t in range(tm):
        for k in range(TOP_K):
            _row_copy(x_ref, t, out_hbm, idx_ref[k * tm + t], row_sem).start()
    for t in range(tm):
        for k in range(TOP_K):
            _row_copy(x_ref, t, out_hbm, idx_ref[k * tm + t], row_sem).wait()


def _moe_dispatch(x2, dest2, pad_lo, pad_n, n_rows):
    t = x2.shape[0]
    tm = ROUTE_TM
    grid_spec = pltpu.PrefetchScalarGridSpec(
        num_scalar_prefetch=2,
        grid=(t // tm,),
        in_specs=[pl.BlockSpec(memory_space=pl.ANY),
                  pl.BlockSpec((tm, D_MODEL), lambda i, lo, n: (i, 0))],
        out_specs=pl.BlockSpec(memory_space=pl.ANY),
        scratch_shapes=[pltpu.SMEM((TOP_K * tm,), jnp.int32), pltpu.VMEM((MOE_TM, D_MODEL), F32),
                        pltpu.SemaphoreType.DMA, pltpu.SemaphoreType.DMA, pltpu.SemaphoreType.DMA],
    )
    return pl.pallas_call(
        _dispatch_kernel,
        grid_spec=grid_spec,
        out_shape=jax.ShapeDtypeStruct((n_rows, D_MODEL), F32),
        compiler_params=_cparams(("arbitrary",), 32),
        name="moe_dispatch",
    )(pad_lo, pad_n, dest2, x2)


def _combine_kernel(dest_hbm, x_ref, gc_ref, fn_ref, y_hbm, o_ref, idx0_ref, idx1_ref, ybuf_ref, idx_sem, row_sem):
    tm = ROUTE_TM
    i, n = pl.program_id(0), pl.num_programs(0)
    idx_refs = (idx0_ref, idx1_ref)

    def gather(idx_ref, slot, t, k):
        return _row_copy(y_hbm, idx_ref[k * tm + t], ybuf_ref.at[slot, k], t, row_sem.at[slot])

    @pl.when(i == 0)
    def _():
        first = pltpu.make_async_copy(dest_hbm.at[0], idx0_ref, idx_sem.at[0])
        first.start()
        first.wait()

        def issue(t, carry):
            for k in range(TOP_K):
                gather(idx0_ref, 0, t, k).start()
            return carry

        lax.fori_loop(0, tm, issue, 0)

    def step(slot):
        nxt = 1 - slot
        load = pltpu.make_async_copy(dest_hbm.at[jnp.minimum(i + 1, n - 1)], idx_refs[nxt], idx_sem.at[nxt])
        load.start()
        for t in range(tm):
            for k in range(TOP_K):
                gather(idx_refs[slot], slot, t, k).wait()
        load.wait()
        for t in range(tm):
            for k in range(TOP_K):
                gather(idx_refs[nxt], nxt, t, k).start()
        gc = gc_ref[...]
        x = x_ref[...] + gc[:, 0:1] * ybuf_ref[slot, 0] + gc[:, 1:2] * ybuf_ref[slot, 1]
        o_ref[...] = _rms(x, fn_ref[...])

        @pl.when(i == n - 1)
        def _():
            for t in range(tm):
                for k in range(TOP_K):
                    gather(idx_refs[nxt], nxt, t, k).wait()

    for slot in range(2):
        pl.when(i % 2 == slot)(functools.partial(step, slot))


def _moe_combine_norm(x2, y_buf, dest2, gcol, final_norm):
    t = x2.shape[0]
    tm = ROUTE_TM
    return pl.pallas_call(
        _combine_kernel,
        grid=(t // tm,),
        in_specs=[pl.BlockSpec(memory_space=pl.ANY),
                  pl.BlockSpec((tm, D_MODEL), lambda i: (i, 0)),
                  pl.BlockSpec((tm, LANES), lambda i: (i, 0)),
                  pl.BlockSpec((1, D_MODEL), lambda i: (0, 0)),
                  pl.BlockSpec(memory_space=pl.ANY)],
        out_specs=pl.BlockSpec((tm, D_MODEL), lambda i: (i, 0)),
        out_shape=jax.ShapeDtypeStruct((t, D_MODEL), F32),
        scratch_shapes=[pltpu.SMEM((TOP_K * tm,), jnp.int32), pltpu.SMEM((TOP_K * tm,), jnp.int32),
                        pltpu.VMEM((2, TOP_K, tm, D_MODEL), F32),
                        pltpu.SemaphoreType.DMA((2,)), pltpu.SemaphoreType.DMA((2,))],
        compiler_params=_cparams(("arbitrary",), 40),
        name="moe_combine_norm",
    )(dest2, x2, gcol, final_norm.reshape(1, -1), y_buf)


def _moe_layer(x2, ffn_norm, w_router, w_gate, w_up, w_down, final_norm):
    t = x2.shape[0]
    tm = ROUTE_TM
    e01, r01, gcol, cnt = _moe_route(x2, ffn_norm, w_router)
    counts = cnt[:, 0].astype(jnp.int32)
    padded = (counts + MOE_TM - 1) // MOE_TM * MOE_TM
    pad_ends = jnp.cumsum(padded)
    pad_starts = pad_ends - padded
    n_blocks = -(-t * TOP_K // MOE_TM) + N_EXPERTS
    blk_start = jnp.arange(n_blocks, dtype=jnp.int32) * MOE_TM
    block_e = jnp.minimum(jnp.sum(blk_start[:, None] >= pad_ends[None, :], axis=1), N_EXPERTS - 1).astype(jnp.int32)
    n_used = (pad_ends[-1] // MOE_TM).astype(jnp.int32).reshape(1)
    dest = r01 + jnp.sum(jnp.where(e01[None] == jnp.arange(N_EXPERTS)[:, None, None], pad_starts[:, None, None], 0),
                         axis=0)
    dest2 = dest.reshape(TOP_K, t // tm, tm).transpose(1, 0, 2).reshape(t // tm, TOP_K * tm)
    xs = _moe_dispatch(x2, dest2, pad_starts + counts, padded - counts, n_blocks * MOE_TM)
    y_buf = _moe_experts(xs, ffn_norm, block_e, n_used, w_gate.astype(BF16), w_up.astype(BF16),
                         w_down.astype(BF16))
    return _moe_combine_norm(x2, y_buf, dest2, gcol, final_norm)


def _permute_w_in(w_in):
    s5 = w_in[:, 0:256]
    gm = w_in[:, 256:768]
    qkv = w_in[:, 768:1536]
    z = w_in[:, 1536:1792]
    a = w_in[:, 1792:1796]
    b = w_in[:, 1796:1800]
    sc = w_in[:, 1800:2568]
    ab = jnp.zeros((D_MODEL, LANES), F32).at[:, 0:GDN_HEADS].set(a).at[:, AB_B_LANE:AB_B_LANE + GDN_HEADS].set(b)
    return jnp.concatenate([gm, s5, z, qkv, sc, ab], axis=1).astype(BF16)


def kernel(x, mix_norm, w_in, s5_lam_re, s5_lam_im, s5_log_step, s5_b_re, s5_b_im, s5_c_re, s5_c_im, s5_d, s5_w_glu, s5_b_glu, s5_out_norm, sgu_ln_g, sgu_ln_b, sgu_w, sgu_b, gmlp_out_norm, gdn_conv, gdn_a_log, gdn_dt_bias, gdn_norm, sc_conv, sc_out_norm, w_out, ffn_norm, ffn_w_gate, ffn_w_up, ffn_w_down, moe_router, moe_w_gate, moe_w_up, moe_w_down, final_norm):
    bsz, seqlen, d = x.shape
    t = bsz * seqlen
    x2 = x.reshape(t, d)
    out = None
    for l in range(DEPTH):
        p2 = _in_proj(x2, mix_norm[l].reshape(1, -1), _permute_w_in(w_in[l]))
        p3 = p2.reshape(bsz, seqlen, P_COLS)
        wb, wc, pw = _s5_params(s5_lam_re[l], s5_lam_im[l], s5_log_step[l], s5_b_re[l], s5_b_im[l],
                                s5_c_re[l], s5_c_im[l])
        y_s5 = _s5_mixer(p3, wb, wc, pw, s5_d[l], s5_w_glu[l], s5_b_glu[l], s5_out_norm[l])
        y_gm = _gmlp_mixer(p3, sgu_ln_g[l], sgu_ln_b[l], sgu_w[l], sgu_b[l], gmlp_out_norm[l])
        y_gdn = _gdn_mixer(p3, gdn_conv[l], gdn_a_log[l], gdn_dt_bias[l], gdn_norm[l])
        y_sc = _shortconv_mixer(p3, sc_conv[l], sc_out_norm[l])
        ys = [y.reshape(t, BRANCH_W) for y in (y_s5, y_gm, y_gdn, y_sc)]
        x2 = _out_proj(x2, ys, w_out[l])
        i = l // 2
        if l % 2 == 0:
            x2 = _ffn_dense(x2, ffn_norm[l], ffn_w_gate[i], ffn_w_up[i], ffn_w_down[i])
        else:
            out = _moe_layer(x2, ffn_norm[l], moe_router[i], moe_w_gate[i], moe_w_up[i], moe_w_down[i],
                             final_norm)
    return out.reshape(bsz, seqlen, d)
```

```python
import functools
import math

import jax
import jax.numpy as jnp
from jax import lax
from jax.experimental import pallas as pl
from jax.experimental.pallas import tpu as pltpu

F32 = jnp.float32
BF16 = jnp.bfloat16
HIGHEST = lax.Precision.HIGHEST

D_MODEL = 1024
DEPTH = 2
BRANCH_W = 256
S5_GROUP = 16
S5_GROUPS = 16
S5_STATE = 64
S5_NSTATE = S5_GROUPS * S5_STATE
GMLP_HEADS = 4
GMLP_HEAD_DIM = 64
GMLP_CHUNK = 128
GDN_HEAD_DIM = 64
GDN_HEADS = 4
GDN_CONV = 4
GDN_CHUNK = 64
SC_CONV = 3
D_FF = 2816
N_EXPERTS = 8
TOP_K = 2
D_FF_EXPERT = 3584
EPS = 1e-6

LANES = 128
SUBLANES = 8
VMEM_BYTES_V7X = 64 * 1024 * 1024

COL_GM = 0
COL_S5 = 512
COL_Z = 768
COL_Q = 1024
COL_K = 1280
COL_V = 1536
COL_SCB = 1792
COL_SCC = 2048
COL_SCX = 2304
COL_AB = 2560
P_COLS = 2688
AB_B_LANE = 64

IN_TM = 512
S5_CHUNK = 128
GM_TILE = 512
SC_TILE = 512
GDN_TILE = 128
GDN_BATCH = 4
OUT_TM = 512
FFN_TM = 512
FFN_TF = 256
MOE_TM = 512
MOE_TF = 512
ROUTE_TM = 512


def _cparams(sem, vmem_mb):
    return pltpu.CompilerParams(dimension_semantics=sem, vmem_limit_bytes=vmem_mb * 1024 * 1024)


def _rms(x, g):
    return x * lax.rsqrt(jnp.mean(x * x, axis=-1, keepdims=True) + EPS) * g


def _silu(x):
    return x * jax.nn.sigmoid(x)


def _bdot(a, b):
    return jnp.dot(a.astype(BF16), b.astype(BF16), preferred_element_type=F32)


def _split_dot(a, b, passes, data_on_left):
    data = a if data_on_left else b
    acc = None
    for _ in range(passes):
        piece = data.astype(BF16)
        term = (jnp.dot(piece, b, preferred_element_type=F32) if data_on_left
                else jnp.dot(a, piece, preferred_element_type=F32))
        acc = term if acc is None else acc + term
        data = data - piece.astype(F32)
    return acc


def _bdot_nt(a, b):
    return lax.dot_general(a.astype(BF16), b.astype(BF16), (((1,), (1,)), ((), ())),
                           preferred_element_type=F32)


def _in_proj_kernel(x_ref, g_ref, w_ref, o_ref):
    h = _rms(x_ref[...], g_ref[...])
    o_ref[...] = _bdot(h, w_ref[...])


def _in_proj(x2, g, w):
    t = x2.shape[0]
    return pl.pallas_call(
        _in_proj_kernel,
        grid=(t // IN_TM,),
        in_specs=[pl.BlockSpec((IN_TM, D_MODEL), lambda i: (i, 0)),
                  pl.BlockSpec((1, D_MODEL), lambda i: (0, 0)),
                  pl.BlockSpec((D_MODEL, P_COLS), lambda i: (0, 0))],
        out_specs=pl.BlockSpec((IN_TM, P_COLS), lambda i: (i, 0)),
        out_shape=jax.ShapeDtypeStruct((t, P_COLS), F32),
        compiler_params=_cparams(("parallel",), 48),
        name="in_proj",
    )(x2, g, w)


def _s5_kernel(u_ref, wb_ref, pw_ref, wc_ref, d_ref, wglu_ref, bglu_ref, on_ref, o_ref, sr_ref, si_ref):
    n = S5_NSTATE

    @pl.when(pl.program_id(1) == 0)
    def _():
        sr_ref[...] = jnp.zeros_like(sr_ref)
        si_ref[...] = jnp.zeros_like(si_ref)

    u = u_ref[...]
    bu = _bdot(u, wb_ref[...])
    n_steps = int(math.log2(SUBLANES))
    cr, ci = sr_ref[...], si_ref[...]
    grs, gis = [], []
    for j in range(S5_CHUNK // SUBLANES):
        rows = slice(j * SUBLANES, (j + 1) * SUBLANES)
        gr, gi = bu[rows, :n], bu[rows, n:]
        for s in range(n_steps):
            tr, ti = pw_ref[s, :, :n], pw_ref[s, :, n:]
            rr, ri = pltpu.roll(gr, 1 << s, 0), pltpu.roll(gi, 1 << s, 0)
            gr, gi = gr + tr * rr - ti * ri, gi + tr * ri + ti * rr
        pr, pi = pw_ref[n_steps, :, :n], pw_ref[n_steps, :, n:]
        br, bi = jnp.broadcast_to(cr, (SUBLANES, n)), jnp.broadcast_to(ci, (SUBLANES, n))
        gr, gi = gr + pr * br - pi * bi, gi + pr * bi + pi * br
        cr, ci = gr[SUBLANES - 1:, :], gi[SUBLANES - 1:, :]
        grs.append(gr)
        gis.append(gi)
    sr_ref[...] = cr
    si_ref[...] = ci
    xr, xi = jnp.concatenate(grs, axis=0), jnp.concatenate(gis, axis=0)
    y = _bdot(xr, wc_ref[:n, :]) + _bdot(xi, wc_ref[n:, :])
    y = y + d_ref[...] * u
    y = jax.nn.gelu(y)
    y = y * jax.nn.sigmoid(_bdot(y, wglu_ref[...]) + bglu_ref[...])
    o_ref[...] = _rms(y, on_ref[...])


def _s5_params(lam_re, lam_im, log_step, b_re, b_im, c_re, c_im):
    g, n, p = S5_GROUPS, S5_STATE, S5_GROUP
    dt = jnp.exp(log_step)[:, None]
    mag = jnp.exp(lam_re * dt)
    ar, ai = mag * jnp.cos(lam_im * dt), mag * jnp.sin(lam_im * dt)
    den = lam_re * lam_re + lam_im * lam_im
    fr = ((ar - 1.0) * lam_re + ai * lam_im) / den
    fi = (ai * lam_re - (ar - 1.0) * lam_im) / den
    bbr = fr[..., None] * b_re - fi[..., None] * b_im
    bbi = fr[..., None] * b_im + fi[..., None] * b_re
    eye = jnp.eye(g, dtype=F32)
    wbr = jnp.einsum('gnp,gh->gphn', bbr, eye).reshape(g * p, g * n)
    wbi = jnp.einsum('gnp,gh->gphn', bbi, eye).reshape(g * p, g * n)
    wb = jnp.concatenate([wbr, wbi], axis=1)
    wcr = jnp.einsum('gpn,gh->gnhp', c_re, eye).reshape(g * n, g * p)
    wci = jnp.einsum('gpn,gh->gnhp', -c_im, eye).reshape(g * n, g * p)
    wc = jnp.concatenate([wcr, wci], axis=0)
    a1r, a1i = ar.reshape(1, g * n), ai.reshape(1, g * n)
    r_idx = jnp.arange(SUBLANES)[:, None]
    tables = []
    pr, pi = a1r, a1i
    for s in range(int(math.log2(SUBLANES))):
        keep = r_idx >= (1 << s)
        tables.append(jnp.concatenate([jnp.where(keep, pr, 0.0), jnp.where(keep, pi, 0.0)], axis=1))
        pr, pi = pr * pr - pi * pi, 2.0 * pr * pi
    rows_r, rows_i = [a1r], [a1i]
    for _ in range(SUBLANES - 1):
        qr, qi = rows_r[-1], rows_i[-1]
        rows_r.append(qr * a1r - qi * a1i)
        rows_i.append(qr * a1i + qi * a1r)
    tables.append(jnp.concatenate([jnp.concatenate(rows_r, axis=0), jnp.concatenate(rows_i, axis=0)], axis=1))
    pw = jnp.stack(tables, axis=0)
    return wb.astype(BF16), wc.astype(BF16), pw


def _s5_mixer(p3, wb, wc, pw, d_skip, w_glu, b_glu, out_norm):
    b, l, _ = p3.shape
    n2 = 2 * S5_NSTATE
    const = lambda shape: pl.BlockSpec(shape, lambda i, j: (0,) * len(shape))
    return pl.pallas_call(
        _s5_kernel,
        grid=(b, l // S5_CHUNK),
        in_specs=[pl.BlockSpec((None, S5_CHUNK, BRANCH_W), lambda i, j: (i, j, COL_S5 // BRANCH_W)),
                  const((BRANCH_W, n2)), const((int(math.log2(SUBLANES)) + 1, SUBLANES, n2)), const((n2, BRANCH_W)),
                  const((1, BRANCH_W)), const((BRANCH_W, BRANCH_W)), const((1, BRANCH_W)),
                  const((1, BRANCH_W))],
        out_specs=pl.BlockSpec((None, S5_CHUNK, BRANCH_W), lambda i, j: (i, j, 0)),
        out_shape=jax.ShapeDtypeStruct((b, l, BRANCH_W), F32),
        scratch_shapes=[pltpu.VMEM((1, S5_NSTATE), F32), pltpu.VMEM((1, S5_NSTATE), F32)],
        compiler_params=_cparams(("parallel", "arbitrary"), 40),
        name="s5_mixer",
    )(p3, wb, pw, wc, d_skip.reshape(1, -1), w_glu.astype(BF16), b_glu.reshape(1, -1),
      out_norm.reshape(1, -1))


def _gmlp_kernel(p_ref, lng_ref, lnb_ref, w_ref, bias_ref, on_ref, o_ref):
    z = jax.nn.gelu(p_ref[...])
    u, v = z[:, :BRANCH_W], z[:, BRANCH_W:]
    vc = v - jnp.mean(v, axis=-1, keepdims=True)
    v = vc * lax.rsqrt(jnp.mean(vc * vc, axis=-1, keepdims=True) + EPS) * lng_ref[...] + lnb_ref[...]
    ti = lax.broadcasted_iota(jnp.int32, (GMLP_CHUNK, GMLP_CHUNK), 0)
    si = lax.broadcasted_iota(jnp.int32, (GMLP_CHUNK, GMLP_CHUNK), 1)
    tril = ti >= si
    ws = [jnp.where(tril, w_ref[h], 0.0).astype(BF16) for h in range(GMLP_HEADS)]
    lane = lax.broadcasted_iota(jnp.int32, (GMLP_CHUNK, BRANCH_W), 1)
    bias = bias_ref[...]
    outs = []
    for c in range(GM_TILE // GMLP_CHUNK):
        vb = v[c * GMLP_CHUNK:(c + 1) * GMLP_CHUNK, :].astype(BF16)
        s = jnp.dot(ws[GMLP_HEADS - 1], vb, preferred_element_type=F32)
        for h in range(GMLP_HEADS - 2, -1, -1):
            sh = jnp.dot(ws[h], vb, preferred_element_type=F32)
            s = jnp.where(lane < (h + 1) * GMLP_HEAD_DIM, sh, s)
        outs.append(s + bias)
    s = jnp.concatenate(outs, axis=0)
    o_ref[...] = _rms(u * s, on_ref[...])


def _gmlp_mixer(p3, ln_g, ln_b, w_sp, b_sp, out_norm):
    b, l, _ = p3.shape
    bias = jnp.repeat(b_sp.T, GMLP_HEAD_DIM, axis=1)
    const = lambda shape: pl.BlockSpec(shape, lambda i, j: (0,) * len(shape))
    return pl.pallas_call(
        _gmlp_kernel,
        grid=(b, l // GM_TILE),
        in_specs=[pl.BlockSpec((None, GM_TILE, 2 * BRANCH_W), lambda i, j: (i, j, COL_GM // (2 * BRANCH_W))),
                  const((1, BRANCH_W)), const((1, BRANCH_W)),
                  const((GMLP_HEADS, GMLP_CHUNK, GMLP_CHUNK)), const((GMLP_CHUNK, BRANCH_W)),
                  const((1, BRANCH_W))],
        out_specs=pl.BlockSpec((None, GM_TILE, BRANCH_W), lambda i, j: (i, j, 0)),
        out_shape=jax.ShapeDtypeStruct((b, l, BRANCH_W), F32),
        compiler_params=_cparams(("parallel", "parallel"), 32),
        name="gmlp_mixer",
    )(p3, ln_g.reshape(1, -1), ln_b.reshape(1, -1), w_sp, bias, out_norm.reshape(1, -1))


def _shortconv_kernel(b_ref, c_ref, x_ref, w_ref, on_ref, o_ref, halo_ref):
    @pl.when(pl.program_id(1) == 0)
    def _():
        halo_ref[...] = jnp.zeros_like(halo_ref)

    cx = c_ref[...] * x_ref[...]
    ext = jnp.concatenate([halo_ref[...], cx], axis=0)
    halo_ref[...] = cx[SC_TILE - SUBLANES:, :]
    y = w_ref[SC_CONV - 1:SC_CONV, :] * cx
    for j in range(SC_CONV - 1):
        sh = SC_CONV - 1 - j
        y = y + w_ref[j:j + 1, :] * pltpu.roll(ext, sh, 0)[SUBLANES:, :]
    o_ref[...] = _rms(b_ref[...] * y, on_ref[...])


def _shortconv_mixer(p3, conv_w, out_norm):
    b, l, _ = p3.shape
    w = jnp.concatenate([conv_w, jnp.zeros((SUBLANES - SC_CONV, BRANCH_W), F32)], axis=0)
    col = lambda c: pl.BlockSpec((None, SC_TILE, BRANCH_W), lambda i, j: (i, j, c // BRANCH_W))
    const = lambda shape: pl.BlockSpec(shape, lambda i, j: (0,) * len(shape))
    return pl.pallas_call(
        _shortconv_kernel,
        grid=(b, l // SC_TILE),
        in_specs=[col(COL_SCB), col(COL_SCC), col(COL_SCX), const((SUBLANES, BRANCH_W)), const((1, BRANCH_W))],
        out_specs=pl.BlockSpec((None, SC_TILE, BRANCH_W), lambda i, j: (i, j, 0)),
        out_shape=jax.ShapeDtypeStruct((b, l, BRANCH_W), F32),
        scratch_shapes=[pltpu.VMEM((SUBLANES, BRANCH_W), F32)],
        compiler_params=_cparams(("parallel", "arbitrary"), 32),
        name="shortconv_mixer",
    )(p3, p3, p3, w, out_norm.reshape(1, -1))


def _gdn_kernel(q_ref, k_ref, v_ref, z_ref, ab_ref, cw_ref, alog_ref, dtb_ref, ng_ref, o_ref,
                halo_ref, state_ref):
    c_len, h_dim, nh, w = GDN_CHUNK, GDN_HEAD_DIM, GDN_HEADS, BRANCH_W
    n_chunks = GDN_TILE // c_len

    @pl.when(pl.program_id(1) == 0)
    def _():
        halo_ref[...] = jnp.zeros_like(halo_ref)
        state_ref[...] = jnp.zeros_like(state_ref)

    shift = int(math.log2(h_dim))
    r256 = lax.broadcasted_iota(jnp.int32, (w, w), 0)
    c256 = lax.broadcasted_iota(jnp.int32, (w, w), 1)
    same_head = (r256 >> shift) == (c256 >> shift)
    ebd = jnp.where(same_head, 1.0, 0.0).astype(BF16)
    causal = same_head & (r256 >= c256)
    strict = same_head & (r256 > c256)
    eye = jnp.where(r256 == c256, 1.0, 0.0)
    r128 = lax.broadcasted_iota(jnp.int32, (LANES, w), 0)
    c128 = lax.broadcasted_iota(jnp.int32, (LANES, w), 1)
    exp_a = jnp.where(r128 == (c128 >> shift), 1.0, 0.0).astype(BF16)
    exp_b = jnp.where(r128 == (c128 >> shift) + AB_B_LANE, 1.0, 0.0).astype(BF16)

    def stack(a):
        lane_head = (lax.broadcasted_iota(jnp.int32, a.shape, 1) >> shift) & (nh - 1)
        return jnp.concatenate([jnp.where(lane_head == h, a, 0.0) for h in range(nh)], axis=0)

    def unstack(a):
        return a[0:c_len] + a[c_len:2 * c_len] + a[2 * c_len:3 * c_len] + a[3 * c_len:4 * c_len]

    def conv_silu(ref, bb, col):
        cur = ref[bb]
        ext = jnp.concatenate([halo_ref[bb, :, col * w:(col + 1) * w], cur], axis=0)
        halo_ref[bb, :, col * w:(col + 1) * w] = cur[GDN_TILE - SUBLANES:, :]
        y = cw_ref[GDN_CONV - 1:GDN_CONV, col * w:(col + 1) * w] * cur
        for j in range(GDN_CONV - 1):
            sh = GDN_CONV - 1 - j
            y = y + cw_ref[j:j + 1, col * w:(col + 1) * w] * pltpu.roll(ext, sh, 0)[SUBLANES:, :]
        return _silu(y)

    rt = lax.broadcasted_iota(jnp.int32, (GDN_TILE, GDN_TILE), 0)
    ct = lax.broadcasted_iota(jnp.int32, (GDN_TILE, GDN_TILE), 1)
    ltri = jnp.where(((rt >> shift) == (ct >> shift)) & (rt >= ct), 1.0, 0.0).astype(BF16)
    tile4 = lambda a: jnp.concatenate([a] * nh, axis=0)

    q_decs, k_decs, g_tots, rhss, a_mats, qkds = [], [], [], [], [], []
    for bb in range(GDN_BATCH):
        q = conv_silu(q_ref, bb, 0)
        k = conv_silu(k_ref, bb, 1)
        v = conv_silu(v_ref, bb, 2)
        q = q * lax.rsqrt(jnp.dot((q * q).astype(BF16), ebd, preferred_element_type=F32) + EPS) * (h_dim ** -0.5)
        k = k * lax.rsqrt(jnp.dot((k * k).astype(BF16), ebd, preferred_element_type=F32) + EPS)
        ab = ab_ref[bb]
        beta = jax.nn.sigmoid(ab)
        xa = ab + dtb_ref[...]
        softplus = jnp.maximum(xa, 0.0) + jnp.log(1.0 + jnp.exp(-jnp.abs(xa)))
        g = -jnp.exp(alog_ref[...]) * softplus
        gcum = _split_dot(ltri, g, 3, data_on_left=False)
        gexp = _split_dot(gcum, exp_a, 2, data_on_left=True)
        bexp = _split_dot(beta, exp_b, 2, data_on_left=True)
        for c in range(n_chunks):
            sl = slice(c * c_len, (c + 1) * c_len)
            qc, kc, vc, gc, bc = q[sl], k[sl], v[sl], gexp[sl], bexp[sl]
            glast = gc[c_len - 1:c_len, :]
            eg = jnp.exp(gc)
            q_decs.append(qc * eg)
            k_decs.append(kc * jnp.exp(glast - gc))
            g_tots.append(jnp.exp(glast))
            kst = stack(kc).astype(BF16)
            kk = _bdot_nt(kst, kst)
            qk = _bdot_nt(stack(qc), kst)
            gcol = stack(gc)
            bcol = stack(bc)
            dec = jnp.exp(jnp.where(causal, gcol - gcol.T, -jnp.inf))
            a_mats.append(jnp.where(strict, bcol * kk * dec, 0.0))
            qkds.append((qk * dec).astype(BF16))
            rhss.append(jnp.concatenate([tile4(vc) * bcol, tile4(kc * eg) * bcol], axis=1).astype(BF16))
    t_invs = [eye - a for a in a_mats]
    pws = a_mats
    for _ in range(int(math.log2(c_len)) - 1):
        pws = [_bdot(p, p) for p in pws]
        t_invs = [t + _bdot(t, p) for t, p in zip(t_invs, pws)]
    sols = [unstack(_bdot(t, r)) for t, r in zip(t_invs, rhss)]
    states = [state_ref[bb] for bb in range(GDN_BATCH)]
    outs = [[] for _ in range(GDN_BATCH)]
    for c in range(n_chunks):
        for bb in range(GDN_BATCH):
            i = bb * n_chunks + c
            w_v, w_k = sols[i][:, :w], sols[i][:, w:]
            v_new = w_v - _bdot(w_k, states[bb])
            o = _bdot(q_decs[i], states[bb]) + unstack(_bdot(qkds[i], stack(v_new)))
            states[bb] = states[bb] * g_tots[i] + jnp.where(same_head, _bdot(k_decs[i].T, v_new), 0.0)
            outs[bb].append(o)
    for bb in range(GDN_BATCH):
        state_ref[bb] = states[bb]
        o = jnp.concatenate(outs[bb], axis=0)
        ms = jnp.dot((o * o).astype(BF16), ebd, preferred_element_type=F32) * (1.0 / h_dim)
        o = o * lax.rsqrt(ms + EPS) * ng_ref[...]
        o_ref[bb] = o * _silu(z_ref[bb])


def _gdn_mixer(p3, conv_w, a_log, dt_bias, norm_g):
    b, l, _ = p3.shape
    cw = jnp.concatenate([conv_w, jnp.zeros((SUBLANES - GDN_CONV, 3 * BRANCH_W), F32)], axis=0)
    lane_row = lambda vec: jnp.zeros((1, LANES), F32).at[0, :GDN_HEADS].set(vec)
    col = lambda c: pl.BlockSpec((GDN_BATCH, GDN_TILE, BRANCH_W), lambda i, j: (i, j, c // BRANCH_W))
    const = lambda shape: pl.BlockSpec(shape, lambda i, j: (0,) * len(shape))
    return pl.pallas_call(
        _gdn_kernel,
        grid=(b // GDN_BATCH, l // GDN_TILE),
        in_specs=[col(COL_Q), col(COL_K), col(COL_V), col(COL_Z),
                  pl.BlockSpec((GDN_BATCH, GDN_TILE, LANES), lambda i, j: (i, j, COL_AB // LANES)),
                  const((SUBLANES, 3 * BRANCH_W)), const((1, LANES)), const((1, LANES)), const((1, BRANCH_W))],
        out_specs=pl.BlockSpec((GDN_BATCH, GDN_TILE, BRANCH_W), lambda i, j: (i, j, 0)),
        out_shape=jax.ShapeDtypeStruct((b, l, BRANCH_W), F32),
        scratch_shapes=[pltpu.VMEM((GDN_BATCH, SUBLANES, 3 * BRANCH_W), F32),
                        pltpu.VMEM((GDN_BATCH, BRANCH_W, BRANCH_W), F32)],
        compiler_params=_cparams(("parallel", "arbitrary"), 48),
        name="gdn_mixer",
    )(p3, p3, p3, p3, p3, cw, lane_row(a_log), lane_row(dt_bias), jnp.tile(norm_g, GDN_HEADS).reshape(1, -1))


def _out_proj_kernel(x_ref, y0_ref, y1_ref, y2_ref, y3_ref, w_ref, o_ref):
    acc = x_ref[...]
    for i, y_ref in enumerate((y0_ref, y1_ref, y2_ref, y3_ref)):
        acc = acc + _bdot(y_ref[...], w_ref[i * BRANCH_W:(i + 1) * BRANCH_W, :])
    o_ref[...] = acc


def _out_proj(x2, ys, w_out):
    t = x2.shape[0]
    row = lambda n: pl.BlockSpec((OUT_TM, n), lambda i: (i, 0))
    return pl.pallas_call(
        _out_proj_kernel,
        grid=(t // OUT_TM,),
        in_specs=[row(D_MODEL)] + [row(BRANCH_W)] * 4 + [pl.BlockSpec((D_MODEL, D_MODEL), lambda i: (0, 0))],
        out_specs=row(D_MODEL),
        out_shape=jax.ShapeDtypeStruct((t, D_MODEL), F32),
        compiler_params=_cparams(("parallel",), 32),
        name="out_proj",
    )(x2, *ys, w_out.astype(BF16))


def _ffn_kernel(x_ref, g_ref, wg_ref, wu_ref, wd_ref, o_ref):
    x = x_ref[...]
    hb = _rms(x, g_ref[...]).astype(BF16)
    acc = x
    for f in range(D_FF // FFN_TF):
        cols = slice(f * FFN_TF, (f + 1) * FFN_TF)
        a = jnp.dot(hb, wg_ref[:, cols], preferred_element_type=F32)
        u = jnp.dot(hb, wu_ref[:, cols], preferred_element_type=F32)
        acc = acc + jnp.dot((_silu(a) * u).astype(BF16), wd_ref[cols, :], preferred_element_type=F32)
    o_ref[...] = acc


def _ffn_dense(x2, g, w_gate, w_up, w_down):
    t = x2.shape[0]
    resident = lambda shape: pl.BlockSpec(shape, lambda i: (0, 0), pipeline_mode=pl.Buffered(1))
    return pl.pallas_call(
        _ffn_kernel,
        grid=(t // FFN_TM,),
        in_specs=[pl.BlockSpec((FFN_TM, D_MODEL), lambda i: (i, 0)),
                  pl.BlockSpec((1, D_MODEL), lambda i: (0, 0)),
                  resident((D_MODEL, D_FF)), resident((D_MODEL, D_FF)), resident((D_FF, D_MODEL))],
        out_specs=pl.BlockSpec((FFN_TM, D_MODEL), lambda i: (i, 0)),
        out_shape=jax.ShapeDtypeStruct((t, D_MODEL), F32),
        compiler_params=_cparams(("parallel",), 48),
        name="ffn_dense",
    )(x2, g.reshape(1, -1), w_gate.astype(BF16), w_up.astype(BF16), w_down.astype(BF16))


def _moe_kernel(be_ref, nu_ref, x_ref, g_ref, wg_ref, wu_ref, wd_ref, o_ref):
    del be_ref

    @pl.when(pl.program_id(0) < nu_ref[0])
    def _():
        xb = _rms(x_ref[...], g_ref[...]).astype(BF16)
        acc = None
        for f in range(D_FF_EXPERT // MOE_TF):
            cols = slice(f * MOE_TF, (f + 1) * MOE_TF)
            a = jnp.dot(xb, wg_ref[:, cols], preferred_element_type=F32)
            u = jnp.dot(xb, wu_ref[:, cols], preferred_element_type=F32)
            y = jnp.dot((_silu(a) * u).astype(BF16), wd_ref[cols, :], preferred_element_type=F32)
            acc = y if acc is None else acc + y
        o_ref[...] = acc

    @pl.when(pl.program_id(0) >= nu_ref[0])
    def _():
        o_ref[...] = jnp.zeros_like(o_ref)


def _moe_experts(buf, g, block_e, n_used, wg, wu, wd):
    rows = buf.shape[0]
    expert = lambda shape: pl.BlockSpec((None,) + shape, lambda m, be, nu: (be[m], 0, 0),
                                        pipeline_mode=pl.Buffered(1))
    grid_spec = pltpu.PrefetchScalarGridSpec(
        num_scalar_prefetch=2,
        grid=(rows // MOE_TM,),
        in_specs=[pl.BlockSpec((MOE_TM, D_MODEL), lambda m, be, nu: (jnp.minimum(m, nu[0] - 1), 0)),
                  pl.BlockSpec((1, D_MODEL), lambda m, be, nu: (0, 0)),
                  expert((D_MODEL, D_FF_EXPERT)), expert((D_MODEL, D_FF_EXPERT)), expert((D_FF_EXPERT, D_MODEL))],
        out_specs=pl.BlockSpec((MOE_TM, D_MODEL), lambda m, be, nu: (m, 0)),
    )
    return pl.pallas_call(
        _moe_kernel,
        grid_spec=grid_spec,
        out_shape=jax.ShapeDtypeStruct((rows, D_MODEL), F32),
        compiler_params=_cparams(("arbitrary",), 56),
        name="moe_experts",
    )(block_e, n_used, buf, g.reshape(1, -1), wg, wu, wd)


def _route_kernel(x_ref, g_ref, wt_ref, e_ref, r_ref, gc_ref, cnt_ref, run_ref):
    tm = ROUTE_TM

    @pl.when(pl.program_id(0) == 0)
    def _():
        run_ref[...] = jnp.zeros_like(run_ref)

    h = _rms(x_ref[...], g_ref[...])
    lt = lax.dot_general(wt_ref[...], h, (((1,), (1,)), ((), ())), precision=HIGHEST,
                         preferred_element_type=F32)
    sub = lax.broadcasted_iota(jnp.int32, (N_EXPERTS, tm), 0)
    m1 = jnp.max(lt, axis=0, keepdims=True)
    i1 = jnp.min(jnp.where(lt == m1, sub, N_EXPERTS), axis=0, keepdims=True)
    lt2 = jnp.where(sub == i1, -jnp.inf, lt)
    m2 = jnp.max(lt2, axis=0, keepdims=True)
    i2 = jnp.min(jnp.where(lt2 == m2, sub, N_EXPERTS), axis=0, keepdims=True)
    oh0 = jnp.where(sub == i1, 1.0, 0.0)
    oh1 = jnp.where(sub == i2, 1.0, 0.0)
    cnt = oh0 + oh1
    ti = lax.broadcasted_iota(jnp.int32, (tm, tm), 0)
    tj = lax.broadcasted_iota(jnp.int32, (tm, tm), 1)
    upper = jnp.where(ti < tj, 1.0, 0.0).astype(BF16)
    pre = jnp.dot(cnt.astype(BF16), upper, preferred_element_type=F32) + run_ref[:, 0:1]
    r0 = jnp.sum(oh0 * pre, axis=0, keepdims=True)
    r1 = jnp.sum(oh1 * pre, axis=0, keepdims=True)
    e_ref[...] = jnp.concatenate([i1, i2], axis=0)
    r_ref[...] = jnp.concatenate([r0, r1], axis=0).astype(jnp.int32)
    run = run_ref[...] + jnp.sum(cnt, axis=1, keepdims=True)
    run_ref[...] = run
    cnt_ref[...] = run
    ex = jnp.exp(m2 - m1)
    g0 = 1.0 / (1.0 + ex)
    gates = jnp.where(sub == 0, g0, jnp.where(sub == 1, ex * g0, 0.0))
    er = lax.broadcasted_iota(jnp.int32, (N_EXPERTS, LANES), 0)
    ec = lax.broadcasted_iota(jnp.int32, (N_EXPERTS, LANES), 1)
    eye = jnp.where(er == ec, 1.0, 0.0).astype(BF16)
    acc = None
    for _ in range(3):
        piece = gates.astype(BF16)
        term = lax.dot_general(piece, eye, (((0,), (0,)), ((), ())), preferred_element_type=F32)
        acc = term if acc is None else acc + term
        gates = gates - piece.astype(F32)
    gc_ref[...] = acc


def _moe_route(x2, ffn_norm, w_router):
    t = x2.shape[0]
    tm = ROUTE_TM
    return pl.pallas_call(
        _route_kernel,
        grid=(t // tm,),
        in_specs=[pl.BlockSpec((tm, D_MODEL), lambda i: (i, 0)),
                  pl.BlockSpec((1, D_MODEL), lambda i: (0, 0)),
                  pl.BlockSpec((N_EXPERTS, D_MODEL), lambda i: (0, 0))],
        out_specs=[pl.BlockSpec((TOP_K, tm), lambda i: (0, i)),
                   pl.BlockSpec((TOP_K, tm), lambda i: (0, i)),
                   pl.BlockSpec((tm, LANES), lambda i: (i, 0)),
                   pl.BlockSpec((N_EXPERTS, LANES), lambda i: (0, 0))],
        out_shape=[jax.ShapeDtypeStruct((TOP_K, t), jnp.int32),
                   jax.ShapeDtypeStruct((TOP_K, t), jnp.int32),
                   jax.ShapeDtypeStruct((t, LANES), F32),
                   jax.ShapeDtypeStruct((N_EXPERTS, LANES), F32)],
        scratch_shapes=[pltpu.VMEM((N_EXPERTS, LANES), F32)],
        compiler_params=_cparams(("arbitrary",), 32),
        name="moe_route",
    )(x2, ffn_norm.reshape(1, -1), w_router.T)


def _row_copy(src_ref, src_row, dst_ref, dst_row, sem):
    return pltpu.make_async_copy(src_ref.at[pl.ds(src_row, 1), :], dst_ref.at[pl.ds(dst_row, 1), :], sem)


def _dispatch_kernel(plo_ref, pn_ref, dest_hbm, x_ref, out_hbm, idx_ref, zero_ref, idx_sem, row_sem, pad_sem):
    tm = ROUTE_TM
    load = pltpu.make_async_copy(dest_hbm.at[pl.program_id(0)], idx_ref, idx_sem)
    load.start()

    @pl.when(pl.program_id(0) == 0)
    def _():
        zero_ref[...] = jnp.zeros_like(zero_ref)
        for e in range(N_EXPERTS):
            def fill(r, carry, e=e):
                _row_copy(zero_ref, 0, out_hbm, plo_ref[e] + r, pad_sem).start()
                return carry
            lax.fori_loop(0, pn_ref[e], fill, 0)
        for e in range(N_EXPERTS):
            def done(r, carry, e=e):
                _row_copy(zero_ref, 0, out_hbm, plo_ref[e] + r, pad_sem).wait()
                return carry
            lax.fori_loop(0, pn_ref[e], done, 0)
        first_free = (plo_ref[N_EXPERTS - 1] + pn_ref[N_EXPERTS - 1]) // MOE_TM

        def block_copy(b):
            return pltpu.make_async_copy(zero_ref, out_hbm.at[pl.ds(pl.multiple_of(b * MOE_TM, MOE_TM), MOE_TM), :],
                                         pad_sem)

        def fill_block(b, carry):
            block_copy(b).start()
            return carry

        def done_block(b, carry):
            block_copy(b).wait()
            return carry

        lax.fori_loop(first_free, out_hbm.shape[0] // MOE_TM, fill_block, 0)
        lax.fori_loop(first_free, out_hbm.shape[0] // MOE_TM, done_block, 0)

    load.wait()
    for t in range(tm):
        for k in range(TOP_K):
            _row_copy(x_ref, t, out_hbm, idx_ref[k * tm + t], row_sem).start(priority=k)
    for t in range(tm):
        for k in range(TOP_K):
            _row_copy(x_ref, t, out_hbm, idx_ref[k * tm + t], row_sem).wait()


def _moe_dispatch(x2, dest2, pad_lo, pad_n, n_rows):
    t = x2.shape[0]
    tm = ROUTE_TM
    grid_spec = pltpu.PrefetchScalarGridSpec(
        num_scalar_prefetch=2,
        grid=(t // tm,),
        in_specs=[pl.BlockSpec(memory_space=pl.ANY),
                  pl.BlockSpec((tm, D_MODEL), lambda i, lo, n: (i, 0))],
        out_specs=pl.BlockSpec(memory_space=pl.ANY),
        scratch_shapes=[pltpu.SMEM((TOP_K * tm,), jnp.int32), pltpu.VMEM((MOE_TM, D_MODEL), F32),
                        pltpu.SemaphoreType.DMA, pltpu.SemaphoreType.DMA, pltpu.SemaphoreType.DMA],
    )
    return pl.pallas_call(
        _dispatch_kernel,
        grid_spec=grid_spec,
        out_shape=jax.ShapeDtypeStruct((n_rows, D_MODEL), F32),
        compiler_params=_cparams(("arbitrary",), 32),
        name="moe_dispatch",
    )(pad_lo, pad_n, dest2, x2)


def _combine_kernel(dest_hbm, x_ref, gc_ref, fn_ref, y_hbm, o_ref, idx0_ref, idx1_ref, ybuf_ref, idx_sem, row_sem):
    tm = ROUTE_TM
    i, n = pl.program_id(0), pl.num_programs(0)
    idx_refs = (idx0_ref, idx1_ref)

    def gather(idx_ref, slot, t, k):
        return _row_copy(y_hbm, idx_ref[k * tm + t], ybuf_ref.at[slot, k], t, row_sem.at[slot])

    @pl.when(i == 0)
    def _():
        first = pltpu.make_async_copy(dest_hbm.at[0], idx0_ref, idx_sem.at[0])
        first.start()
        first.wait()

        def issue(t, carry):
            for k in range(TOP_K):
                gather(idx0_ref, 0, t, k).start()
            return carry

        lax.fori_loop(0, tm, issue, 0)

    def step(slot):
        nxt = 1 - slot
        load = pltpu.make_async_copy(dest_hbm.at[jnp.minimum(i + 1, n - 1)], idx_refs[nxt], idx_sem.at[nxt])
        load.start()
        for t in range(tm):
            for k in range(TOP_K):
                gather(idx_refs[slot], slot, t, k).wait()
        load.wait()
        for t in range(tm):
            for k in range(TOP_K):
                gather(idx_refs[nxt], nxt, t, k).start(priority=k)
        gc = gc_ref[...]
        x = x_ref[...] + gc[:, 0:1] * ybuf_ref[slot, 0] + gc[:, 1:2] * ybuf_ref[slot, 1]
        o_ref[...] = _rms(x, fn_ref[...])

        @pl.when(i == n - 1)
        def _():
            for t in range(tm):
                for k in range(TOP_K):
                    gather(idx_refs[nxt], nxt, t, k).wait()

    for slot in range(2):
        pl.when(i % 2 == slot)(functools.partial(step, slot))


def _moe_combine_norm(x2, y_buf, dest2, gcol, final_norm):
    t = x2.shape[0]
    tm = ROUTE_TM
    return pl.pallas_call(
        _combine_kernel,
        grid=(t // tm,),
        in_specs=[pl.BlockSpec(memory_space=pl.ANY),
                  pl.BlockSpec((tm, D_MODEL), lambda i: (i, 0)),
                  pl.BlockSpec((tm, LANES), lambda i: (i, 0)),
                  pl.BlockSpec((1, D_MODEL), lambda i: (0, 0)),
                  pl.BlockSpec(memory_space=pl.ANY)],
        out_specs=pl.BlockSpec((tm, D_MODEL), lambda i: (i, 0)),
        out_shape=jax.ShapeDtypeStruct((t, D_MODEL), F32),
        scratch_shapes=[pltpu.SMEM((TOP_K * tm,), jnp.int32), pltpu.SMEM((TOP_K * tm,), jnp.int32),
                        pltpu.VMEM((2, TOP_K, tm, D_MODEL), F32),
                        pltpu.SemaphoreType.DMA((2,)), pltpu.SemaphoreType.DMA((2,))],
        compiler_params=_cparams(("arbitrary",), 40),
        name="moe_combine_norm",
    )(dest2, x2, gcol, final_norm.reshape(1, -1), y_buf)


def _moe_layer(x2, ffn_norm, w_router, w_gate, w_up, w_down, final_norm):
    t = x2.shape[0]
    tm = ROUTE_TM
    e01, r01, gcol, cnt = _moe_route(x2, ffn_norm, w_router)
    counts = cnt[:, 0].astype(jnp.int32)
    padded = (counts + MOE_TM - 1) // MOE_TM * MOE_TM
    pad_ends = jnp.cumsum(padded)
    pad_starts = pad_ends - padded
    n_blocks = -(-t * TOP_K // MOE_TM) + N_EXPERTS
    blk_start = jnp.arange(n_blocks, dtype=jnp.int32) * MOE_TM
    block_e = jnp.minimum(jnp.sum(blk_start[:, None] >= pad_ends[None, :], axis=1), N_EXPERTS - 1).astype(jnp.int32)
    n_used = (pad_ends[-1] // MOE_TM).astype(jnp.int32).reshape(1)
    dest = r01 + jnp.sum(jnp.where(e01[None] == jnp.arange(N_EXPERTS)[:, None, None], pad_starts[:, None, None], 0),
                         axis=0)
    dest2 = dest.reshape(TOP_K, t // tm, tm).transpose(1, 0, 2).reshape(t // tm, TOP_K * tm)
    xs = _moe_dispatch(x2, dest2, pad_starts + counts, padded - counts, n_blocks * MOE_TM)
    y_buf = _moe_experts(xs, ffn_norm, block_e, n_used, w_gate.astype(BF16), w_up.astype(BF16),
                         w_down.astype(BF16))
    return _moe_combine_norm(x2, y_buf, dest2, gcol, final_norm)


def _permute_w_in(w_in):
    s5 = w_in[:, 0:256]
    gm = w_in[:, 256:768]
    qkv = w_in[:, 768:1536]
    z = w_in[:, 1536:1792]
    a = w_in[:, 1792:1796]
    b = w_in[:, 1796:1800]
    sc = w_in[:, 1800:2568]
    ab = jnp.zeros((D_MODEL, LANES), F32).at[:, 0:GDN_HEADS].set(a).at[:, AB_B_LANE:AB_B_LANE + GDN_HEADS].set(b)
    return jnp.concatenate([gm, s5, z, qkv, sc, ab], axis=1).astype(BF16)


def kernel(x, mix_norm, w_in, s5_lam_re, s5_lam_im, s5_log_step, s5_b_re, s5_b_im, s5_c_re, s5_c_im, s5_d, s5_w_glu, s5_b_glu, s5_out_norm, sgu_ln_g, sgu_ln_b, sgu_w, sgu_b, gmlp_out_norm, gdn_conv, gdn_a_log, gdn_dt_bias, gdn_norm, sc_conv, sc_out_norm, w_out, ffn_norm, ffn_w_gate, ffn_w_up, ffn_w_down, moe_router, moe_w_gate, moe_w_up, moe_w_down, final_norm):
    bsz, seqlen, d = x.shape
    t = bsz * seqlen
    x2 = x.reshape(t, d)
    out = None
    for l in range(DEPTH):
        p2 = _in_proj(x2, mix_norm[l].reshape(1, -1), _permute_w_in(w_in[l]))
        p3 = p2.reshape(bsz, seqlen, P_COLS)
        wb, wc, pw = _s5_params(s5_lam_re[l], s5_lam_im[l], s5_log_step[l], s5_b_re[l], s5_b_im[l],
                                s5_c_re[l], s5_c_im[l])
        y_s5 = _s5_mixer(p3, wb, wc, pw, s5_d[l], s5_w_glu[l], s5_b_glu[l], s5_out_norm[l])
        y_gm = _gmlp_mixer(p3, sgu_ln_g[l], sgu_ln_b[l], sgu_w[l], sgu_b[l], gmlp_out_norm[l])
        y_gdn = _gdn_mixer(p3, gdn_conv[l], gdn_a_log[l], gdn_dt_bias[l], gdn_norm[l])
        y_sc = _shortconv_mixer(p3, sc_conv[l], sc_out_norm[l])
        ys = [y.reshape(t, BRANCH_W) for y in (y_s5, y_gm, y_gdn, y_sc)]
        x2 = _out_proj(x2, ys, w_out[l])
        i = l // 2
        if l % 2 == 0:
            x2 = _ffn_dense(x2, ffn_norm[l], ffn_w_gate[i], ffn_w_up[i], ffn_w_down[i])
        else:
            out = _moe_layer(x2, ffn_norm[l], moe_router[i], moe_w_gate[i], moe_w_up[i], moe_w_down[i],
                             final_norm)
    return out.reshape(bsz, seqlen, d)
```

```python
import functools
import math

import jax
import jax.numpy as jnp
from jax import lax
from jax.experimental import pallas as pl
from jax.experimental.pallas import tpu as pltpu

F32 = jnp.float32
BF16 = jnp.bfloat16
HIGHEST = lax.Precision.HIGHEST

D_MODEL = 1024
DEPTH = 2
BRANCH_W = 256
S5_GROUP = 16
S5_GROUPS = 16
S5_STATE = 64
S5_NSTATE = S5_GROUPS * S5_STATE
GMLP_HEADS = 4
GMLP_HEAD_DIM = 64
GMLP_CHUNK = 128
GDN_HEAD_DIM = 64
GDN_HEADS = 4
GDN_CONV = 4
GDN_CHUNK = 64
SC_CONV = 3
D_FF = 2816
N_EXPERTS = 8
TOP_K = 2
D_FF_EXPERT = 3584
EPS = 1e-6

LANES = 128
SUBLANES = 8
VMEM_BYTES_V7X = 64 * 1024 * 1024

COL_GM = 0
COL_S5 = 512
COL_Z = 768
COL_Q = 1024
COL_K = 1280
COL_V = 1536
COL_SCB = 1792
COL_SCC = 2048
COL_SCX = 2304
COL_AB = 2560
P_COLS = 2688
AB_B_LANE = 64

IN_TM = 512
S5_CHUNK = 128
S5_BATCH = 2
S5_SEG = S5_CHUNK // SUBLANES
S5_POW_ROW = 0
S5_SEG_ROW = S5_POW_ROW + S5_SEG
S5_CARRY_ROW = S5_SEG_ROW + 3 * SUBLANES
S5_TABLE_ROWS = S5_CARRY_ROW + SUBLANES
GM_TILE = 512
SC_TILE = 512
GDN_TILE = 128
GDN_BATCH = 4
OUT_TM = 512
FFN_TM = 512
FFN_TF = 256
MOE_TM = 512
MOE_TF = 512
ROUTE_TM = 512


def _cparams(sem, vmem_mb):
    return pltpu.CompilerParams(dimension_semantics=sem, vmem_limit_bytes=vmem_mb * 1024 * 1024)


def _rms(x, g):
    return x * lax.rsqrt(jnp.mean(x * x, axis=-1, keepdims=True) + EPS) * g


def _silu(x):
    return x * jax.nn.sigmoid(x)


def _bdot(a, b):
    return jnp.dot(a.astype(BF16), b.astype(BF16), preferred_element_type=F32)


def _split_dot(a, b, passes, data_on_left):
    data = a if data_on_left else b
    acc = None
    for _ in range(passes):
        piece = data.astype(BF16)
        term = (jnp.dot(piece, b, preferred_element_type=F32) if data_on_left
                else jnp.dot(a, piece, preferred_element_type=F32))
        acc = term if acc is None else acc + term
        data = data - piece.astype(F32)
    return acc


def _bdot_nt(a, b):
    return lax.dot_general(a.astype(BF16), b.astype(BF16), (((1,), (1,)), ((), ())),
                           preferred_element_type=F32)


def _in_proj_kernel(x_ref, g_ref, w_ref, o_ref):
    h = _rms(x_ref[...], g_ref[...])
    o_ref[...] = _bdot(h, w_ref[...])


def _in_proj(x2, g, w):
    t = x2.shape[0]
    return pl.pallas_call(
        _in_proj_kernel,
        grid=(t // IN_TM,),
        in_specs=[pl.BlockSpec((IN_TM, D_MODEL), lambda i: (i, 0)),
                  pl.BlockSpec((1, D_MODEL), lambda i: (0, 0)),
                  pl.BlockSpec((D_MODEL, P_COLS), lambda i: (0, 0))],
        out_specs=pl.BlockSpec((IN_TM, P_COLS), lambda i: (i, 0)),
        out_shape=jax.ShapeDtypeStruct((t, P_COLS), F32),
        compiler_params=_cparams(("parallel",), 48),
        name="in_proj",
    )(x2, g, w)


def _s5_kernel(u0_ref, u1_ref, wb_ref, pw_ref, wc_ref, d_ref, wglu_ref, bglu_ref, on_ref, o_ref,
               sr_ref, si_ref, ys_ref):
    @pl.when(pl.program_id(1) == 0)
    def _():
        sr_ref[...] = jnp.zeros_like(sr_ref)
        si_ref[...] = jnp.zeros_like(si_ref)

    for bb in range(S5_BATCH):
        _s5_sequence(u0_ref.at[bb], u1_ref.at[bb], wb_ref, pw_ref, wc_ref, d_ref, wglu_ref, bglu_ref, on_ref,
                     o_ref.at[bb], sr_ref.at[bb], si_ref.at[bb], ys_ref.at[bb])


def _s5_sequence(u0_ref, u1_ref, wb_ref, pw_ref, wc_ref, d_ref, wglu_ref, bglu_ref, on_ref, o_ref,
                 sr_ref, si_ref, ys_ref):
    n, seg = S5_NSTATE, S5_SEG
    u = jnp.concatenate(
        [jnp.concatenate([ref[pl.ds(i, SUBLANES, stride=seg), :] for i in range(seg)], axis=0)
         for ref in (u0_ref, u1_ref)], axis=1)
    bu = _bdot(u, wb_ref[...])
    cmul = lambda ar, ai, xr, xi: (ar * xr - ai * xi, ar * xi + ai * xr)
    a_r, a_i = pw_ref[S5_POW_ROW:S5_POW_ROW + 1, :n], pw_ref[S5_POW_ROW:S5_POW_ROW + 1, n:]
    xr, xi = bu[0:SUBLANES, :n], bu[0:SUBLANES, n:]
    xrs, xis = [xr], [xi]
    for i in range(1, seg):
        rows = slice(i * SUBLANES, (i + 1) * SUBLANES)
        pr, pi = cmul(a_r, a_i, xr, xi)
        xr, xi = pr + bu[rows, :n], pi + bu[rows, n:]
        xrs.append(xr)
        xis.append(xi)
    fr, fi = xr, xi
    for s in range(int(math.log2(SUBLANES))):
        rows = slice(S5_SEG_ROW + s * SUBLANES, S5_SEG_ROW + (s + 1) * SUBLANES)
        pr, pi = cmul(pw_ref[rows, :n], pw_ref[rows, n:], pltpu.roll(fr, 1 << s, 0), pltpu.roll(fi, 1 << s, 0))
        fr, fi = fr + pr, fi + pi
    sbr = jnp.broadcast_to(sr_ref[...], (SUBLANES, n))
    sbi = jnp.broadcast_to(si_ref[...], (SUBLANES, n))
    rows = slice(S5_CARRY_ROW, S5_CARRY_ROW + SUBLANES)
    pr, pi = cmul(pw_ref[rows, :n], pw_ref[rows, n:], sbr, sbi)
    fr, fi = fr + pr, fi + pi
    sr_ref[...] = fr[SUBLANES - 1:, :]
    si_ref[...] = fi[SUBLANES - 1:, :]
    first = lax.broadcasted_iota(jnp.int32, (SUBLANES, n), 0) == 0
    cin_r = jnp.where(first, sbr, pltpu.roll(fr, 1, 0))
    cin_i = jnp.where(first, sbi, pltpu.roll(fi, 1, 0))
    for i in range(seg):
        row = S5_POW_ROW + i
        pr, pi = cmul(pw_ref[row:row + 1, :n], pw_ref[row:row + 1, n:], cin_r, cin_i)
        xrs[i], xis[i] = xrs[i] + pr, xis[i] + pi
    xr, xi = jnp.concatenate(xrs, axis=0), jnp.concatenate(xis, axis=0)
    y = _bdot(xr, wc_ref[:n, :]) + _bdot(xi, wc_ref[n:, :])
    y = y + d_ref[...] * u
    y = jax.nn.gelu(y)
    y = y * jax.nn.sigmoid(_bdot(y, wglu_ref[...]) + bglu_ref[...])
    y = _rms(y, on_ref[...])
    ys_ref[0] = y[:, :LANES]
    ys_ref[1] = y[:, LANES:]
    for r in range(S5_CHUNK // SUBLANES):
        start = (S5_CHUNK // 2) * (r % 2) + r // 2
        for slab in range(BRANCH_W // LANES):
            o_ref[r * SUBLANES:(r + 1) * SUBLANES, slab * LANES:(slab + 1) * LANES] = (
                ys_ref[slab, pl.ds(start, SUBLANES, stride=SUBLANES), :])


def _s5_params(lam_re, lam_im, log_step, b_re, b_im, c_re, c_im):
    g, n, p = S5_GROUPS, S5_STATE, S5_GROUP
    dt = jnp.exp(log_step)[:, None]
    mag = jnp.exp(lam_re * dt)
    ar, ai = mag * jnp.cos(lam_im * dt), mag * jnp.sin(lam_im * dt)
    den = lam_re * lam_re + lam_im * lam_im
    fr = ((ar - 1.0) * lam_re + ai * lam_im) / den
    fi = (ai * lam_re - (ar - 1.0) * lam_im) / den
    bbr = fr[..., None] * b_re - fi[..., None] * b_im
    bbi = fr[..., None] * b_im + fi[..., None] * b_re
    eye = jnp.eye(g, dtype=F32)
    wbr = jnp.einsum('gnp,gh->gphn', bbr, eye).reshape(g * p, g * n)
    wbi = jnp.einsum('gnp,gh->gphn', bbi, eye).reshape(g * p, g * n)
    wb = jnp.concatenate([wbr, wbi], axis=1)
    wcr = jnp.einsum('gpn,gh->gnhp', c_re, eye).reshape(g * n, g * p)
    wci = jnp.einsum('gpn,gh->gnhp', -c_im, eye).reshape(g * n, g * p)
    wc = jnp.concatenate([wcr, wci], axis=0)
    def powers(br, bi, count):
        rows_r, rows_i = [br], [bi]
        for _ in range(count - 1):
            qr, qi = rows_r[-1], rows_i[-1]
            rows_r.append(qr * br - qi * bi)
            rows_i.append(qr * bi + qi * br)
        return jnp.concatenate(rows_r, axis=0), jnp.concatenate(rows_i, axis=0)

    a1r, a1i = ar.reshape(1, g * n), ai.reshape(1, g * n)
    pos_r, pos_i = powers(a1r, a1i, S5_SEG)
    a16r, a16i = pos_r[S5_SEG - 1:], pos_i[S5_SEG - 1:]
    r_idx = jnp.arange(SUBLANES)[:, None]
    seg_r, seg_i = [], []
    pr, pi = a16r, a16i
    for s in range(int(math.log2(SUBLANES))):
        keep = r_idx >= (1 << s)
        seg_r.append(jnp.where(keep, pr, 0.0))
        seg_i.append(jnp.where(keep, pi, 0.0))
        pr, pi = pr * pr - pi * pi, 2.0 * pr * pi
    car_r, car_i = powers(a16r, a16i, SUBLANES)
    pw = jnp.concatenate([jnp.concatenate([pos_r] + seg_r + [car_r], axis=0),
                          jnp.concatenate([pos_i] + seg_i + [car_i], axis=0)], axis=1)
    return wb.astype(BF16), wc.astype(BF16), pw


def _s5_mixer(p3, wb, wc, pw, d_skip, w_glu, b_glu, out_norm):
    b, l, _ = p3.shape
    n2 = 2 * S5_NSTATE
    const = lambda shape: pl.BlockSpec(shape, lambda i, j: (0,) * len(shape))
    return pl.pallas_call(
        _s5_kernel,
        grid=(b // S5_BATCH, l // S5_CHUNK),
        in_specs=[pl.BlockSpec((S5_BATCH, S5_CHUNK, LANES), lambda i, j: (i, j, COL_S5 // LANES)),
                  pl.BlockSpec((S5_BATCH, S5_CHUNK, LANES), lambda i, j: (i, j, COL_S5 // LANES + 1)),
                  const((BRANCH_W, n2)), const((S5_TABLE_ROWS, n2)), const((n2, BRANCH_W)),
                  const((1, BRANCH_W)), const((BRANCH_W, BRANCH_W)), const((1, BRANCH_W)),
                  const((1, BRANCH_W))],
        out_specs=pl.BlockSpec((S5_BATCH, S5_CHUNK, BRANCH_W), lambda i, j: (i, j, 0)),
        out_shape=jax.ShapeDtypeStruct((b, l, BRANCH_W), F32),
        scratch_shapes=[pltpu.VMEM((S5_BATCH, 1, S5_NSTATE), F32), pltpu.VMEM((S5_BATCH, 1, S5_NSTATE), F32),
                        pltpu.VMEM((S5_BATCH, BRANCH_W // LANES, S5_CHUNK, LANES), F32)],
        compiler_params=_cparams(("parallel", "arbitrary"), 40),
        name="s5_mixer",
    )(p3, p3, wb, pw, wc, d_skip.reshape(1, -1), w_glu.astype(BF16), b_glu.reshape(1, -1),
      out_norm.reshape(1, -1))


def _gmlp_kernel(p_ref, lng_ref, lnb_ref, w_ref, bias_ref, on_ref, o_ref):
    z = jax.nn.gelu(p_ref[...])
    u, v = z[:, :BRANCH_W], z[:, BRANCH_W:]
    vc = v - jnp.mean(v, axis=-1, keepdims=True)
    v = vc * lax.rsqrt(jnp.mean(vc * vc, axis=-1, keepdims=True) + EPS) * lng_ref[...] + lnb_ref[...]
    ti = lax.broadcasted_iota(jnp.int32, (GMLP_CHUNK, GMLP_CHUNK), 0)
    si = lax.broadcasted_iota(jnp.int32, (GMLP_CHUNK, GMLP_CHUNK), 1)
    tril = ti >= si
    ws = [jnp.where(tril, w_ref[h], 0.0).astype(BF16) for h in range(GMLP_HEADS)]
    lane = lax.broadcasted_iota(jnp.int32, (GMLP_CHUNK, BRANCH_W), 1)
    bias = bias_ref[...]
    outs = []
    for c in range(GM_TILE // GMLP_CHUNK):
        vb = v[c * GMLP_CHUNK:(c + 1) * GMLP_CHUNK, :].astype(BF16)
        s = jnp.dot(ws[GMLP_HEADS - 1], vb, preferred_element_type=F32)
        for h in range(GMLP_HEADS - 2, -1, -1):
            sh = jnp.dot(ws[h], vb, preferred_element_type=F32)
            s = jnp.where(lane < (h + 1) * GMLP_HEAD_DIM, sh, s)
        outs.append(s + bias)
    s = jnp.concatenate(outs, axis=0)
    o_ref[...] = _rms(u * s, on_ref[...])


def _gmlp_mixer(p3, ln_g, ln_b, w_sp, b_sp, out_norm):
    b, l, _ = p3.shape
    bias = jnp.repeat(b_sp.T, GMLP_HEAD_DIM, axis=1)
    const = lambda shape: pl.BlockSpec(shape, lambda i, j: (0,) * len(shape))
    return pl.pallas_call(
        _gmlp_kernel,
        grid=(b, l // GM_TILE),
        in_specs=[pl.BlockSpec((None, GM_TILE, 2 * BRANCH_W), lambda i, j: (i, j, COL_GM // (2 * BRANCH_W))),
                  const((1, BRANCH_W)), const((1, BRANCH_W)),
                  const((GMLP_HEADS, GMLP_CHUNK, GMLP_CHUNK)), const((GMLP_CHUNK, BRANCH_W)),
                  const((1, BRANCH_W))],
        out_specs=pl.BlockSpec((None, GM_TILE, BRANCH_W), lambda i, j: (i, j, 0)),
        out_shape=jax.ShapeDtypeStruct((b, l, BRANCH_W), F32),
        compiler_params=_cparams(("parallel", "parallel"), 32),
        name="gmlp_mixer",
    )(p3, ln_g.reshape(1, -1), ln_b.reshape(1, -1), w_sp, bias, out_norm.reshape(1, -1))


def _shortconv_kernel(b_ref, c_ref, x_ref, w_ref, on_ref, o_ref, halo_ref):
    @pl.when(pl.program_id(1) == 0)
    def _():
        halo_ref[...] = jnp.zeros_like(halo_ref)

    cx = c_ref[...] * x_ref[...]
    ext = jnp.concatenate([halo_ref[...], cx], axis=0)
    halo_ref[...] = cx[SC_TILE - SUBLANES:, :]
    y = w_ref[SC_CONV - 1:SC_CONV, :] * cx
    for j in range(SC_CONV - 1):
        sh = SC_CONV - 1 - j
        y = y + w_ref[j:j + 1, :] * pltpu.roll(ext, sh, 0)[SUBLANES:, :]
    o_ref[...] = _rms(b_ref[...] * y, on_ref[...])


def _shortconv_mixer(p3, conv_w, out_norm):
    b, l, _ = p3.shape
    w = jnp.concatenate([conv_w, jnp.zeros((SUBLANES - SC_CONV, BRANCH_W), F32)], axis=0)
    col = lambda c: pl.BlockSpec((None, SC_TILE, BRANCH_W), lambda i, j: (i, j, c // BRANCH_W))
    const = lambda shape: pl.BlockSpec(shape, lambda i, j: (0,) * len(shape))
    return pl.pallas_call(
        _shortconv_kernel,
        grid=(b, l // SC_TILE),
        in_specs=[col(COL_SCB), col(COL_SCC), col(COL_SCX), const((SUBLANES, BRANCH_W)), const((1, BRANCH_W))],
        out_specs=pl.BlockSpec((None, SC_TILE, BRANCH_W), lambda i, j: (i, j, 0)),
        out_shape=jax.ShapeDtypeStruct((b, l, BRANCH_W), F32),
        scratch_shapes=[pltpu.VMEM((SUBLANES, BRANCH_W), F32)],
        compiler_params=_cparams(("parallel", "arbitrary"), 32),
        name="shortconv_mixer",
    )(p3, p3, p3, w, out_norm.reshape(1, -1))


def _gdn_kernel(q_ref, k_ref, v_ref, z_ref, ab_ref, cw_ref, alog_ref, dtb_ref, ng_ref, o_ref,
                halo_ref, state_ref):
    c_len, h_dim, nh, w = GDN_CHUNK, GDN_HEAD_DIM, GDN_HEADS, BRANCH_W
    n_chunks = GDN_TILE // c_len

    @pl.when(pl.program_id(1) == 0)
    def _():
        halo_ref[...] = jnp.zeros_like(halo_ref)
        state_ref[...] = jnp.zeros_like(state_ref)

    shift = int(math.log2(h_dim))
    r256 = lax.broadcasted_iota(jnp.int32, (w, w), 0)
    c256 = lax.broadcasted_iota(jnp.int32, (w, w), 1)
    same_head = (r256 >> shift) == (c256 >> shift)
    ebd = jnp.where(same_head, 1.0, 0.0).astype(BF16)
    causal = same_head & (r256 >= c256)
    strict = same_head & (r256 > c256)
    eye = jnp.where(r256 == c256, 1.0, 0.0)
    r128 = lax.broadcasted_iota(jnp.int32, (LANES, w), 0)
    c128 = lax.broadcasted_iota(jnp.int32, (LANES, w), 1)
    exp_a = jnp.where(r128 == (c128 >> shift), 1.0, 0.0).astype(BF16)
    exp_b = jnp.where(r128 == (c128 >> shift) + AB_B_LANE, 1.0, 0.0).astype(BF16)

    def stack(a):
        lane_head = (lax.broadcasted_iota(jnp.int32, a.shape, 1) >> shift) & (nh - 1)
        return jnp.concatenate([jnp.where(lane_head == h, a, 0.0) for h in range(nh)], axis=0)

    def unstack(a):
        return a[0:c_len] + a[c_len:2 * c_len] + a[2 * c_len:3 * c_len] + a[3 * c_len:4 * c_len]

    def conv_silu(ref, bb, col):
        cur = ref[bb]
        ext = jnp.concatenate([halo_ref[bb, :, col * w:(col + 1) * w], cur], axis=0)
        halo_ref[bb, :, col * w:(col + 1) * w] = cur[GDN_TILE - SUBLANES:, :]
        y = cw_ref[GDN_CONV - 1:GDN_CONV, col * w:(col + 1) * w] * cur
        for j in range(GDN_CONV - 1):
            sh = GDN_CONV - 1 - j
            y = y + cw_ref[j:j + 1, col * w:(col + 1) * w] * pltpu.roll(ext, sh, 0)[SUBLANES:, :]
        return _silu(y)

    rt = lax.broadcasted_iota(jnp.int32, (GDN_TILE, GDN_TILE), 0)
    ct = lax.broadcasted_iota(jnp.int32, (GDN_TILE, GDN_TILE), 1)
    ltri = jnp.where(((rt >> shift) == (ct >> shift)) & (rt >= ct), 1.0, 0.0).astype(BF16)
    tile4 = lambda a: jnp.concatenate([a] * nh, axis=0)

    q_decs, k_decs, g_tots, rhss, a_mats, qkds = [], [], [], [], [], []
    for bb in range(GDN_BATCH):
        q = conv_silu(q_ref, bb, 0)
        k = conv_silu(k_ref, bb, 1)
        v = conv_silu(v_ref, bb, 2)
        q = q * lax.rsqrt(jnp.dot((q * q).astype(BF16), ebd, preferred_element_type=F32) + EPS) * (h_dim ** -0.5)
        k = k * lax.rsqrt(jnp.dot((k * k).astype(BF16), ebd, preferred_element_type=F32) + EPS)
        ab = ab_ref[bb]
        beta = jax.nn.sigmoid(ab)
        xa = ab + dtb_ref[...]
        softplus = jnp.maximum(xa, 0.0) + jnp.log(1.0 + jnp.exp(-jnp.abs(xa)))
        g = -jnp.exp(alog_ref[...]) * softplus
        gcum = _split_dot(ltri, g, 3, data_on_left=False)
        gexp = _split_dot(gcum, exp_a, 2, data_on_left=True)
        bexp = _split_dot(beta, exp_b, 2, data_on_left=True)
        for c in range(n_chunks):
            sl = slice(c * c_len, (c + 1) * c_len)
            qc, kc, vc, gc, bc = q[sl], k[sl], v[sl], gexp[sl], bexp[sl]
            glast = gc[c_len - 1:c_len, :]
            eg = jnp.exp(gc)
            q_decs.append(qc * eg)
            k_decs.append(kc * jnp.exp(glast - gc))
            g_tots.append(jnp.exp(glast))
            kst = stack(kc).astype(BF16)
            kk = _bdot_nt(kst, kst)
            qk = _bdot_nt(stack(qc), kst)
            gcol = stack(gc)
            bcol = stack(bc)
            dec = jnp.exp(jnp.where(causal, gcol - gcol.T, -jnp.inf))
            a_mats.append(jnp.where(strict, bcol * kk * dec, 0.0))
            qkds.append((qk * dec).astype(BF16))
            rhss.append(jnp.concatenate([tile4(vc) * bcol, tile4(kc * eg) * bcol], axis=1).astype(BF16))
    t_invs = [eye - a for a in a_mats]
    pws = a_mats
    for _ in range(int(math.log2(c_len)) - 1):
        pws = [_bdot(p, p) for p in pws]
        t_invs = [t + _bdot(t, p) for t, p in zip(t_invs, pws)]
    sols = [unstack(_bdot(t, r)) for t, r in zip(t_invs, rhss)]
    states = [state_ref[bb] for bb in range(GDN_BATCH)]
    outs = [[] for _ in range(GDN_BATCH)]
    for c in range(n_chunks):
        for bb in range(GDN_BATCH):
            i = bb * n_chunks + c
            w_v, w_k = sols[i][:, :w], sols[i][:, w:]
            v_new = w_v - _bdot(w_k, states[bb])
            o = _bdot(q_decs[i], states[bb]) + unstack(_bdot(qkds[i], stack(v_new)))
            states[bb] = states[bb] * g_tots[i] + jnp.where(same_head, _bdot(k_decs[i].T, v_new), 0.0)
            outs[bb].append(o)
    for bb in range(GDN_BATCH):
        state_ref[bb] = states[bb]
        o = jnp.concatenate(outs[bb], axis=0)
        ms = jnp.dot((o * o).astype(BF16), ebd, preferred_element_type=F32) * (1.0 / h_dim)
        o = o * lax.rsqrt(ms + EPS) * ng_ref[...]
        o_ref[bb] = o * _silu(z_ref[bb])


def _gdn_mixer(p3, conv_w, a_log, dt_bias, norm_g):
    b, l, _ = p3.shape
    cw = jnp.concatenate([conv_w, jnp.zeros((SUBLANES - GDN_CONV, 3 * BRANCH_W), F32)], axis=0)
    lane_row = lambda vec: jnp.zeros((1, LANES), F32).at[0, :GDN_HEADS].set(vec)
    col = lambda c: pl.BlockSpec((GDN_BATCH, GDN_TILE, BRANCH_W), lambda i, j: (i, j, c // BRANCH_W))
    const = lambda shape: pl.BlockSpec(shape, lambda i, j: (0,) * len(shape))
    return pl.pallas_call(
        _gdn_kernel,
        grid=(b // GDN_BATCH, l // GDN_TILE),
        in_specs=[col(COL_Q), col(COL_K), col(COL_V), col(COL_Z),
                  pl.BlockSpec((GDN_BATCH, GDN_TILE, LANES), lambda i, j: (i, j, COL_AB // LANES)),
                  const((SUBLANES, 3 * BRANCH_W)), const((1, LANES)), const((1, LANES)), const((1, BRANCH_W))],
        out_specs=pl.BlockSpec((GDN_BATCH, GDN_TILE, BRANCH_W), lambda i, j: (i, j, 0)),
        out_shape=jax.ShapeDtypeStruct((b, l, BRANCH_W), F32),
        scratch_shapes=[pltpu.VMEM((GDN_BATCH, SUBLANES, 3 * BRANCH_W), F32),
                        pltpu.VMEM((GDN_BATCH, BRANCH_W, BRANCH_W), F32)],
        compiler_params=_cparams(("parallel", "arbitrary"), 48),
        name="gdn_mixer",
    )(p3, p3, p3, p3, p3, cw, lane_row(a_log), lane_row(dt_bias), jnp.tile(norm_g, GDN_HEADS).reshape(1, -1))


def _mix_residual(x_ref, y_refs, wo_ref):
    acc = x_ref[...]
    for i, y_ref in enumerate(y_refs):
        acc = acc + _bdot(y_ref[...], wo_ref[i * BRANCH_W:(i + 1) * BRANCH_W, :])
    return acc


def _ffn_kernel(x_ref, y0_ref, y1_ref, y2_ref, y3_ref, wo_ref, g_ref, wg_ref, wu_ref, wd_ref, o_ref):
    x = _mix_residual(x_ref, (y0_ref, y1_ref, y2_ref, y3_ref), wo_ref)
    hb = _rms(x, g_ref[...]).astype(BF16)
    acc = None
    for f in range(D_FF // FFN_TF):
        cols = slice(f * FFN_TF, (f + 1) * FFN_TF)
        a = jnp.dot(hb, wg_ref[:, cols], preferred_element_type=F32)
        u = jnp.dot(hb, wu_ref[:, cols], preferred_element_type=F32)
        y = jnp.dot((_silu(a) * u).astype(BF16), wd_ref[cols, :], preferred_element_type=F32)
        acc = y if acc is None else acc + y
    o_ref[...] = x + acc


def _ffn_dense(x2, ys, w_out, g, w_gate, w_up, w_down):
    t = x2.shape[0]
    row = lambda n: pl.BlockSpec((FFN_TM, n), lambda i: (i, 0))
    resident = lambda shape: pl.BlockSpec(shape, lambda i: (0, 0), pipeline_mode=pl.Buffered(1))
    return pl.pallas_call(
        _ffn_kernel,
        grid=(t // FFN_TM,),
        in_specs=[row(D_MODEL)] + [row(BRANCH_W)] * 4 + [pl.BlockSpec((D_MODEL, D_MODEL), lambda i: (0, 0)),
                  pl.BlockSpec((1, D_MODEL), lambda i: (0, 0)),
                  resident((D_MODEL, D_FF)), resident((D_MODEL, D_FF)), resident((D_FF, D_MODEL))],
        out_specs=row(D_MODEL),
        out_shape=jax.ShapeDtypeStruct((t, D_MODEL), F32),
        compiler_params=_cparams(("parallel",), 52),
        name="ffn_dense",
    )(x2, *ys, w_out.astype(BF16), g.reshape(1, -1), w_gate.astype(BF16), w_up.astype(BF16), w_down.astype(BF16))


def _moe_kernel(be_ref, nu_ref, x_ref, g_ref, wg_ref, wu_ref, wd_ref, o_ref):
    del be_ref

    @pl.when(pl.program_id(0) < nu_ref[0])
    def _():
        xb = _rms(x_ref[...], g_ref[...]).astype(BF16)
        acc = None
        for f in range(D_FF_EXPERT // MOE_TF):
            cols = slice(f * MOE_TF, (f + 1) * MOE_TF)
            a = jnp.dot(xb, wg_ref[:, cols], preferred_element_type=F32)
            u = jnp.dot(xb, wu_ref[:, cols], preferred_element_type=F32)
            y = jnp.dot((_silu(a) * u).astype(BF16), wd_ref[cols, :], preferred_element_type=F32)
            acc = y if acc is None else acc + y
        o_ref[...] = acc

    @pl.when(pl.program_id(0) >= nu_ref[0])
    def _():
        o_ref[...] = jnp.zeros_like(o_ref)


def _moe_experts(buf, g, block_e, n_used, wg, wu, wd):
    rows = buf.shape[0]
    expert = lambda shape: pl.BlockSpec((None,) + shape, lambda m, be, nu: (be[m], 0, 0),
                                        pipeline_mode=pl.Buffered(1))
    grid_spec = pltpu.PrefetchScalarGridSpec(
        num_scalar_prefetch=2,
        grid=(rows // MOE_TM,),
        in_specs=[pl.BlockSpec((MOE_TM, D_MODEL), lambda m, be, nu: (jnp.minimum(m, nu[0] - 1), 0)),
                  pl.BlockSpec((1, D_MODEL), lambda m, be, nu: (0, 0)),
                  expert((D_MODEL, D_FF_EXPERT)), expert((D_MODEL, D_FF_EXPERT)), expert((D_FF_EXPERT, D_MODEL))],
        out_specs=pl.BlockSpec((MOE_TM, D_MODEL), lambda m, be, nu: (m, 0)),
    )
    return pl.pallas_call(
        _moe_kernel,
        grid_spec=grid_spec,
        out_shape=jax.ShapeDtypeStruct((rows, D_MODEL), F32),
        compiler_params=_cparams(("arbitrary",), 56),
        name="moe_experts",
    )(block_e, n_used, buf, g.reshape(1, -1), wg, wu, wd)


def _route_kernel(x_ref, y0_ref, y1_ref, y2_ref, y3_ref, wo_ref, g_ref, wt_ref,
                  xo_ref, e_ref, r_ref, gc_ref, cnt_ref, run_ref):
    tm = ROUTE_TM

    @pl.when(pl.program_id(0) == 0)
    def _():
        run_ref[...] = jnp.zeros_like(run_ref)

    x = _mix_residual(x_ref, (y0_ref, y1_ref, y2_ref, y3_ref), wo_ref)
    xo_ref[...] = x
    h = _rms(x, g_ref[...])
    lt = lax.dot_general(wt_ref[...], h, (((1,), (1,)), ((), ())), precision=HIGHEST,
                         preferred_element_type=F32)
    sub = lax.broadcasted_iota(jnp.int32, (N_EXPERTS, tm), 0)
    m1 = jnp.max(lt, axis=0, keepdims=True)
    i1 = jnp.min(jnp.where(lt == m1, sub, N_EXPERTS), axis=0, keepdims=True)
    lt2 = jnp.where(sub == i1, -jnp.inf, lt)
    m2 = jnp.max(lt2, axis=0, keepdims=True)
    i2 = jnp.min(jnp.where(lt2 == m2, sub, N_EXPERTS), axis=0, keepdims=True)
    oh0 = jnp.where(sub == i1, 1.0, 0.0)
    oh1 = jnp.where(sub == i2, 1.0, 0.0)
    cnt = oh0 + oh1
    ti = lax.broadcasted_iota(jnp.int32, (tm, tm), 0)
    tj = lax.broadcasted_iota(jnp.int32, (tm, tm), 1)
    upper = jnp.where(ti < tj, 1.0, 0.0).astype(BF16)
    pre = jnp.dot(cnt.astype(BF16), upper, preferred_element_type=F32) + run_ref[:, 0:1]
    r0 = jnp.sum(oh0 * pre, axis=0, keepdims=True)
    r1 = jnp.sum(oh1 * pre, axis=0, keepdims=True)
    e_ref[...] = jnp.concatenate([i1, i2], axis=0)
    r_ref[...] = jnp.concatenate([r0, r1], axis=0).astype(jnp.int32)
    run = run_ref[...] + jnp.sum(cnt, axis=1, keepdims=True)
    run_ref[...] = run
    cnt_ref[...] = run
    ex = jnp.exp(m2 - m1)
    g0 = 1.0 / (1.0 + ex)
    gates = jnp.where(sub == 0, g0, jnp.where(sub == 1, ex * g0, 0.0))
    er = lax.broadcasted_iota(jnp.int32, (N_EXPERTS, LANES), 0)
    ec = lax.broadcasted_iota(jnp.int32, (N_EXPERTS, LANES), 1)
    eye = jnp.where(er == ec, 1.0, 0.0).astype(BF16)
    acc = None
    for _ in range(3):
        piece = gates.astype(BF16)
        term = lax.dot_general(piece, eye, (((0,), (0,)), ((), ())), preferred_element_type=F32)
        acc = term if acc is None else acc + term
        gates = gates - piece.astype(F32)
    gc_ref[...] = acc


def _moe_route(x2, ys, w_out, ffn_norm, w_router):
    t = x2.shape[0]
    tm = ROUTE_TM
    row = lambda n: pl.BlockSpec((tm, n), lambda i: (i, 0))
    return pl.pallas_call(
        _route_kernel,
        grid=(t // tm,),
        in_specs=[row(D_MODEL)] + [row(BRANCH_W)] * 4 + [pl.BlockSpec((D_MODEL, D_MODEL), lambda i: (0, 0)),
                  pl.BlockSpec((1, D_MODEL), lambda i: (0, 0)),
                  pl.BlockSpec((N_EXPERTS, D_MODEL), lambda i: (0, 0))],
        out_specs=[row(D_MODEL),
                   pl.BlockSpec((TOP_K, tm), lambda i: (0, i)),
                   pl.BlockSpec((TOP_K, tm), lambda i: (0, i)),
                   row(LANES),
                   pl.BlockSpec((N_EXPERTS, LANES), lambda i: (0, 0))],
        out_shape=[jax.ShapeDtypeStruct((t, D_MODEL), F32),
                   jax.ShapeDtypeStruct((TOP_K, t), jnp.int32),
                   jax.ShapeDtypeStruct((TOP_K, t), jnp.int32),
                   jax.ShapeDtypeStruct((t, LANES), F32),
                   jax.ShapeDtypeStruct((N_EXPERTS, LANES), F32)],
        scratch_shapes=[pltpu.VMEM((N_EXPERTS, LANES), F32)],
        compiler_params=_cparams(("arbitrary",), 40),
        name="moe_route",
    )(x2, *ys, w_out.astype(BF16), ffn_norm.reshape(1, -1), w_router.T)


def _row_copy(src_ref, src_row, dst_ref, dst_row, sem):
    return pltpu.make_async_copy(src_ref.at[pl.ds(src_row, 1), :], dst_ref.at[pl.ds(dst_row, 1), :], sem)


def _dispatch_kernel(plo_ref, pn_ref, dest_hbm, x_ref, out_hbm, idx_ref, zero_ref, idx_sem, row_sem, pad_sem):
    tm = ROUTE_TM
    load = pltpu.make_async_copy(dest_hbm.at[pl.program_id(0)], idx_ref, idx_sem)
    load.start()

    @pl.when(pl.program_id(0) == 0)
    def _():
        zero_ref[...] = jnp.zeros_like(zero_ref)
        for e in range(N_EXPERTS):
            def fill(r, carry, e=e):
                _row_copy(zero_ref, 0, out_hbm, plo_ref[e] + r, pad_sem).start()
                return carry
            lax.fori_loop(0, pn_ref[e], fill, 0)
        for e in range(N_EXPERTS):
            def done(r, carry, e=e):
                _row_copy(zero_ref, 0, out_hbm, plo_ref[e] + r, pad_sem).wait()
                return carry
            lax.fori_loop(0, pn_ref[e], done, 0)
        first_free = (plo_ref[N_EXPERTS - 1] + pn_ref[N_EXPERTS - 1]) // MOE_TM

        def block_copy(b):
            return pltpu.make_async_copy(zero_ref, out_hbm.at[pl.ds(pl.multiple_of(b * MOE_TM, MOE_TM), MOE_TM), :],
                                         pad_sem)

        def fill_block(b, carry):
            block_copy(b).start()
            return carry

        def done_block(b, carry):
            block_copy(b).wait()
            return carry

        lax.fori_loop(first_free, out_hbm.shape[0] // MOE_TM, fill_block, 0)
        lax.fori_loop(first_free, out_hbm.shape[0] // MOE_TM, done_block, 0)

    load.wait()
    for t in range(tm):
        for k in range(TOP_K):
            _row_copy(x_ref, t, out_hbm, idx_ref[k * tm + t], row_sem).start(priority=k)
    for t in range(tm):
        for k in range(TOP_K):
            _row_copy(x_ref, t, out_hbm, idx_ref[k * tm + t], row_sem).wait()


def _moe_dispatch(x2, dest2, pad_lo, pad_n, n_rows):
    t = x2.shape[0]
    tm = ROUTE_TM
    grid_spec = pltpu.PrefetchScalarGridSpec(
        num_scalar_prefetch=2,
        grid=(t // tm,),
        in_specs=[pl.BlockSpec(memory_space=pl.ANY),
                  pl.BlockSpec((tm, D_MODEL), lambda i, lo, n: (i, 0))],
        out_specs=pl.BlockSpec(memory_space=pl.ANY),
        scratch_shapes=[pltpu.SMEM((TOP_K * tm,), jnp.int32), pltpu.VMEM((MOE_TM, D_MODEL), F32),
                        pltpu.SemaphoreType.DMA, pltpu.SemaphoreType.DMA, pltpu.SemaphoreType.DMA],
    )
    return pl.pallas_call(
        _dispatch_kernel,
        grid_spec=grid_spec,
        out_shape=jax.ShapeDtypeStruct((n_rows, D_MODEL), F32),
        compiler_params=_cparams(("arbitrary",), 32),
        name="moe_dispatch",
    )(pad_lo, pad_n, dest2, x2)


def _combine_kernel(dest_hbm, x_ref, gc_ref, fn_ref, y_hbm, o_ref, idx0_ref, idx1_ref, ybuf_ref, idx_sem, row_sem):
    tm = ROUTE_TM
    i, n = pl.program_id(0), pl.num_programs(0)
    idx_refs = (idx0_ref, idx1_ref)

    def gather(idx_ref, slot, t, k):
        return _row_copy(y_hbm, idx_ref[k * tm + t], ybuf_ref.at[slot, k], t, row_sem.at[slot])

    @pl.when(i == 0)
    def _():
        first = pltpu.make_async_copy(dest_hbm.at[0], idx0_ref, idx_sem.at[0])
        first.start()
        first.wait()

        def issue(t, carry):
            for k in range(TOP_K):
                gather(idx0_ref, 0, t, k).start()
            return carry

        lax.fori_loop(0, tm, issue, 0)

    def step(slot):
        nxt = 1 - slot
        load = pltpu.make_async_copy(dest_hbm.at[jnp.minimum(i + 1, n - 1)], idx_refs[nxt], idx_sem.at[nxt])
        load.start()
        for t in range(tm):
            for k in range(TOP_K):
                gather(idx_refs[slot], slot, t, k).wait()
        load.wait()
        for t in range(tm):
            for k in range(TOP_K):
                gather(idx_refs[nxt], nxt, t, k).start(priority=k)
        gc = gc_ref[...]
        x = x_ref[...] + gc[:, 0:1] * ybuf_ref[slot, 0] + gc[:, 1:2] * ybuf_ref[slot, 1]
        o_ref[...] = _rms(x, fn_ref[...])

        @pl.when(i == n - 1)
        def _():
            for t in range(tm):
                for k in range(TOP_K):
                    gather(idx_refs[nxt], nxt, t, k).wait()

    for slot in range(2):
        pl.when(i % 2 == slot)(functools.partial(step, slot))


def _moe_combine_norm(x2, y_buf, dest2, gcol, final_norm):
    t = x2.shape[0]
    tm = ROUTE_TM
    return pl.pallas_call(
        _combine_kernel,
        grid=(t // tm,),
        in_specs=[pl.BlockSpec(memory_space=pl.ANY),
                  pl.BlockSpec((tm, D_MODEL), lambda i: (i, 0)),
                  pl.BlockSpec((tm, LANES), lambda i: (i, 0)),
                  pl.BlockSpec((1, D_MODEL), lambda i: (0, 0)),
                  pl.BlockSpec(memory_space=pl.ANY)],
        out_specs=pl.BlockSpec((tm, D_MODEL), lambda i: (i, 0)),
        out_shape=jax.ShapeDtypeStruct((t, D_MODEL), F32),
        scratch_shapes=[pltpu.SMEM((TOP_K * tm,), jnp.int32), pltpu.SMEM((TOP_K * tm,), jnp.int32),
                        pltpu.VMEM((2, TOP_K, tm, D_MODEL), F32),
                        pltpu.SemaphoreType.DMA((2,)), pltpu.SemaphoreType.DMA((2,))],
        compiler_params=_cparams(("arbitrary",), 40),
        name="moe_combine_norm",
    )(dest2, x2, gcol, final_norm.reshape(1, -1), y_buf)


def _moe_layer(x2, ys, w_out, ffn_norm, w_router, w_gate, w_up, w_down, final_norm):
    t = x2.shape[0]
    tm = ROUTE_TM
    x2, e01, r01, gcol, cnt = _moe_route(x2, ys, w_out, ffn_norm, w_router)
    counts = cnt[:, 0].astype(jnp.int32)
    padded = (counts + MOE_TM - 1) // MOE_TM * MOE_TM
    pad_ends = jnp.cumsum(padded)
    pad_starts = pad_ends - padded
    n_blocks = -(-t * TOP_K // MOE_TM) + N_EXPERTS
    blk_start = jnp.arange(n_blocks, dtype=jnp.int32) * MOE_TM
    block_e = jnp.minimum(jnp.sum(blk_start[:, None] >= pad_ends[None, :], axis=1), N_EXPERTS - 1).astype(jnp.int32)
    n_used = (pad_ends[-1] // MOE_TM).astype(jnp.int32).reshape(1)
    dest = r01 + jnp.sum(jnp.where(e01[None] == jnp.arange(N_EXPERTS)[:, None, None], pad_starts[:, None, None], 0),
                         axis=0)
    dest2 = dest.reshape(TOP_K, t // tm, tm).transpose(1, 0, 2).reshape(t // tm, TOP_K * tm)
    xs = _moe_dispatch(x2, dest2, pad_starts + counts, padded - counts, n_blocks * MOE_TM)
    y_buf = _moe_experts(xs, ffn_norm, block_e, n_used, w_gate.astype(BF16), w_up.astype(BF16),
                         w_down.astype(BF16))
    return _moe_combine_norm(x2, y_buf, dest2, gcol, final_norm)


def _permute_w_in(w_in):
    s5 = w_in[:, 0:256]
    gm = w_in[:, 256:768]
    qkv = w_in[:, 768:1536]
    z = w_in[:, 1536:1792]
    a = w_in[:, 1792:1796]
    b = w_in[:, 1796:1800]
    sc = w_in[:, 1800:2568]
    ab = jnp.zeros((D_MODEL, LANES), F32).at[:, 0:GDN_HEADS].set(a).at[:, AB_B_LANE:AB_B_LANE + GDN_HEADS].set(b)
    return jnp.concatenate([gm, s5, z, qkv, sc, ab], axis=1).astype(BF16)


def kernel(x, mix_norm, w_in, s5_lam_re, s5_lam_im, s5_log_step, s5_b_re, s5_b_im, s5_c_re, s5_c_im, s5_d, s5_w_glu, s5_b_glu, s5_out_norm, sgu_ln_g, sgu_ln_b, sgu_w, sgu_b, gmlp_out_norm, gdn_conv, gdn_a_log, gdn_dt_bias, gdn_norm, sc_conv, sc_out_norm, w_out, ffn_norm, ffn_w_gate, ffn_w_up, ffn_w_down, moe_router, moe_w_gate, moe_w_up, moe_w_down, final_norm):
    bsz, seqlen, d = x.shape
    t = bsz * seqlen
    x2 = x.reshape(t, d)
    out = None
    for l in range(DEPTH):
        p2 = _in_proj(x2, mix_norm[l].reshape(1, -1), _permute_w_in(w_in[l]))
        p3 = p2.reshape(bsz, seqlen, P_COLS)
        wb, wc, pw = _s5_params(s5_lam_re[l], s5_lam_im[l], s5_log_step[l], s5_b_re[l], s5_b_im[l],
                                s5_c_re[l], s5_c_im[l])
        y_s5 = _s5_mixer(p3, wb, wc, pw, s5_d[l], s5_w_glu[l], s5_b_glu[l], s5_out_norm[l])
        y_gm = _gmlp_mixer(p3, sgu_ln_g[l], sgu_ln_b[l], sgu_w[l], sgu_b[l], gmlp_out_norm[l])
        y_gdn = _gdn_mixer(p3, gdn_conv[l], gdn_a_log[l], gdn_dt_bias[l], gdn_norm[l])
        y_sc = _shortconv_mixer(p3, sc_conv[l], sc_out_norm[l])
        ys = [y.reshape(t, BRANCH_W) for y in (y_s5, y_gm, y_gdn, y_sc)]
        i = l // 2
        if l % 2 == 0:
            x2 = _ffn_dense(x2, ys, w_out[l], ffn_norm[l], ffn_w_gate[i], ffn_w_up[i], ffn_w_down[i])
        else:
            out = _moe_layer(x2, ys, w_out[l], ffn_norm[l], moe_router[i], moe_w_gate[i], moe_w_up[i],
                             moe_w_down[i], final_norm)
    return out.reshape(bsz, seqlen, d)
```

```python
import functools
import math

import jax
import jax.numpy as jnp
from jax import lax
from jax.experimental import pallas as pl
from jax.experimental.pallas import tpu as pltpu

F32 = jnp.float32
BF16 = jnp.bfloat16
HIGHEST = lax.Precision.HIGHEST

D_MODEL = 1024
DEPTH = 2
BRANCH_W = 256
S5_GROUP = 16
S5_GROUPS = 16
S5_STATE = 64
S5_NSTATE = S5_GROUPS * S5_STATE
GMLP_HEADS = 4
GMLP_HEAD_DIM = 64
GMLP_CHUNK = 128
GDN_HEAD_DIM = 64
GDN_HEADS = 4
GDN_CONV = 4
GDN_CHUNK = 64
SC_CONV = 3
D_FF = 2816
N_EXPERTS = 8
TOP_K = 2
D_FF_EXPERT = 3584
EPS = 1e-6

LANES = 128
SUBLANES = 8
VMEM_BYTES_V7X = 64 * 1024 * 1024

COL_GM = 0
COL_S5 = 512
COL_Z = 768
COL_Q = 1024
COL_K = 1280
COL_V = 1536
COL_SCB = 1792
COL_SCC = 2048
COL_SCX = 2304
COL_AB = 2560
P_COLS = 2688
AB_B_LANE = 64

IN_TM = 512
S5_CHUNK = 128
S5_BATCH = 4
S5_SEG = S5_CHUNK // SUBLANES
S5_POW_ROW = 0
S5_SEG_ROW = S5_POW_ROW + S5_SEG
S5_CARRY_ROW = S5_SEG_ROW + 3 * SUBLANES
S5_TABLE_ROWS = S5_CARRY_ROW + SUBLANES
GM_TILE = 512
SC_TILE = 512
GDN_TILE = 128
GDN_BATCH = 4
OUT_TM = 512
FFN_TM = 512
FFN_TF = 256
MOE_TM = 512
MOE_TF = 512
ROUTE_TM = 512


def _cparams(sem, vmem_mb):
    return pltpu.CompilerParams(dimension_semantics=sem, vmem_limit_bytes=vmem_mb * 1024 * 1024)


def _rms(x, g):
    return x * lax.rsqrt(jnp.mean(x * x, axis=-1, keepdims=True) + EPS) * g


def _silu(x):
    return x * jax.nn.sigmoid(x)


def _bdot(a, b):
    return jnp.dot(a.astype(BF16), b.astype(BF16), preferred_element_type=F32)


def _split_dot(a, b, passes, data_on_left):
    data = a if data_on_left else b
    acc = None
    for _ in range(passes):
        piece = data.astype(BF16)
        term = (jnp.dot(piece, b, preferred_element_type=F32) if data_on_left
                else jnp.dot(a, piece, preferred_element_type=F32))
        acc = term if acc is None else acc + term
        data = data - piece.astype(F32)
    return acc


def _bdot_nt(a, b):
    return lax.dot_general(a.astype(BF16), b.astype(BF16), (((1,), (1,)), ((), ())),
                           preferred_element_type=F32)


def _in_proj_kernel(x_ref, g_ref, w_ref, o_ref):
    h = _rms(x_ref[...], g_ref[...])
    o_ref[...] = _bdot(h, w_ref[...])


def _in_proj(x2, g, w):
    t = x2.shape[0]
    return pl.pallas_call(
        _in_proj_kernel,
        grid=(t // IN_TM,),
        in_specs=[pl.BlockSpec((IN_TM, D_MODEL), lambda i: (i, 0)),
                  pl.BlockSpec((1, D_MODEL), lambda i: (0, 0)),
                  pl.BlockSpec((D_MODEL, P_COLS), lambda i: (0, 0))],
        out_specs=pl.BlockSpec((IN_TM, P_COLS), lambda i: (i, 0)),
        out_shape=jax.ShapeDtypeStruct((t, P_COLS), F32),
        compiler_params=_cparams(("parallel",), 48),
        name="in_proj",
    )(x2, g, w)


def _s5_kernel(u0_ref, u1_ref, wb_ref, pw_ref, wc_ref, d_ref, wglu_ref, bglu_ref, on_ref, o_ref,
               sr_ref, si_ref, ys_ref):
    @pl.when(pl.program_id(1) == 0)
    def _():
        sr_ref[...] = jnp.zeros_like(sr_ref)
        si_ref[...] = jnp.zeros_like(si_ref)

    seg, c = S5_SEG, S5_CHUNK
    u = jnp.concatenate(
        [jnp.concatenate([jnp.concatenate([ref[bb, pl.ds(i, SUBLANES, stride=seg), :] for i in range(seg)], axis=0)
                          for ref in (u0_ref, u1_ref)], axis=1) for bb in range(S5_BATCH)], axis=0)
    bu = _bdot(u, wb_ref[...])
    xs = [_s5_scan(bu[bb * c:(bb + 1) * c], pw_ref, sr_ref.at[bb], si_ref.at[bb]) for bb in range(S5_BATCH)]
    x = jnp.concatenate(xs, axis=0)
    y = _bdot(x, wc_ref[...])
    y = y + d_ref[...] * u
    y = jax.nn.gelu(y)
    y = y * jax.nn.sigmoid(_bdot(y, wglu_ref[...]) + bglu_ref[...])
    y = _rms(y, on_ref[...])
    for bb in range(S5_BATCH):
        for slab in range(BRANCH_W // LANES):
            ys_ref[bb, slab] = y[bb * c:(bb + 1) * c, slab * LANES:(slab + 1) * LANES]
        for r in range(c // SUBLANES):
            start = (c // 2) * (r % 2) + r // 2
            for slab in range(BRANCH_W // LANES):
                o_ref[bb, r * SUBLANES:(r + 1) * SUBLANES, slab * LANES:(slab + 1) * LANES] = (
                    ys_ref[bb, slab, pl.ds(start, SUBLANES, stride=SUBLANES), :])


def _s5_scan(bu, pw_ref, sr_ref, si_ref):
    n, seg = S5_NSTATE, S5_SEG
    cmul = lambda ar, ai, xr, xi: (ar * xr - ai * xi, ar * xi + ai * xr)
    a_r, a_i = pw_ref[S5_POW_ROW:S5_POW_ROW + 1, :n], pw_ref[S5_POW_ROW:S5_POW_ROW + 1, n:]
    xr, xi = bu[0:SUBLANES, :n], bu[0:SUBLANES, n:]
    xrs, xis = [xr], [xi]
    for i in range(1, seg):
        rows = slice(i * SUBLANES, (i + 1) * SUBLANES)
        pr, pi = cmul(a_r, a_i, xr, xi)
        xr, xi = pr + bu[rows, :n], pi + bu[rows, n:]
        xrs.append(xr)
        xis.append(xi)
    fr, fi = xr, xi
    for s in range(int(math.log2(SUBLANES))):
        rows = slice(S5_SEG_ROW + s * SUBLANES, S5_SEG_ROW + (s + 1) * SUBLANES)
        pr, pi = cmul(pw_ref[rows, :n], pw_ref[rows, n:], pltpu.roll(fr, 1 << s, 0), pltpu.roll(fi, 1 << s, 0))
        fr, fi = fr + pr, fi + pi
    sbr = jnp.broadcast_to(sr_ref[...], (SUBLANES, n))
    sbi = jnp.broadcast_to(si_ref[...], (SUBLANES, n))
    rows = slice(S5_CARRY_ROW, S5_CARRY_ROW + SUBLANES)
    pr, pi = cmul(pw_ref[rows, :n], pw_ref[rows, n:], sbr, sbi)
    fr, fi = fr + pr, fi + pi
    sr_ref[...] = fr[SUBLANES - 1:, :]
    si_ref[...] = fi[SUBLANES - 1:, :]
    first = lax.broadcasted_iota(jnp.int32, (SUBLANES, n), 0) == 0
    cin_r = jnp.where(first, sbr, pltpu.roll(fr, 1, 0))
    cin_i = jnp.where(first, sbi, pltpu.roll(fi, 1, 0))
    for i in range(seg):
        row = S5_POW_ROW + i
        pr, pi = cmul(pw_ref[row:row + 1, :n], pw_ref[row:row + 1, n:], cin_r, cin_i)
        xrs[i], xis[i] = xrs[i] + pr, xis[i] + pi
    return jnp.concatenate([jnp.concatenate(xrs, axis=0), jnp.concatenate(xis, axis=0)], axis=1)


def _s5_params(lam_re, lam_im, log_step, b_re, b_im, c_re, c_im):
    g, n, p = S5_GROUPS, S5_STATE, S5_GROUP
    dt = jnp.exp(log_step)[:, None]
    mag = jnp.exp(lam_re * dt)
    ar, ai = mag * jnp.cos(lam_im * dt), mag * jnp.sin(lam_im * dt)
    den = lam_re * lam_re + lam_im * lam_im
    fr = ((ar - 1.0) * lam_re + ai * lam_im) / den
    fi = (ai * lam_re - (ar - 1.0) * lam_im) / den
    bbr = fr[..., None] * b_re - fi[..., None] * b_im
    bbi = fr[..., None] * b_im + fi[..., None] * b_re
    eye = jnp.eye(g, dtype=F32)
    wbr = jnp.einsum('gnp,gh->gphn', bbr, eye).reshape(g * p, g * n)
    wbi = jnp.einsum('gnp,gh->gphn', bbi, eye).reshape(g * p, g * n)
    wb = jnp.concatenate([wbr, wbi], axis=1)
    wcr = jnp.einsum('gpn,gh->gnhp', c_re, eye).reshape(g * n, g * p)
    wci = jnp.einsum('gpn,gh->gnhp', -c_im, eye).reshape(g * n, g * p)
    wc = jnp.concatenate([wcr, wci], axis=0)
    def powers(br, bi, count):
        rows_r, rows_i = [br], [bi]
        for _ in range(count - 1):
            qr, qi = rows_r[-1], rows_i[-1]
            rows_r.append(qr * br - qi * bi)
            rows_i.append(qr * bi + qi * br)
        return jnp.concatenate(rows_r, axis=0), jnp.concatenate(rows_i, axis=0)

    a1r, a1i = ar.reshape(1, g * n), ai.reshape(1, g * n)
    pos_r, pos_i = powers(a1r, a1i, S5_SEG)
    a16r, a16i = pos_r[S5_SEG - 1:], pos_i[S5_SEG - 1:]
    r_idx = jnp.arange(SUBLANES)[:, None]
    seg_r, seg_i = [], []
    pr, pi = a16r, a16i
    for s in range(int(math.log2(SUBLANES))):
        keep = r_idx >= (1 << s)
        seg_r.append(jnp.where(keep, pr, 0.0))
        seg_i.append(jnp.where(keep, pi, 0.0))
        pr, pi = pr * pr - pi * pi, 2.0 * pr * pi
    car_r, car_i = powers(a16r, a16i, SUBLANES)
    pw = jnp.concatenate([jnp.concatenate([pos_r] + seg_r + [car_r], axis=0),
                          jnp.concatenate([pos_i] + seg_i + [car_i], axis=0)], axis=1)
    return wb.astype(BF16), wc.astype(BF16), pw


def _s5_mixer(p3, wb, wc, pw, d_skip, w_glu, b_glu, out_norm):
    b, l, _ = p3.shape
    n2 = 2 * S5_NSTATE
    const = lambda shape: pl.BlockSpec(shape, lambda i, j: (0,) * len(shape))
    return pl.pallas_call(
        _s5_kernel,
        grid=(b // S5_BATCH, l // S5_CHUNK),
        in_specs=[pl.BlockSpec((S5_BATCH, S5_CHUNK, LANES), lambda i, j: (i, j, COL_S5 // LANES)),
                  pl.BlockSpec((S5_BATCH, S5_CHUNK, LANES), lambda i, j: (i, j, COL_S5 // LANES + 1)),
                  const((BRANCH_W, n2)), const((S5_TABLE_ROWS, n2)), const((n2, BRANCH_W)),
                  const((1, BRANCH_W)), const((BRANCH_W, BRANCH_W)), const((1, BRANCH_W)),
                  const((1, BRANCH_W))],
        out_specs=pl.BlockSpec((S5_BATCH, S5_CHUNK, BRANCH_W), lambda i, j: (i, j, 0)),
        out_shape=jax.ShapeDtypeStruct((b, l, BRANCH_W), F32),
        scratch_shapes=[pltpu.VMEM((S5_BATCH, 1, S5_NSTATE), F32), pltpu.VMEM((S5_BATCH, 1, S5_NSTATE), F32),
                        pltpu.VMEM((S5_BATCH, BRANCH_W // LANES, S5_CHUNK, LANES), F32)],
        compiler_params=_cparams(("parallel", "arbitrary"), 40),
        name="s5_mixer",
    )(p3, p3, wb, pw, wc, d_skip.reshape(1, -1), w_glu.astype(BF16), b_glu.reshape(1, -1),
      out_norm.reshape(1, -1))


def _gmlp_kernel(p_ref, lng_ref, lnb_ref, w_ref, bias_ref, on_ref, o_ref):
    z = jax.nn.gelu(p_ref[...])
    u, v = z[:, :BRANCH_W], z[:, BRANCH_W:]
    vc = v - jnp.mean(v, axis=-1, keepdims=True)
    v = vc * lax.rsqrt(jnp.mean(vc * vc, axis=-1, keepdims=True) + EPS) * lng_ref[...] + lnb_ref[...]
    ti = lax.broadcasted_iota(jnp.int32, (GMLP_CHUNK, GMLP_CHUNK), 0)
    si = lax.broadcasted_iota(jnp.int32, (GMLP_CHUNK, GMLP_CHUNK), 1)
    tril = ti >= si
    ws = [jnp.where(tril, w_ref[h], 0.0).astype(BF16) for h in range(GMLP_HEADS)]
    lane = lax.broadcasted_iota(jnp.int32, (GMLP_CHUNK, BRANCH_W), 1)
    bias = bias_ref[...]
    outs = []
    for c in range(GM_TILE // GMLP_CHUNK):
        vb = v[c * GMLP_CHUNK:(c + 1) * GMLP_CHUNK, :].astype(BF16)
        s = jnp.dot(ws[GMLP_HEADS - 1], vb, preferred_element_type=F32)
        for h in range(GMLP_HEADS - 2, -1, -1):
            sh = jnp.dot(ws[h], vb, preferred_element_type=F32)
            s = jnp.where(lane < (h + 1) * GMLP_HEAD_DIM, sh, s)
        outs.append(s + bias)
    s = jnp.concatenate(outs, axis=0)
    o_ref[...] = _rms(u * s, on_ref[...])


def _gmlp_mixer(p3, ln_g, ln_b, w_sp, b_sp, out_norm):
    b, l, _ = p3.shape
    bias = jnp.repeat(b_sp.T, GMLP_HEAD_DIM, axis=1)
    const = lambda shape: pl.BlockSpec(shape, lambda i, j: (0,) * len(shape))
    return pl.pallas_call(
        _gmlp_kernel,
        grid=(b, l // GM_TILE),
        in_specs=[pl.BlockSpec((None, GM_TILE, 2 * BRANCH_W), lambda i, j: (i, j, COL_GM // (2 * BRANCH_W))),
                  const((1, BRANCH_W)), const((1, BRANCH_W)),
                  const((GMLP_HEADS, GMLP_CHUNK, GMLP_CHUNK)), const((GMLP_CHUNK, BRANCH_W)),
                  const((1, BRANCH_W))],
        out_specs=pl.BlockSpec((None, GM_TILE, BRANCH_W), lambda i, j: (i, j, 0)),
        out_shape=jax.ShapeDtypeStruct((b, l, BRANCH_W), F32),
        compiler_params=_cparams(("parallel", "parallel"), 32),
        name="gmlp_mixer",
    )(p3, ln_g.reshape(1, -1), ln_b.reshape(1, -1), w_sp, bias, out_norm.reshape(1, -1))


def _shortconv_kernel(b_ref, c_ref, x_ref, w_ref, on_ref, o_ref, halo_ref):
    @pl.when(pl.program_id(1) == 0)
    def _():
        halo_ref[...] = jnp.zeros_like(halo_ref)

    cx = c_ref[...] * x_ref[...]
    ext = jnp.concatenate([halo_ref[...], cx], axis=0)
    halo_ref[...] = cx[SC_TILE - SUBLANES:, :]
    y = w_ref[SC_CONV - 1:SC_CONV, :] * cx
    for j in range(SC_CONV - 1):
        sh = SC_CONV - 1 - j
        y = y + w_ref[j:j + 1, :] * pltpu.roll(ext, sh, 0)[SUBLANES:, :]
    o_ref[...] = _rms(b_ref[...] * y, on_ref[...])


def _shortconv_mixer(p3, conv_w, out_norm):
    b, l, _ = p3.shape
    w = jnp.concatenate([conv_w, jnp.zeros((SUBLANES - SC_CONV, BRANCH_W), F32)], axis=0)
    col = lambda c: pl.BlockSpec((None, SC_TILE, BRANCH_W), lambda i, j: (i, j, c // BRANCH_W))
    const = lambda shape: pl.BlockSpec(shape, lambda i, j: (0,) * len(shape))
    return pl.pallas_call(
        _shortconv_kernel,
        grid=(b, l // SC_TILE),
        in_specs=[col(COL_SCB), col(COL_SCC), col(COL_SCX), const((SUBLANES, BRANCH_W)), const((1, BRANCH_W))],
        out_specs=pl.BlockSpec((None, SC_TILE, BRANCH_W), lambda i, j: (i, j, 0)),
        out_shape=jax.ShapeDtypeStruct((b, l, BRANCH_W), F32),
        scratch_shapes=[pltpu.VMEM((SUBLANES, BRANCH_W), F32)],
        compiler_params=_cparams(("parallel", "arbitrary"), 32),
        name="shortconv_mixer",
    )(p3, p3, p3, w, out_norm.reshape(1, -1))


def _gdn_kernel(q_ref, k_ref, v_ref, z_ref, ab_ref, cw_ref, alog_ref, dtb_ref, ng_ref, o_ref,
                halo_ref, state_ref):
    c_len, h_dim, nh, w = GDN_CHUNK, GDN_HEAD_DIM, GDN_HEADS, BRANCH_W
    n_chunks = GDN_TILE // c_len

    @pl.when(pl.program_id(1) == 0)
    def _():
        halo_ref[...] = jnp.zeros_like(halo_ref)
        state_ref[...] = jnp.zeros_like(state_ref)

    shift = int(math.log2(h_dim))
    r256 = lax.broadcasted_iota(jnp.int32, (w, w), 0)
    c256 = lax.broadcasted_iota(jnp.int32, (w, w), 1)
    same_head = (r256 >> shift) == (c256 >> shift)
    ebd = jnp.where(same_head, 1.0, 0.0).astype(BF16)
    causal = same_head & (r256 >= c256)
    strict = same_head & (r256 > c256)
    eye = jnp.where(r256 == c256, 1.0, 0.0)
    r128 = lax.broadcasted_iota(jnp.int32, (LANES, w), 0)
    c128 = lax.broadcasted_iota(jnp.int32, (LANES, w), 1)
    exp_a = jnp.where(r128 == (c128 >> shift), 1.0, 0.0).astype(BF16)
    exp_b = jnp.where(r128 == (c128 >> shift) + AB_B_LANE, 1.0, 0.0).astype(BF16)

    def stack(a):
        lane_head = (lax.broadcasted_iota(jnp.int32, a.shape, 1) >> shift) & (nh - 1)
        return jnp.concatenate([jnp.where(lane_head == h, a, 0.0) for h in range(nh)], axis=0)

    def unstack(a):
        return a[0:c_len] + a[c_len:2 * c_len] + a[2 * c_len:3 * c_len] + a[3 * c_len:4 * c_len]

    def conv_silu(ref, bb, col):
        cur = ref[bb]
        ext = jnp.concatenate([halo_ref[bb, :, col * w:(col + 1) * w], cur], axis=0)
        halo_ref[bb, :, col * w:(col + 1) * w] = cur[GDN_TILE - SUBLANES:, :]
        y = cw_ref[GDN_CONV - 1:GDN_CONV, col * w:(col + 1) * w] * cur
        for j in range(GDN_CONV - 1):
            sh = GDN_CONV - 1 - j
            y = y + cw_ref[j:j + 1, col * w:(col + 1) * w] * pltpu.roll(ext, sh, 0)[SUBLANES:, :]
        return _silu(y)

    rt = lax.broadcasted_iota(jnp.int32, (GDN_TILE, GDN_TILE), 0)
    ct = lax.broadcasted_iota(jnp.int32, (GDN_TILE, GDN_TILE), 1)
    ltri = jnp.where(((rt >> shift) == (ct >> shift)) & (rt >= ct), 1.0, 0.0).astype(BF16)
    tile4 = lambda a: jnp.concatenate([a] * nh, axis=0)

    q_decs, k_decs, g_tots, rhss, a_mats, qkds = [], [], [], [], [], []
    for bb in range(GDN_BATCH):
        q = conv_silu(q_ref, bb, 0)
        k = conv_silu(k_ref, bb, 1)
        v = conv_silu(v_ref, bb, 2)
        q = q * lax.rsqrt(jnp.dot((q * q).astype(BF16), ebd, preferred_element_type=F32) + EPS) * (h_dim ** -0.5)
        k = k * lax.rsqrt(jnp.dot((k * k).astype(BF16), ebd, preferred_element_type=F32) + EPS)
        ab = ab_ref[bb]
        beta = jax.nn.sigmoid(ab)
        xa = ab + dtb_ref[...]
        softplus = jnp.maximum(xa, 0.0) + jnp.log(1.0 + jnp.exp(-jnp.abs(xa)))
        g = -jnp.exp(alog_ref[...]) * softplus
        gcum = _split_dot(ltri, g, 3, data_on_left=False)
        gexp = _split_dot(gcum, exp_a, 2, data_on_left=True)
        bexp = _split_dot(beta, exp_b, 2, data_on_left=True)
        for c in range(n_chunks):
            sl = slice(c * c_len, (c + 1) * c_len)
            qc, kc, vc, gc, bc = q[sl], k[sl], v[sl], gexp[sl], bexp[sl]
            glast = gc[c_len - 1:c_len, :]
            eg = jnp.exp(gc)
            q_decs.append(qc * eg)
            k_decs.append(kc * jnp.exp(glast - gc))
            g_tots.append(jnp.exp(glast))
            kst = stack(kc).astype(BF16)
            kk = _bdot_nt(kst, kst)
            qk = _bdot_nt(stack(qc), kst)
            gcol = stack(gc)
            bcol = stack(bc)
            dec = jnp.exp(jnp.where(causal, gcol - gcol.T, -jnp.inf))
            a_mats.append(jnp.where(strict, bcol * kk * dec, 0.0))
            qkds.append((qk * dec).astype(BF16))
            rhss.append(jnp.concatenate([tile4(vc) * bcol, tile4(kc * eg) * bcol], axis=1).astype(BF16))
    t_invs = [eye - a for a in a_mats]
    pws = a_mats
    for _ in range(int(math.log2(c_len)) - 1):
        pws = [_bdot(p, p) for p in pws]
        t_invs = [t + _bdot(t, p) for t, p in zip(t_invs, pws)]
    sols = [unstack(_bdot(t, r)) for t, r in zip(t_invs, rhss)]
    states = [state_ref[bb] for bb in range(GDN_BATCH)]
    outs = [[] for _ in range(GDN_BATCH)]
    for c in range(n_chunks):
        for bb in range(GDN_BATCH):
            i = bb * n_chunks + c
            w_v, w_k = sols[i][:, :w], sols[i][:, w:]
            v_new = w_v - _bdot(w_k, states[bb])
            o = _bdot(q_decs[i], states[bb]) + unstack(_bdot(qkds[i], stack(v_new)))
            states[bb] = states[bb] * g_tots[i] + jnp.where(same_head, _bdot(k_decs[i].T, v_new), 0.0)
            outs[bb].append(o)
    for bb in range(GDN_BATCH):
        state_ref[bb] = states[bb]
        o = jnp.concatenate(outs[bb], axis=0)
        ms = jnp.dot((o * o).astype(BF16), ebd, preferred_element_type=F32) * (1.0 / h_dim)
        o = o * lax.rsqrt(ms + EPS) * ng_ref[...]
        o_ref[bb] = o * _silu(z_ref[bb])


def _gdn_mixer(p3, conv_w, a_log, dt_bias, norm_g):
    b, l, _ = p3.shape
    cw = jnp.concatenate([conv_w, jnp.zeros((SUBLANES - GDN_CONV, 3 * BRANCH_W), F32)], axis=0)
    lane_row = lambda vec: jnp.zeros((1, LANES), F32).at[0, :GDN_HEADS].set(vec)
    col = lambda c: pl.BlockSpec((GDN_BATCH, GDN_TILE, BRANCH_W), lambda i, j: (i, j, c // BRANCH_W))
    const = lambda shape: pl.BlockSpec(shape, lambda i, j: (0,) * len(shape))
    return pl.pallas_call(
        _gdn_kernel,
        grid=(b // GDN_BATCH, l // GDN_TILE),
        in_specs=[col(COL_Q), col(COL_K), col(COL_V), col(COL_Z),
                  pl.BlockSpec((GDN_BATCH, GDN_TILE, LANES), lambda i, j: (i, j, COL_AB // LANES)),
                  const((SUBLANES, 3 * BRANCH_W)), const((1, LANES)), const((1, LANES)), const((1, BRANCH_W))],
        out_specs=pl.BlockSpec((GDN_BATCH, GDN_TILE, BRANCH_W), lambda i, j: (i, j, 0)),
        out_shape=jax.ShapeDtypeStruct((b, l, BRANCH_W), F32),
        scratch_shapes=[pltpu.VMEM((GDN_BATCH, SUBLANES, 3 * BRANCH_W), F32),
                        pltpu.VMEM((GDN_BATCH, BRANCH_W, BRANCH_W), F32)],
        compiler_params=_cparams(("parallel", "arbitrary"), 48),
        name="gdn_mixer",
    )(p3, p3, p3, p3, p3, cw, lane_row(a_log), lane_row(dt_bias), jnp.tile(norm_g, GDN_HEADS).reshape(1, -1))


def _mix_residual(x_ref, y_refs, wo_ref):
    acc = x_ref[...]
    for i, y_ref in enumerate(y_refs):
        acc = acc + _bdot(y_ref[...], wo_ref[i * BRANCH_W:(i + 1) * BRANCH_W, :])
    return acc


def _ffn_kernel(x_ref, y0_ref, y1_ref, y2_ref, y3_ref, wo_ref, g_ref, wg_ref, wu_ref, wd_ref, o_ref):
    x = _mix_residual(x_ref, (y0_ref, y1_ref, y2_ref, y3_ref), wo_ref)
    hb = _rms(x, g_ref[...]).astype(BF16)
    acc = None
    for f in range(D_FF // FFN_TF):
        cols = slice(f * FFN_TF, (f + 1) * FFN_TF)
        a = jnp.dot(hb, wg_ref[:, cols], preferred_element_type=F32)
        u = jnp.dot(hb, wu_ref[:, cols], preferred_element_type=F32)
        y = jnp.dot((_silu(a) * u).astype(BF16), wd_ref[cols, :], preferred_element_type=F32)
        acc = y if acc is None else acc + y
    o_ref[...] = x + acc


def _ffn_dense(x2, ys, w_out, g, w_gate, w_up, w_down):
    t = x2.shape[0]
    row = lambda n: pl.BlockSpec((FFN_TM, n), lambda i: (i, 0))
    resident = lambda shape: pl.BlockSpec(shape, lambda i: (0, 0), pipeline_mode=pl.Buffered(1))
    return pl.pallas_call(
        _ffn_kernel,
        grid=(t // FFN_TM,),
        in_specs=[row(D_MODEL)] + [row(BRANCH_W)] * 4 + [pl.BlockSpec((D_MODEL, D_MODEL), lambda i: (0, 0)),
                  pl.BlockSpec((1, D_MODEL), lambda i: (0, 0)),
                  resident((D_MODEL, D_FF)), resident((D_MODEL, D_FF)), resident((D_FF, D_MODEL))],
        out_specs=row(D_MODEL),
        out_shape=jax.ShapeDtypeStruct((t, D_MODEL), F32),
        compiler_params=_cparams(("parallel",), 52),
        name="ffn_dense",
    )(x2, *ys, w_out.astype(BF16), g.reshape(1, -1), w_gate.astype(BF16), w_up.astype(BF16), w_down.astype(BF16))


def _moe_kernel(be_ref, nu_ref, x_ref, g_ref, wg_ref, wu_ref, wd_ref, o_ref):
    del be_ref

    @pl.when(pl.program_id(0) < nu_ref[0])
    def _():
        xb = _rms(x_ref[...], g_ref[...]).astype(BF16)
        acc = None
        for f in range(D_FF_EXPERT // MOE_TF):
            cols = slice(f * MOE_TF, (f + 1) * MOE_TF)
            a = jnp.dot(xb, wg_ref[:, cols], preferred_element_type=F32)
            u = jnp.dot(xb, wu_ref[:, cols], preferred_element_type=F32)
            y = jnp.dot((_silu(a) * u).astype(BF16), wd_ref[cols, :], preferred_element_type=F32)
            acc = y if acc is None else acc + y
        o_ref[...] = acc

    @pl.when(pl.program_id(0) >= nu_ref[0])
    def _():
        o_ref[...] = jnp.zeros_like(o_ref)


def _moe_experts(buf, g, block_e, n_used, wg, wu, wd):
    rows = buf.shape[0]
    expert = lambda shape: pl.BlockSpec((None,) + shape, lambda m, be, nu: (be[m], 0, 0),
                                        pipeline_mode=pl.Buffered(1))
    grid_spec = pltpu.PrefetchScalarGridSpec(
        num_scalar_prefetch=2,
        grid=(rows // MOE_TM,),
        in_specs=[pl.BlockSpec((MOE_TM, D_MODEL), lambda m, be, nu: (jnp.minimum(m, nu[0] - 1), 0)),
                  pl.BlockSpec((1, D_MODEL), lambda m, be, nu: (0, 0)),
                  expert((D_MODEL, D_FF_EXPERT)), expert((D_MODEL, D_FF_EXPERT)), expert((D_FF_EXPERT, D_MODEL))],
        out_specs=pl.BlockSpec((MOE_TM, D_MODEL), lambda m, be, nu: (m, 0)),
    )
    return pl.pallas_call(
        _moe_kernel,
        grid_spec=grid_spec,
        out_shape=jax.ShapeDtypeStruct((rows, D_MODEL), F32),
        compiler_params=_cparams(("arbitrary",), 56),
        name="moe_experts",
    )(block_e, n_used, buf, g.reshape(1, -1), wg, wu, wd)


def _route_kernel(x_ref, y0_ref, y1_ref, y2_ref, y3_ref, wo_ref, g_ref, wt_ref,
                  xo_ref, e_ref, r_ref, gc_ref, cnt_ref, run_ref):
    tm = ROUTE_TM

    @pl.when(pl.program_id(0) == 0)
    def _():
        run_ref[...] = jnp.zeros_like(run_ref)

    x = _mix_residual(x_ref, (y0_ref, y1_ref, y2_ref, y3_ref), wo_ref)
    xo_ref[...] = x
    h = _rms(x, g_ref[...])
    lt = lax.dot_general(wt_ref[...], h, (((1,), (1,)), ((), ())), precision=HIGHEST,
                         preferred_element_type=F32)
    sub = lax.broadcasted_iota(jnp.int32, (N_EXPERTS, tm), 0)
    m1 = jnp.max(lt, axis=0, keepdims=True)
    i1 = jnp.min(jnp.where(lt == m1, sub, N_EXPERTS), axis=0, keepdims=True)
    lt2 = jnp.where(sub == i1, -jnp.inf, lt)
    m2 = jnp.max(lt2, axis=0, keepdims=True)
    i2 = jnp.min(jnp.where(lt2 == m2, sub, N_EXPERTS), axis=0, keepdims=True)
    oh0 = jnp.where(sub == i1, 1.0, 0.0)
    oh1 = jnp.where(sub == i2, 1.0, 0.0)
    cnt = oh0 + oh1
    ti = lax.broadcasted_iota(jnp.int32, (tm, tm), 0)
    tj = lax.broadcasted_iota(jnp.int32, (tm, tm), 1)
    upper = jnp.where(ti < tj, 1.0, 0.0).astype(BF16)
    pre = jnp.dot(cnt.astype(BF16), upper, preferred_element_type=F32) + run_ref[:, 0:1]
    r0 = jnp.sum(oh0 * pre, axis=0, keepdims=True)
    r1 = jnp.sum(oh1 * pre, axis=0, keepdims=True)
    e_ref[...] = jnp.concatenate([i1, i2], axis=0)
    r_ref[...] = jnp.concatenate([r0, r1], axis=0).astype(jnp.int32)
    run = run_ref[...] + jnp.sum(cnt, axis=1, keepdims=True)
    run_ref[...] = run
    cnt_ref[...] = run
    ex = jnp.exp(m2 - m1)
    g0 = 1.0 / (1.0 + ex)
    gates = jnp.where(sub == 0, g0, jnp.where(sub == 1, ex * g0, 0.0))
    er = lax.broadcasted_iota(jnp.int32, (N_EXPERTS, LANES), 0)
    ec = lax.broadcasted_iota(jnp.int32, (N_EXPERTS, LANES), 1)
    eye = jnp.where(er == ec, 1.0, 0.0).astype(BF16)
    acc = None
    for _ in range(3):
        piece = gates.astype(BF16)
        term = lax.dot_general(piece, eye, (((0,), (0,)), ((), ())), preferred_element_type=F32)
        acc = term if acc is None else acc + term
        gates = gates - piece.astype(F32)
    gc_ref[...] = acc


def _moe_route(x2, ys, w_out, ffn_norm, w_router):
    t = x2.shape[0]
    tm = ROUTE_TM
    row = lambda n: pl.BlockSpec((tm, n), lambda i: (i, 0))
    return pl.pallas_call(
        _route_kernel,
        grid=(t // tm,),
        in_specs=[row(D_MODEL)] + [row(BRANCH_W)] * 4 + [pl.BlockSpec((D_MODEL, D_MODEL), lambda i: (0, 0)),
                  pl.BlockSpec((1, D_MODEL), lambda i: (0, 0)),
                  pl.BlockSpec((N_EXPERTS, D_MODEL), lambda i: (0, 0))],
        out_specs=[row(D_MODEL),
                   pl.BlockSpec((TOP_K, tm), lambda i: (0, i)),
                   pl.BlockSpec((TOP_K, tm), lambda i: (0, i)),
                   row(LANES),
                   pl.BlockSpec((N_EXPERTS, LANES), lambda i: (0, 0))],
        out_shape=[jax.ShapeDtypeStruct((t, D_MODEL), F32),
                   jax.ShapeDtypeStruct((TOP_K, t), jnp.int32),
                   jax.ShapeDtypeStruct((TOP_K, t), jnp.int32),
                   jax.ShapeDtypeStruct((t, LANES), F32),
                   jax.ShapeDtypeStruct((N_EXPERTS, LANES), F32)],
        scratch_shapes=[pltpu.VMEM((N_EXPERTS, LANES), F32)],
        compiler_params=_cparams(("arbitrary",), 40),
        name="moe_route",
    )(x2, *ys, w_out.astype(BF16), ffn_norm.reshape(1, -1), w_router.T)


def _row_copy(src_ref, src_row, dst_ref, dst_row, sem):
    return pltpu.make_async_copy(src_ref.at[pl.ds(src_row, 1), :], dst_ref.at[pl.ds(dst_row, 1), :], sem)


def _dispatch_kernel(plo_ref, pn_ref, dest_hbm, x_ref, out_hbm, idx_ref, zero_ref, idx_sem, row_sem, pad_sem):
    tm = ROUTE_TM
    load = pltpu.make_async_copy(dest_hbm.at[pl.program_id(0)], idx_ref, idx_sem)
    load.start()

    @pl.when(pl.program_id(0) == 0)
    def _():
        zero_ref[...] = jnp.zeros_like(zero_ref)
        for e in range(N_EXPERTS):
            def fill(r, carry, e=e):
                _row_copy(zero_ref, 0, out_hbm, plo_ref[e] + r, pad_sem).start()
                return carry
            lax.fori_loop(0, pn_ref[e], fill, 0)
        for e in range(N_EXPERTS):
            def done(r, carry, e=e):
                _row_copy(zero_ref, 0, out_hbm, plo_ref[e] + r, pad_sem).wait()
                return carry
            lax.fori_loop(0, pn_ref[e], done, 0)
        first_free = (plo_ref[N_EXPERTS - 1] + pn_ref[N_EXPERTS - 1]) // MOE_TM

        def block_copy(b):
            return pltpu.make_async_copy(zero_ref, out_hbm.at[pl.ds(pl.multiple_of(b * MOE_TM, MOE_TM), MOE_TM), :],
                                         pad_sem)

        def fill_block(b, carry):
            block_copy(b).start()
            return carry

        def done_block(b, carry):
            block_copy(b).wait()
            return carry

        lax.fori_loop(first_free, out_hbm.shape[0] // MOE_TM, fill_block, 0)
        lax.fori_loop(first_free, out_hbm.shape[0] // MOE_TM, done_block, 0)

    load.wait()
    for t in range(tm):
        for k in range(TOP_K):
            _row_copy(x_ref, t, out_hbm, idx_ref[k * tm + t], row_sem).start(priority=k)
    for t in range(tm):
        for k in range(TOP_K):
            _row_copy(x_ref, t, out_hbm, idx_ref[k * tm + t], row_sem).wait()


def _moe_dispatch(x2, dest2, pad_lo, pad_n, n_rows):
    t = x2.shape[0]
    tm = ROUTE_TM
    grid_spec = pltpu.PrefetchScalarGridSpec(
        num_scalar_prefetch=2,
        grid=(t // tm,),
        in_specs=[pl.BlockSpec(memory_space=pl.ANY),
                  pl.BlockSpec((tm, D_MODEL), lambda i, lo, n: (i, 0))],
        out_specs=pl.BlockSpec(memory_space=pl.ANY),
        scratch_shapes=[pltpu.SMEM((TOP_K * tm,), jnp.int32), pltpu.VMEM((MOE_TM, D_MODEL), F32),
                        pltpu.SemaphoreType.DMA, pltpu.SemaphoreType.DMA, pltpu.SemaphoreType.DMA],
    )
    return pl.pallas_call(
        _dispatch_kernel,
        grid_spec=grid_spec,
        out_shape=jax.ShapeDtypeStruct((n_rows, D_MODEL), F32),
        compiler_params=_cparams(("arbitrary",), 32),
        name="moe_dispatch",
    )(pad_lo, pad_n, dest2, x2)


def _combine_kernel(dest_hbm, x_ref, gc_ref, fn_ref, y_hbm, o_ref, idx0_ref, idx1_ref, ybuf_ref, idx_sem, row_sem):
    tm = ROUTE_TM
    i, n = pl.program_id(0), pl.num_programs(0)
    idx_refs = (idx0_ref, idx1_ref)

    def gather(idx_ref, slot, t, k):
        return _row_copy(y_hbm, idx_ref[k * tm + t], ybuf_ref.at[slot, k], t, row_sem.at[slot])

    @pl.when(i == 0)
    def _():
        first = pltpu.make_async_copy(dest_hbm.at[0], idx0_ref, idx_sem.at[0])
        first.start()
        first.wait()

        def issue(t, carry):
            for k in range(TOP_K):
                gather(idx0_ref, 0, t, k).start()
            return carry

        lax.fori_loop(0, tm, issue, 0)

    def step(slot):
        nxt = 1 - slot
        load = pltpu.make_async_copy(dest_hbm.at[jnp.minimum(i + 1, n - 1)], idx_refs[nxt], idx_sem.at[nxt])
        load.start()
        for t in range(tm):
            for k in range(TOP_K):
                gather(idx_refs[slot], slot, t, k).wait()
        load.wait()
        for t in range(tm):
            for k in range(TOP_K):
                gather(idx_refs[nxt], nxt, t, k).start(priority=k)
        gc = gc_ref[...]
        x = x_ref[...] + gc[:, 0:1] * ybuf_ref[slot, 0] + gc[:, 1:2] * ybuf_ref[slot, 1]
        o_ref[...] = _rms(x, fn_ref[...])

        @pl.when(i == n - 1)
        def _():
            for t in range(tm):
                for k in range(TOP_K):
                    gather(idx_refs[nxt], nxt, t, k).wait()

    for slot in range(2):
        pl.when(i % 2 == slot)(functools.partial(step, slot))


def _moe_combine_norm(x2, y_buf, dest2, gcol, final_norm):
    t = x2.shape[0]
    tm = ROUTE_TM
    return pl.pallas_call(
        _combine_kernel,
        grid=(t // tm,),
        in_specs=[pl.BlockSpec(memory_space=pl.ANY),
                  pl.BlockSpec((tm, D_MODEL), lambda i: (i, 0)),
                  pl.BlockSpec((tm, LANES), lambda i: (i, 0)),
                  pl.BlockSpec((1, D_MODEL), lambda i: (0, 0)),
                  pl.BlockSpec(memory_space=pl.ANY)],
        out_specs=pl.BlockSpec((tm, D_MODEL), lambda i: (i, 0)),
        out_shape=jax.ShapeDtypeStruct((t, D_MODEL), F32),
        scratch_shapes=[pltpu.SMEM((TOP_K * tm,), jnp.int32), pltpu.SMEM((TOP_K * tm,), jnp.int32),
                        pltpu.VMEM((2, TOP_K, tm, D_MODEL), F32),
                        pltpu.SemaphoreType.DMA((2,)), pltpu.SemaphoreType.DMA((2,))],
        compiler_params=_cparams(("arbitrary",), 40),
        name="moe_combine_norm",
    )(dest2, x2, gcol, final_norm.reshape(1, -1), y_buf)


def _moe_layer(x2, ys, w_out, ffn_norm, w_router, w_gate, w_up, w_down, final_norm):
    t = x2.shape[0]
    tm = ROUTE_TM
    x2, e01, r01, gcol, cnt = _moe_route(x2, ys, w_out, ffn_norm, w_router)
    counts = cnt[:, 0].astype(jnp.int32)
    padded = (counts + MOE_TM - 1) // MOE_TM * MOE_TM
    pad_ends = jnp.cumsum(padded)
    pad_starts = pad_ends - padded
    n_blocks = -(-t * TOP_K // MOE_TM) + N_EXPERTS
    blk_start = jnp.arange(n_blocks, dtype=jnp.int32) * MOE_TM
    block_e = jnp.minimum(jnp.sum(blk_start[:, None] >= pad_ends[None, :], axis=1), N_EXPERTS - 1).astype(jnp.int32)
    n_used = (pad_ends[-1] // MOE_TM).astype(jnp.int32).reshape(1)
    dest = r01 + jnp.sum(jnp.where(e01[None] == jnp.arange(N_EXPERTS)[:, None, None], pad_starts[:, None, None], 0),
                         axis=0)
    dest2 = dest.reshape(TOP_K, t // tm, tm).transpose(1, 0, 2).reshape(t // tm, TOP_K * tm)
    xs = _moe_dispatch(x2, dest2, pad_starts + counts, padded - counts, n_blocks * MOE_TM)
    y_buf = _moe_experts(xs, ffn_norm, block_e, n_used, w_gate.astype(BF16), w_up.astype(BF16),
                         w_down.astype(BF16))
    return _moe_combine_norm(x2, y_buf, dest2, gcol, final_norm)


def _permute_w_in(w_in):
    s5 = w_in[:, 0:256]
    gm = w_in[:, 256:768]
    qkv = w_in[:, 768:1536]
    z = w_in[:, 1536:1792]
    a = w_in[:, 1792:1796]
    b = w_in[:, 1796:1800]
    sc = w_in[:, 1800:2568]
    ab = jnp.zeros((D_MODEL, LANES), F32).at[:, 0:GDN_HEADS].set(a).at[:, AB_B_LANE:AB_B_LANE + GDN_HEADS].set(b)
    return jnp.concatenate([gm, s5, z, qkv, sc, ab], axis=1).astype(BF16)


def kernel(x, mix_norm, w_in, s5_lam_re, s5_lam_im, s5_log_step, s5_b_re, s5_b_im, s5_c_re, s5_c_im, s5_d, s5_w_glu, s5_b_glu, s5_out_norm, sgu_ln_g, sgu_ln_b, sgu_w, sgu_b, gmlp_out_norm, gdn_conv, gdn_a_log, gdn_dt_bias, gdn_norm, sc_conv, sc_out_norm, w_out, ffn_norm, ffn_w_gate, ffn_w_up, ffn_w_down, moe_router, moe_w_gate, moe_w_up, moe_w_down, final_norm):
    bsz, seqlen, d = x.shape
    t = bsz * seqlen
    x2 = x.reshape(t, d)
    out = None
    for l in range(DEPTH):
        p2 = _in_proj(x2, mix_norm[l].reshape(1, -1), _permute_w_in(w_in[l]))
        p3 = p2.reshape(bsz, seqlen, P_COLS)
        wb, wc, pw = _s5_params(s5_lam_re[l], s5_lam_im[l], s5_log_step[l], s5_b_re[l], s5_b_im[l],
                                s5_c_re[l], s5_c_im[l])
        y_s5 = _s5_mixer(p3, wb, wc, pw, s5_d[l], s5_w_glu[l], s5_b_glu[l], s5_out_norm[l])
        y_gm = _gmlp_mixer(p3, sgu_ln_g[l], sgu_ln_b[l], sgu_w[l], sgu_b[l], gmlp_out_norm[l])
        y_gdn = _gdn_mixer(p3, gdn_conv[l], gdn_a_log[l], gdn_dt_bias[l], gdn_norm[l])
        y_sc = _shortconv_mixer(p3, sc_conv[l], sc_out_norm[l])
        ys = [y.reshape(t, BRANCH_W) for y in (y_s5, y_gm, y_gdn, y_sc)]
        i = l // 2
        if l % 2 == 0:
            x2 = _ffn_dense(x2, ys, w_out[l], ffn_norm[l], ffn_w_gate[i], ffn_w_up[i], ffn_w_down[i])
        else:
            out = _moe_layer(x2, ys, w_out[l], ffn_norm[l], moe_router[i], moe_w_gate[i], moe_w_up[i],
                             moe_w_down[i], final_norm)
    return out.reshape(bsz, seqlen, d)
```

```python
import functools
import math

import jax
import jax.numpy as jnp
from jax import lax
from jax.experimental import pallas as pl
from jax.experimental.pallas import tpu as pltpu

F32 = jnp.float32
BF16 = jnp.bfloat16
HIGHEST = lax.Precision.HIGHEST

D_MODEL = 1024
DEPTH = 2
BRANCH_W = 256
S5_GROUP = 16
S5_GROUPS = 16
S5_STATE = 64
S5_NSTATE = S5_GROUPS * S5_STATE
GMLP_HEADS = 4
GMLP_HEAD_DIM = 64
GMLP_CHUNK = 128
GDN_HEAD_DIM = 64
GDN_HEADS = 4
GDN_CONV = 4
GDN_CHUNK = 64
SC_CONV = 3
D_FF = 2816
N_EXPERTS = 8
TOP_K = 2
D_FF_EXPERT = 3584
EPS = 1e-6

LANES = 128
SUBLANES = 8
VMEM_BYTES_V7X = 64 * 1024 * 1024

COL_GM = 0
COL_S5 = 512
COL_Z = 768
COL_Q = 1024
COL_K = 1280
COL_V = 1536
COL_SCB = 1792
COL_SCC = 2048
COL_SCX = 2304
COL_AB = 2560
P_COLS = 2688
AB_B_LANE = 64

IN_TM = 512
S5_CHUNK = 128
S5_BATCH = 4
S5_SEG = S5_CHUNK // SUBLANES
S5_POW_ROW = 0
S5_SEG_ROW = S5_POW_ROW + S5_SEG
S5_CARRY_ROW = S5_SEG_ROW + 3 * SUBLANES
S5_TABLE_ROWS = S5_CARRY_ROW + SUBLANES
GM_TILE = 512
SC_TILE = 512
GDN_TILE = 128
GDN_BATCH = 4
OUT_TM = 512
FFN_TM = 512
FFN_TF = 256
MOE_TM = 512
MOE_TF = 512
ROUTE_TM = 512
ROW_TILE = D_MODEL // LANES


def _cparams(sem, vmem_mb):
    return pltpu.CompilerParams(dimension_semantics=sem, vmem_limit_bytes=vmem_mb * 1024 * 1024)


def _rms(x, g):
    return x * lax.rsqrt(jnp.mean(x * x, axis=-1, keepdims=True) + EPS) * g


def _silu(x):
    return x * jax.nn.sigmoid(x)


def _bdot(a, b):
    return jnp.dot(a.astype(BF16), b.astype(BF16), preferred_element_type=F32)


def _split_dot(a, b, passes, data_on_left):
    data = a if data_on_left else b
    acc = None
    for _ in range(passes):
        piece = data.astype(BF16)
        term = (jnp.dot(piece, b, preferred_element_type=F32) if data_on_left
                else jnp.dot(a, piece, preferred_element_type=F32))
        acc = term if acc is None else acc + term
        data = data - piece.astype(F32)
    return acc


def _bdot_nt(a, b):
    return lax.dot_general(a.astype(BF16), b.astype(BF16), (((1,), (1,)), ((), ())),
                           preferred_element_type=F32)


def _in_proj_kernel(x_ref, g_ref, w_ref, o_ref):
    h = _rms(x_ref[...], g_ref[...])
    o_ref[...] = _bdot(h, w_ref[...])


def _in_proj(x2, g, w):
    t = x2.shape[0]
    return pl.pallas_call(
        _in_proj_kernel,
        grid=(t // IN_TM,),
        in_specs=[pl.BlockSpec((IN_TM, D_MODEL), lambda i: (i, 0)),
                  pl.BlockSpec((1, D_MODEL), lambda i: (0, 0)),
                  pl.BlockSpec((D_MODEL, P_COLS), lambda i: (0, 0))],
        out_specs=pl.BlockSpec((IN_TM, P_COLS), lambda i: (i, 0)),
        out_shape=jax.ShapeDtypeStruct((t, P_COLS), F32),
        compiler_params=_cparams(("parallel",), 48),
        name="in_proj",
    )(x2, g, w)


def _s5_kernel(u0_ref, u1_ref, wb_ref, pw_ref, wc_ref, d_ref, wglu_ref, bglu_ref, on_ref, o_ref,
               sr_ref, si_ref, ys_ref):
    @pl.when(pl.program_id(1) == 0)
    def _():
        sr_ref[...] = jnp.zeros_like(sr_ref)
        si_ref[...] = jnp.zeros_like(si_ref)

    seg, c = S5_SEG, S5_CHUNK
    u = jnp.concatenate(
        [jnp.concatenate([jnp.concatenate([ref[bb, pl.ds(i, SUBLANES, stride=seg), :] for i in range(seg)], axis=0)
                          for ref in (u0_ref, u1_ref)], axis=1) for bb in range(S5_BATCH)], axis=0)
    bu = _bdot(u, wb_ref[...])
    xs = [_s5_scan(bu[bb * c:(bb + 1) * c], pw_ref, sr_ref.at[bb], si_ref.at[bb]) for bb in range(S5_BATCH)]
    x = jnp.concatenate(xs, axis=0)
    y = _bdot(x, wc_ref[...])
    y = y + d_ref[...] * u
    y = jax.nn.gelu(y)
    y = y * jax.nn.sigmoid(_bdot(y, wglu_ref[...]) + bglu_ref[...])
    y = _rms(y, on_ref[...])
    for bb in range(S5_BATCH):
        for slab in range(BRANCH_W // LANES):
            ys_ref[bb, slab] = y[bb * c:(bb + 1) * c, slab * LANES:(slab + 1) * LANES]
        for r in range(c // SUBLANES):
            start = (c // 2) * (r % 2) + r // 2
            for slab in range(BRANCH_W // LANES):
                o_ref[bb, r * SUBLANES:(r + 1) * SUBLANES, slab * LANES:(slab + 1) * LANES] = (
                    ys_ref[bb, slab, pl.ds(start, SUBLANES, stride=SUBLANES), :])


def _s5_scan(bu, pw_ref, sr_ref, si_ref):
    n, seg = S5_NSTATE, S5_SEG
    cmul = lambda ar, ai, xr, xi: (ar * xr - ai * xi, ar * xi + ai * xr)
    a_r, a_i = pw_ref[S5_POW_ROW:S5_POW_ROW + 1, :n], pw_ref[S5_POW_ROW:S5_POW_ROW + 1, n:]
    xr, xi = bu[0:SUBLANES, :n], bu[0:SUBLANES, n:]
    xrs, xis = [xr], [xi]
    for i in range(1, seg):
        rows = slice(i * SUBLANES, (i + 1) * SUBLANES)
        pr, pi = cmul(a_r, a_i, xr, xi)
        xr, xi = pr + bu[rows, :n], pi + bu[rows, n:]
        xrs.append(xr)
        xis.append(xi)
    fr, fi = xr, xi
    for s in range(int(math.log2(SUBLANES))):
        rows = slice(S5_SEG_ROW + s * SUBLANES, S5_SEG_ROW + (s + 1) * SUBLANES)
        pr, pi = cmul(pw_ref[rows, :n], pw_ref[rows, n:], pltpu.roll(fr, 1 << s, 0), pltpu.roll(fi, 1 << s, 0))
        fr, fi = fr + pr, fi + pi
    sbr = jnp.broadcast_to(sr_ref[...], (SUBLANES, n))
    sbi = jnp.broadcast_to(si_ref[...], (SUBLANES, n))
    rows = slice(S5_CARRY_ROW, S5_CARRY_ROW + SUBLANES)
    pr, pi = cmul(pw_ref[rows, :n], pw_ref[rows, n:], sbr, sbi)
    fr, fi = fr + pr, fi + pi
    sr_ref[...] = fr[SUBLANES - 1:, :]
    si_ref[...] = fi[SUBLANES - 1:, :]
    first = lax.broadcasted_iota(jnp.int32, (SUBLANES, n), 0) == 0
    cin_r = jnp.where(first, sbr, pltpu.roll(fr, 1, 0))
    cin_i = jnp.where(first, sbi, pltpu.roll(fi, 1, 0))
    for i in range(seg):
        row = S5_POW_ROW + i
        pr, pi = cmul(pw_ref[row:row + 1, :n], pw_ref[row:row + 1, n:], cin_r, cin_i)
        xrs[i], xis[i] = xrs[i] + pr, xis[i] + pi
    return jnp.concatenate([jnp.concatenate(xrs, axis=0), jnp.concatenate(xis, axis=0)], axis=1)


def _s5_params(lam_re, lam_im, log_step, b_re, b_im, c_re, c_im):
    g, n, p = S5_GROUPS, S5_STATE, S5_GROUP
    dt = jnp.exp(log_step)[:, None]
    mag = jnp.exp(lam_re * dt)
    ar, ai = mag * jnp.cos(lam_im * dt), mag * jnp.sin(lam_im * dt)
    den = lam_re * lam_re + lam_im * lam_im
    fr = ((ar - 1.0) * lam_re + ai * lam_im) / den
    fi = (ai * lam_re - (ar - 1.0) * lam_im) / den
    bbr = fr[..., None] * b_re - fi[..., None] * b_im
    bbi = fr[..., None] * b_im + fi[..., None] * b_re
    eye = jnp.eye(g, dtype=F32)
    wbr = jnp.einsum('gnp,gh->gphn', bbr, eye).reshape(g * p, g * n)
    wbi = jnp.einsum('gnp,gh->gphn', bbi, eye).reshape(g * p, g * n)
    wb = jnp.concatenate([wbr, wbi], axis=1)
    wcr = jnp.einsum('gpn,gh->gnhp', c_re, eye).reshape(g * n, g * p)
    wci = jnp.einsum('gpn,gh->gnhp', -c_im, eye).reshape(g * n, g * p)
    wc = jnp.concatenate([wcr, wci], axis=0)
    def powers(br, bi, count):
        rows_r, rows_i = [br], [bi]
        for _ in range(count - 1):
            qr, qi = rows_r[-1], rows_i[-1]
            rows_r.append(qr * br - qi * bi)
            rows_i.append(qr * bi + qi * br)
        return jnp.concatenate(rows_r, axis=0), jnp.concatenate(rows_i, axis=0)

    a1r, a1i = ar.reshape(1, g * n), ai.reshape(1, g * n)
    pos_r, pos_i = powers(a1r, a1i, S5_SEG)
    a16r, a16i = pos_r[S5_SEG - 1:], pos_i[S5_SEG - 1:]
    r_idx = jnp.arange(SUBLANES)[:, None]
    seg_r, seg_i = [], []
    pr, pi = a16r, a16i
    for s in range(int(math.log2(SUBLANES))):
        keep = r_idx >= (1 << s)
        seg_r.append(jnp.where(keep, pr, 0.0))
        seg_i.append(jnp.where(keep, pi, 0.0))
        pr, pi = pr * pr - pi * pi, 2.0 * pr * pi
    car_r, car_i = powers(a16r, a16i, SUBLANES)
    pw = jnp.concatenate([jnp.concatenate([pos_r] + seg_r + [car_r], axis=0),
                          jnp.concatenate([pos_i] + seg_i + [car_i], axis=0)], axis=1)
    return wb.astype(BF16), wc.astype(BF16), pw


def _s5_mixer(p3, wb, wc, pw, d_skip, w_glu, b_glu, out_norm):
    b, l, _ = p3.shape
    n2 = 2 * S5_NSTATE
    const = lambda shape: pl.BlockSpec(shape, lambda i, j: (0,) * len(shape))
    return pl.pallas_call(
        _s5_kernel,
        grid=(b // S5_BATCH, l // S5_CHUNK),
        in_specs=[pl.BlockSpec((S5_BATCH, S5_CHUNK, LANES), lambda i, j: (i, j, COL_S5 // LANES)),
                  pl.BlockSpec((S5_BATCH, S5_CHUNK, LANES), lambda i, j: (i, j, COL_S5 // LANES + 1)),
                  const((BRANCH_W, n2)), const((S5_TABLE_ROWS, n2)), const((n2, BRANCH_W)),
                  const((1, BRANCH_W)), const((BRANCH_W, BRANCH_W)), const((1, BRANCH_W)),
                  const((1, BRANCH_W))],
        out_specs=pl.BlockSpec((S5_BATCH, S5_CHUNK, BRANCH_W), lambda i, j: (i, j, 0)),
        out_shape=jax.ShapeDtypeStruct((b, l, BRANCH_W), F32),
        scratch_shapes=[pltpu.VMEM((S5_BATCH, 1, S5_NSTATE), F32), pltpu.VMEM((S5_BATCH, 1, S5_NSTATE), F32),
                        pltpu.VMEM((S5_BATCH, BRANCH_W // LANES, S5_CHUNK, LANES), F32)],
        compiler_params=_cparams(("parallel", "arbitrary"), 40),
        name="s5_mixer",
    )(p3, p3, wb, pw, wc, d_skip.reshape(1, -1), w_glu.astype(BF16), b_glu.reshape(1, -1),
      out_norm.reshape(1, -1))


def _gmlp_kernel(p_ref, lng_ref, lnb_ref, w_ref, bias_ref, on_ref, o_ref):
    z = jax.nn.gelu(p_ref[...])
    u, v = z[:, :BRANCH_W], z[:, BRANCH_W:]
    vc = v - jnp.mean(v, axis=-1, keepdims=True)
    v = vc * lax.rsqrt(jnp.mean(vc * vc, axis=-1, keepdims=True) + EPS) * lng_ref[...] + lnb_ref[...]
    ti = lax.broadcasted_iota(jnp.int32, (GMLP_CHUNK, GMLP_CHUNK), 0)
    si = lax.broadcasted_iota(jnp.int32, (GMLP_CHUNK, GMLP_CHUNK), 1)
    tril = ti >= si
    ws = [jnp.where(tril, w_ref[h], 0.0).astype(BF16) for h in range(GMLP_HEADS)]
    lane = lax.broadcasted_iota(jnp.int32, (GMLP_CHUNK, BRANCH_W), 1)
    bias = bias_ref[...]
    outs = []
    for c in range(GM_TILE // GMLP_CHUNK):
        vb = v[c * GMLP_CHUNK:(c + 1) * GMLP_CHUNK, :].astype(BF16)
        s = jnp.dot(ws[GMLP_HEADS - 1], vb, preferred_element_type=F32)
        for h in range(GMLP_HEADS - 2, -1, -1):
            sh = jnp.dot(ws[h], vb, preferred_element_type=F32)
            s = jnp.where(lane < (h + 1) * GMLP_HEAD_DIM, sh, s)
        outs.append(s + bias)
    s = jnp.concatenate(outs, axis=0)
    o_ref[...] = _rms(u * s, on_ref[...])


def _gmlp_mixer(p3, ln_g, ln_b, w_sp, b_sp, out_norm):
    b, l, _ = p3.shape
    bias = jnp.repeat(b_sp.T, GMLP_HEAD_DIM, axis=1)
    const = lambda shape: pl.BlockSpec(shape, lambda i, j: (0,) * len(shape))
    return pl.pallas_call(
        _gmlp_kernel,
        grid=(b, l // GM_TILE),
        in_specs=[pl.BlockSpec((None, GM_TILE, 2 * BRANCH_W), lambda i, j: (i, j, COL_GM // (2 * BRANCH_W))),
                  const((1, BRANCH_W)), const((1, BRANCH_W)),
                  const((GMLP_HEADS, GMLP_CHUNK, GMLP_CHUNK)), const((GMLP_CHUNK, BRANCH_W)),
                  const((1, BRANCH_W))],
        out_specs=pl.BlockSpec((None, GM_TILE, BRANCH_W), lambda i, j: (i, j, 0)),
        out_shape=jax.ShapeDtypeStruct((b, l, BRANCH_W), F32),
        compiler_params=_cparams(("parallel", "parallel"), 32),
        name="gmlp_mixer",
    )(p3, ln_g.reshape(1, -1), ln_b.reshape(1, -1), w_sp, bias, out_norm.reshape(1, -1))


def _shortconv_kernel(b_ref, c_ref, x_ref, w_ref, on_ref, o_ref, halo_ref):
    @pl.when(pl.program_id(1) == 0)
    def _():
        halo_ref[...] = jnp.zeros_like(halo_ref)

    cx = c_ref[...] * x_ref[...]
    ext = jnp.concatenate([halo_ref[...], cx], axis=0)
    halo_ref[...] = cx[SC_TILE - SUBLANES:, :]
    y = w_ref[SC_CONV - 1:SC_CONV, :] * cx
    for j in range(SC_CONV - 1):
        sh = SC_CONV - 1 - j
        y = y + w_ref[j:j + 1, :] * pltpu.roll(ext, sh, 0)[SUBLANES:, :]
    o_ref[...] = _rms(b_ref[...] * y, on_ref[...])


def _shortconv_mixer(p3, conv_w, out_norm):
    b, l, _ = p3.shape
    w = jnp.concatenate([conv_w, jnp.zeros((SUBLANES - SC_CONV, BRANCH_W), F32)], axis=0)
    col = lambda c: pl.BlockSpec((None, SC_TILE, BRANCH_W), lambda i, j: (i, j, c // BRANCH_W))
    const = lambda shape: pl.BlockSpec(shape, lambda i, j: (0,) * len(shape))
    return pl.pallas_call(
        _shortconv_kernel,
        grid=(b, l // SC_TILE),
        in_specs=[col(COL_SCB), col(COL_SCC), col(COL_SCX), const((SUBLANES, BRANCH_W)), const((1, BRANCH_W))],
        out_specs=pl.BlockSpec((None, SC_TILE, BRANCH_W), lambda i, j: (i, j, 0)),
        out_shape=jax.ShapeDtypeStruct((b, l, BRANCH_W), F32),
        scratch_shapes=[pltpu.VMEM((SUBLANES, BRANCH_W), F32)],
        compiler_params=_cparams(("parallel", "arbitrary"), 32),
        name="shortconv_mixer",
    )(p3, p3, p3, w, out_norm.reshape(1, -1))


def _gdn_kernel(q_ref, k_ref, v_ref, z_ref, ab_ref, cw_ref, alog_ref, dtb_ref, ng_ref, o_ref,
                halo_ref, state_ref):
    c_len, h_dim, nh, w = GDN_CHUNK, GDN_HEAD_DIM, GDN_HEADS, BRANCH_W
    n_chunks = GDN_TILE // c_len

    @pl.when(pl.program_id(1) == 0)
    def _():
        halo_ref[...] = jnp.zeros_like(halo_ref)
        state_ref[...] = jnp.zeros_like(state_ref)

    shift = int(math.log2(h_dim))
    r256 = lax.broadcasted_iota(jnp.int32, (w, w), 0)
    c256 = lax.broadcasted_iota(jnp.int32, (w, w), 1)
    same_head = (r256 >> shift) == (c256 >> shift)
    ebd = jnp.where(same_head, 1.0, 0.0).astype(BF16)
    causal = same_head & (r256 >= c256)
    strict = same_head & (r256 > c256)
    eye = jnp.where(r256 == c256, 1.0, 0.0)
    r128 = lax.broadcasted_iota(jnp.int32, (LANES, w), 0)
    c128 = lax.broadcasted_iota(jnp.int32, (LANES, w), 1)
    exp_a = jnp.where(r128 == (c128 >> shift), 1.0, 0.0).astype(BF16)
    exp_b = jnp.where(r128 == (c128 >> shift) + AB_B_LANE, 1.0, 0.0).astype(BF16)

    def stack(a):
        lane_head = (lax.broadcasted_iota(jnp.int32, a.shape, 1) >> shift) & (nh - 1)
        return jnp.concatenate([jnp.where(lane_head == h, a, 0.0) for h in range(nh)], axis=0)

    def unstack(a):
        return a[0:c_len] + a[c_len:2 * c_len] + a[2 * c_len:3 * c_len] + a[3 * c_len:4 * c_len]

    def conv_silu(ref, bb, col):
        cur = ref[bb]
        ext = jnp.concatenate([halo_ref[bb, :, col * w:(col + 1) * w], cur], axis=0)
        halo_ref[bb, :, col * w:(col + 1) * w] = cur[GDN_TILE - SUBLANES:, :]
        y = cw_ref[GDN_CONV - 1:GDN_CONV, col * w:(col + 1) * w] * cur
        for j in range(GDN_CONV - 1):
            sh = GDN_CONV - 1 - j
            y = y + cw_ref[j:j + 1, col * w:(col + 1) * w] * pltpu.roll(ext, sh, 0)[SUBLANES:, :]
        return _silu(y)

    rt = lax.broadcasted_iota(jnp.int32, (GDN_TILE, GDN_TILE), 0)
    ct = lax.broadcasted_iota(jnp.int32, (GDN_TILE, GDN_TILE), 1)
    ltri = jnp.where(((rt >> shift) == (ct >> shift)) & (rt >= ct), 1.0, 0.0).astype(BF16)
    tile4 = lambda a: jnp.concatenate([a] * nh, axis=0)

    q_decs, k_decs, g_tots, rhss, a_mats, qkds = [], [], [], [], [], []
    for bb in range(GDN_BATCH):
        q = conv_silu(q_ref, bb, 0)
        k = conv_silu(k_ref, bb, 1)
        v = conv_silu(v_ref, bb, 2)
        q = q * lax.rsqrt(jnp.dot((q * q).astype(BF16), ebd, preferred_element_type=F32) + EPS) * (h_dim ** -0.5)
        k = k * lax.rsqrt(jnp.dot((k * k).astype(BF16), ebd, preferred_element_type=F32) + EPS)
        ab = ab_ref[bb]
        beta = jax.nn.sigmoid(ab)
        xa = ab + dtb_ref[...]
        softplus = jnp.maximum(xa, 0.0) + jnp.log(1.0 + jnp.exp(-jnp.abs(xa)))
        g = -jnp.exp(alog_ref[...]) * softplus
        gcum = _split_dot(ltri, g, 3, data_on_left=False)
        gexp = _split_dot(gcum, exp_a, 2, data_on_left=True)
        bexp = _split_dot(beta, exp_b, 2, data_on_left=True)
        for c in range(n_chunks):
            sl = slice(c * c_len, (c + 1) * c_len)
            qc, kc, vc, gc, bc = q[sl], k[sl], v[sl], gexp[sl], bexp[sl]
            glast = gc[c_len - 1:c_len, :]
            eg = jnp.exp(gc)
            q_decs.append(qc * eg)
            k_decs.append(kc * jnp.exp(glast - gc))
            g_tots.append(jnp.exp(glast))
            kst = stack(kc).astype(BF16)
            kk = _bdot_nt(kst, kst)
            qk = _bdot_nt(stack(qc), kst)
            gcol = stack(gc)
            bcol = stack(bc)
            dec = jnp.exp(jnp.where(causal, gcol - gcol.T, -jnp.inf))
            a_mats.append(jnp.where(strict, bcol * kk * dec, 0.0))
            qkds.append((qk * dec).astype(BF16))
            rhss.append(jnp.concatenate([tile4(vc) * bcol, tile4(kc * eg) * bcol], axis=1).astype(BF16))
    t_invs = [eye - a for a in a_mats]
    pws = a_mats
    for _ in range(int(math.log2(c_len)) - 1):
        pws = [_bdot(p, p) for p in pws]
        t_invs = [t + _bdot(t, p) for t, p in zip(t_invs, pws)]
    sols = [unstack(_bdot(t, r)) for t, r in zip(t_invs, rhss)]
    states = [state_ref[bb] for bb in range(GDN_BATCH)]
    outs = [[] for _ in range(GDN_BATCH)]
    for c in range(n_chunks):
        for bb in range(GDN_BATCH):
            i = bb * n_chunks + c
            w_v, w_k = sols[i][:, :w], sols[i][:, w:]
            v_new = w_v - _bdot(w_k, states[bb])
            o = _bdot(q_decs[i], states[bb]) + unstack(_bdot(qkds[i], stack(v_new)))
            states[bb] = states[bb] * g_tots[i] + jnp.where(same_head, _bdot(k_decs[i].T, v_new), 0.0)
            outs[bb].append(o)
    for bb in range(GDN_BATCH):
        state_ref[bb] = states[bb]
        o = jnp.concatenate(outs[bb], axis=0)
        ms = jnp.dot((o * o).astype(BF16), ebd, preferred_element_type=F32) * (1.0 / h_dim)
        o = o * lax.rsqrt(ms + EPS) * ng_ref[...]
        o_ref[bb] = o * _silu(z_ref[bb])


def _gdn_mixer(p3, conv_w, a_log, dt_bias, norm_g):
    b, l, _ = p3.shape
    cw = jnp.concatenate([conv_w, jnp.zeros((SUBLANES - GDN_CONV, 3 * BRANCH_W), F32)], axis=0)
    lane_row = lambda vec: jnp.zeros((1, LANES), F32).at[0, :GDN_HEADS].set(vec)
    col = lambda c: pl.BlockSpec((GDN_BATCH, GDN_TILE, BRANCH_W), lambda i, j: (i, j, c // BRANCH_W))
    const = lambda shape: pl.BlockSpec(shape, lambda i, j: (0,) * len(shape))
    return pl.pallas_call(
        _gdn_kernel,
        grid=(b // GDN_BATCH, l // GDN_TILE),
        in_specs=[col(COL_Q), col(COL_K), col(COL_V), col(COL_Z),
                  pl.BlockSpec((GDN_BATCH, GDN_TILE, LANES), lambda i, j: (i, j, COL_AB // LANES)),
                  const((SUBLANES, 3 * BRANCH_W)), const((1, LANES)), const((1, LANES)), const((1, BRANCH_W))],
        out_specs=pl.BlockSpec((GDN_BATCH, GDN_TILE, BRANCH_W), lambda i, j: (i, j, 0)),
        out_shape=jax.ShapeDtypeStruct((b, l, BRANCH_W), F32),
        scratch_shapes=[pltpu.VMEM((GDN_BATCH, SUBLANES, 3 * BRANCH_W), F32),
                        pltpu.VMEM((GDN_BATCH, BRANCH_W, BRANCH_W), F32)],
        compiler_params=_cparams(("parallel", "arbitrary"), 48),
        name="gdn_mixer",
    )(p3, p3, p3, p3, p3, cw, lane_row(a_log), lane_row(dt_bias), jnp.tile(norm_g, GDN_HEADS).reshape(1, -1))


def _mix_residual(x_ref, y_refs, wo_ref):
    acc = x_ref[...]
    for i, y_ref in enumerate(y_refs):
        acc = acc + _bdot(y_ref[...], wo_ref[i * BRANCH_W:(i + 1) * BRANCH_W, :])
    return acc


def _ffn_kernel(x_ref, y0_ref, y1_ref, y2_ref, y3_ref, wo_ref, g_ref, wg_ref, wu_ref, wd_ref, o_ref):
    x = _mix_residual(x_ref, (y0_ref, y1_ref, y2_ref, y3_ref), wo_ref)
    hb = _rms(x, g_ref[...]).astype(BF16)
    acc = None
    for f in range(D_FF // FFN_TF):
        cols = slice(f * FFN_TF, (f + 1) * FFN_TF)
        a = jnp.dot(hb, wg_ref[:, cols], preferred_element_type=F32)
        u = jnp.dot(hb, wu_ref[:, cols], preferred_element_type=F32)
        y = jnp.dot((_silu(a) * u).astype(BF16), wd_ref[cols, :], preferred_element_type=F32)
        acc = y if acc is None else acc + y
    o_ref[...] = x + acc


def _ffn_dense(x2, ys, w_out, g, w_gate, w_up, w_down):
    t = x2.shape[0]
    row = lambda n: pl.BlockSpec((FFN_TM, n), lambda i: (i, 0))
    resident = lambda shape: pl.BlockSpec(shape, lambda i: (0, 0), pipeline_mode=pl.Buffered(1))
    return pl.pallas_call(
        _ffn_kernel,
        grid=(t // FFN_TM,),
        in_specs=[row(D_MODEL)] + [row(BRANCH_W)] * 4 + [pl.BlockSpec((D_MODEL, D_MODEL), lambda i: (0, 0)),
                  pl.BlockSpec((1, D_MODEL), lambda i: (0, 0)),
                  resident((D_MODEL, D_FF)), resident((D_MODEL, D_FF)), resident((D_FF, D_MODEL))],
        out_specs=row(D_MODEL),
        out_shape=jax.ShapeDtypeStruct((t, D_MODEL), F32),
        compiler_params=_cparams(("parallel",), 52),
        name="ffn_dense",
    )(x2, *ys, w_out.astype(BF16), g.reshape(1, -1), w_gate.astype(BF16), w_up.astype(BF16), w_down.astype(BF16))


def _tiles_to_rows(ref, m):
    return jnp.concatenate([ref[pl.ds(c, m, stride=ROW_TILE), :] for c in range(ROW_TILE)], axis=1)


def _rows_to_tiles(ref, value):
    for c in range(ROW_TILE):
        ref[pl.ds(c, value.shape[0], stride=ROW_TILE), :] = value[:, c * LANES:(c + 1) * LANES]


def _moe_kernel(be_ref, nu_ref, x_ref, g_ref, wg_ref, wu_ref, wd_ref, o_ref):
    del be_ref

    @pl.when(pl.program_id(0) < nu_ref[0])
    def _():
        xb = _rms(_tiles_to_rows(x_ref, MOE_TM), g_ref[...]).astype(BF16)
        acc = None
        for f in range(D_FF_EXPERT // MOE_TF):
            cols = slice(f * MOE_TF, (f + 1) * MOE_TF)
            a = jnp.dot(xb, wg_ref[:, cols], preferred_element_type=F32)
            u = jnp.dot(xb, wu_ref[:, cols], preferred_element_type=F32)
            y = jnp.dot((_silu(a) * u).astype(BF16), wd_ref[cols, :], preferred_element_type=F32)
            acc = y if acc is None else acc + y
        _rows_to_tiles(o_ref, acc)

    @pl.when(pl.program_id(0) >= nu_ref[0])
    def _():
        o_ref[...] = jnp.zeros_like(o_ref)


def _moe_experts(buf, g, block_e, n_used, wg, wu, wd):
    rows = buf.shape[0] // ROW_TILE
    expert = lambda shape: pl.BlockSpec((None,) + shape, lambda m, be, nu: (be[m], 0, 0),
                                        pipeline_mode=pl.Buffered(1))
    grid_spec = pltpu.PrefetchScalarGridSpec(
        num_scalar_prefetch=2,
        grid=(rows // MOE_TM,),
        in_specs=[pl.BlockSpec((MOE_TM * ROW_TILE, LANES), lambda m, be, nu: (jnp.minimum(m, nu[0] - 1), 0)),
                  pl.BlockSpec((1, D_MODEL), lambda m, be, nu: (0, 0)),
                  expert((D_MODEL, D_FF_EXPERT)), expert((D_MODEL, D_FF_EXPERT)), expert((D_FF_EXPERT, D_MODEL))],
        out_specs=pl.BlockSpec((MOE_TM * ROW_TILE, LANES), lambda m, be, nu: (m, 0)),
    )
    return pl.pallas_call(
        _moe_kernel,
        grid_spec=grid_spec,
        out_shape=jax.ShapeDtypeStruct((rows * ROW_TILE, LANES), F32),
        compiler_params=_cparams(("arbitrary",), 56),
        name="moe_experts",
    )(block_e, n_used, buf, g.reshape(1, -1), wg, wu, wd)


def _route_kernel(x_ref, y0_ref, y1_ref, y2_ref, y3_ref, wo_ref, g_ref, wt_ref,
                  xo_ref, e_ref, r_ref, gc_ref, cnt_ref, run_ref):
    tm = ROUTE_TM

    @pl.when(pl.program_id(0) == 0)
    def _():
        run_ref[...] = jnp.zeros_like(run_ref)

    x = _mix_residual(x_ref, (y0_ref, y1_ref, y2_ref, y3_ref), wo_ref)
    _rows_to_tiles(xo_ref, x)
    h = _rms(x, g_ref[...])
    lt = lax.dot_general(wt_ref[...], h, (((1,), (1,)), ((), ())), precision=HIGHEST,
                         preferred_element_type=F32)
    sub = lax.broadcasted_iota(jnp.int32, (N_EXPERTS, tm), 0)
    m1 = jnp.max(lt, axis=0, keepdims=True)
    i1 = jnp.min(jnp.where(lt == m1, sub, N_EXPERTS), axis=0, keepdims=True)
    lt2 = jnp.where(sub == i1, -jnp.inf, lt)
    m2 = jnp.max(lt2, axis=0, keepdims=True)
    i2 = jnp.min(jnp.where(lt2 == m2, sub, N_EXPERTS), axis=0, keepdims=True)
    oh0 = jnp.where(sub == i1, 1.0, 0.0)
    oh1 = jnp.where(sub == i2, 1.0, 0.0)
    cnt = oh0 + oh1
    ti = lax.broadcasted_iota(jnp.int32, (tm, tm), 0)
    tj = lax.broadcasted_iota(jnp.int32, (tm, tm), 1)
    upper = jnp.where(ti < tj, 1.0, 0.0).astype(BF16)
    pre = jnp.dot(cnt.astype(BF16), upper, preferred_element_type=F32) + run_ref[:, 0:1]
    r0 = jnp.sum(oh0 * pre, axis=0, keepdims=True)
    r1 = jnp.sum(oh1 * pre, axis=0, keepdims=True)
    e_ref[...] = jnp.concatenate([i1, i2], axis=0)
    r_ref[...] = jnp.concatenate([r0, r1], axis=0).astype(jnp.int32)
    run = run_ref[...] + jnp.sum(cnt, axis=1, keepdims=True)
    run_ref[...] = run
    cnt_ref[...] = run
    ex = jnp.exp(m2 - m1)
    g0 = 1.0 / (1.0 + ex)
    gates = jnp.where(sub == 0, g0, jnp.where(sub == 1, ex * g0, 0.0))
    er = lax.broadcasted_iota(jnp.int32, (N_EXPERTS, LANES), 0)
    ec = lax.broadcasted_iota(jnp.int32, (N_EXPERTS, LANES), 1)
    eye = jnp.where(er == ec, 1.0, 0.0).astype(BF16)
    acc = None
    for _ in range(3):
        piece = gates.astype(BF16)
        term = lax.dot_general(piece, eye, (((0,), (0,)), ((), ())), preferred_element_type=F32)
        acc = term if acc is None else acc + term
        gates = gates - piece.astype(F32)
    gc_ref[...] = acc


def _moe_route(x2, ys, w_out, ffn_norm, w_router):
    t = x2.shape[0]
    tm = ROUTE_TM
    row = lambda n: pl.BlockSpec((tm, n), lambda i: (i, 0))
    return pl.pallas_call(
        _route_kernel,
        grid=(t // tm,),
        in_specs=[row(D_MODEL)] + [row(BRANCH_W)] * 4 + [pl.BlockSpec((D_MODEL, D_MODEL), lambda i: (0, 0)),
                  pl.BlockSpec((1, D_MODEL), lambda i: (0, 0)),
                  pl.BlockSpec((N_EXPERTS, D_MODEL), lambda i: (0, 0))],
        out_specs=[pl.BlockSpec((tm * ROW_TILE, LANES), lambda i: (i, 0)),
                   pl.BlockSpec((TOP_K, tm), lambda i: (0, i)),
                   pl.BlockSpec((TOP_K, tm), lambda i: (0, i)),
                   row(LANES),
                   pl.BlockSpec((N_EXPERTS, LANES), lambda i: (0, 0))],
        out_shape=[jax.ShapeDtypeStruct((t * ROW_TILE, LANES), F32),
                   jax.ShapeDtypeStruct((TOP_K, t), jnp.int32),
                   jax.ShapeDtypeStruct((TOP_K, t), jnp.int32),
                   jax.ShapeDtypeStruct((t, LANES), F32),
                   jax.ShapeDtypeStruct((N_EXPERTS, LANES), F32)],
        scratch_shapes=[pltpu.VMEM((N_EXPERTS, LANES), F32)],
        compiler_params=_cparams(("arbitrary",), 40),
        name="moe_route",
    )(x2, *ys, w_out.astype(BF16), ffn_norm.reshape(1, -1), w_router.T)


def _row_copy(src_ref, src_row, dst_ref, dst_row, sem):
    tile = lambda row: pl.ds(row * ROW_TILE if isinstance(row, int) else pl.multiple_of(row * ROW_TILE, ROW_TILE),
                             ROW_TILE)
    return pltpu.make_async_copy(src_ref.at[tile(src_row), :], dst_ref.at[tile(dst_row), :], sem)


def _dispatch_kernel(plo_ref, pn_ref, dest_hbm, x_ref, out_hbm, idx_ref, zero_ref, idx_sem, row_sem, pad_sem):
    tm = ROUTE_TM
    load = pltpu.make_async_copy(dest_hbm.at[pl.program_id(0)], idx_ref, idx_sem)
    load.start()

    @pl.when(pl.program_id(0) == 0)
    def _():
        zero_ref[...] = jnp.zeros_like(zero_ref)
        for e in range(N_EXPERTS):
            def fill(r, carry, e=e):
                _row_copy(zero_ref, 0, out_hbm, plo_ref[e] + r, pad_sem).start()
                return carry
            lax.fori_loop(0, pn_ref[e], fill, 0)
        for e in range(N_EXPERTS):
            def done(r, carry, e=e):
                _row_copy(zero_ref, 0, out_hbm, plo_ref[e] + r, pad_sem).wait()
                return carry
            lax.fori_loop(0, pn_ref[e], done, 0)
        first_free = (plo_ref[N_EXPERTS - 1] + pn_ref[N_EXPERTS - 1]) // MOE_TM

        block_rows = MOE_TM * ROW_TILE

        def block_copy(b):
            return pltpu.make_async_copy(
                zero_ref, out_hbm.at[pl.ds(pl.multiple_of(b * block_rows, block_rows), block_rows), :], pad_sem)

        def fill_block(b, carry):
            block_copy(b).start()
            return carry

        def done_block(b, carry):
            block_copy(b).wait()
            return carry

        lax.fori_loop(first_free, out_hbm.shape[0] // block_rows, fill_block, 0)
        lax.fori_loop(first_free, out_hbm.shape[0] // block_rows, done_block, 0)

    load.wait()
    for t in range(tm):
        for k in range(TOP_K):
            _row_copy(x_ref, t, out_hbm, idx_ref[k * tm + t], row_sem).start(priority=k)
    for t in range(tm):
        for k in range(TOP_K):
            _row_copy(x_ref, t, out_hbm, idx_ref[k * tm + t], row_sem).wait()


def _moe_dispatch(xt, dest2, pad_lo, pad_n, n_rows):
    t = xt.shape[0] // ROW_TILE
    tm = ROUTE_TM
    grid_spec = pltpu.PrefetchScalarGridSpec(
        num_scalar_prefetch=2,
        grid=(t // tm,),
        in_specs=[pl.BlockSpec(memory_space=pl.ANY),
                  pl.BlockSpec((tm * ROW_TILE, LANES), lambda i, lo, n: (i, 0))],
        out_specs=pl.BlockSpec(memory_space=pl.ANY),
        scratch_shapes=[pltpu.SMEM((TOP_K * tm,), jnp.int32), pltpu.VMEM((MOE_TM * ROW_TILE, LANES), F32),
                        pltpu.SemaphoreType.DMA, pltpu.SemaphoreType.DMA, pltpu.SemaphoreType.DMA],
    )
    return pl.pallas_call(
        _dispatch_kernel,
        grid_spec=grid_spec,
        out_shape=jax.ShapeDtypeStruct((n_rows * ROW_TILE, LANES), F32),
        compiler_params=_cparams(("arbitrary",), 32),
        name="moe_dispatch",
    )(pad_lo, pad_n, dest2, xt)


def _combine_kernel(dest_hbm, x_ref, gc_ref, fn_ref, y_hbm, o_ref, idx0_ref, idx1_ref, ybuf_ref, idx_sem, row_sem):
    tm = ROUTE_TM
    i, n = pl.program_id(0), pl.num_programs(0)
    idx_refs = (idx0_ref, idx1_ref)

    def gather(idx_ref, slot, t, k):
        return _row_copy(y_hbm, idx_ref[k * tm + t], ybuf_ref.at[slot, k], t, row_sem.at[slot])

    @pl.when(i == 0)
    def _():
        first = pltpu.make_async_copy(dest_hbm.at[0], idx0_ref, idx_sem.at[0])
        first.start()
        first.wait()

        def issue(t, carry):
            for k in range(TOP_K):
                gather(idx0_ref, 0, t, k).start()
            return carry

        lax.fori_loop(0, tm, issue, 0)

    def step(slot):
        nxt = 1 - slot
        load = pltpu.make_async_copy(dest_hbm.at[jnp.minimum(i + 1, n - 1)], idx_refs[nxt], idx_sem.at[nxt])
        load.start()
        for t in range(tm):
            for k in range(TOP_K):
                gather(idx_refs[slot], slot, t, k).wait()
        load.wait()
        for t in range(tm):
            for k in range(TOP_K):
                gather(idx_refs[nxt], nxt, t, k).start(priority=k)
        gc = gc_ref[...]
        x = (_tiles_to_rows(x_ref, tm) + gc[:, 0:1] * _tiles_to_rows(ybuf_ref.at[slot, 0], tm)
             + gc[:, 1:2] * _tiles_to_rows(ybuf_ref.at[slot, 1], tm))
        o_ref[...] = _rms(x, fn_ref[...])

        @pl.when(i == n - 1)
        def _():
            for t in range(tm):
                for k in range(TOP_K):
                    gather(idx_refs[nxt], nxt, t, k).wait()

    for slot in range(2):
        pl.when(i % 2 == slot)(functools.partial(step, slot))


def _moe_combine_norm(xt, y_buf, dest2, gcol, final_norm):
    t = xt.shape[0] // ROW_TILE
    tm = ROUTE_TM
    return pl.pallas_call(
        _combine_kernel,
        grid=(t // tm,),
        in_specs=[pl.BlockSpec(memory_space=pl.ANY),
                  pl.BlockSpec((tm * ROW_TILE, LANES), lambda i: (i, 0)),
                  pl.BlockSpec((tm, LANES), lambda i: (i, 0)),
                  pl.BlockSpec((1, D_MODEL), lambda i: (0, 0)),
                  pl.BlockSpec(memory_space=pl.ANY)],
        out_specs=pl.BlockSpec((tm, D_MODEL), lambda i: (i, 0)),
        out_shape=jax.ShapeDtypeStruct((t, D_MODEL), F32),
        scratch_shapes=[pltpu.SMEM((TOP_K * tm,), jnp.int32), pltpu.SMEM((TOP_K * tm,), jnp.int32),
                        pltpu.VMEM((2, TOP_K, tm * ROW_TILE, LANES), F32),
                        pltpu.SemaphoreType.DMA((2,)), pltpu.SemaphoreType.DMA((2,))],
        compiler_params=_cparams(("arbitrary",), 40),
        name="moe_combine_norm",
    )(dest2, xt, gcol, final_norm.reshape(1, -1), y_buf)


def _moe_layer(x2, ys, w_out, ffn_norm, w_router, w_gate, w_up, w_down, final_norm):
    t = x2.shape[0]
    tm = ROUTE_TM
    xt, e01, r01, gcol, cnt = _moe_route(x2, ys, w_out, ffn_norm, w_router)
    counts = cnt[:, 0].astype(jnp.int32)
    padded = (counts + MOE_TM - 1) // MOE_TM * MOE_TM
    pad_ends = jnp.cumsum(padded)
    pad_starts = pad_ends - padded
    n_blocks = -(-t * TOP_K // MOE_TM) + N_EXPERTS
    blk_start = jnp.arange(n_blocks, dtype=jnp.int32) * MOE_TM
    block_e = jnp.minimum(jnp.sum(blk_start[:, None] >= pad_ends[None, :], axis=1), N_EXPERTS - 1).astype(jnp.int32)
    n_used = (pad_ends[-1] // MOE_TM).astype(jnp.int32).reshape(1)
    dest = r01 + jnp.sum(jnp.where(e01[None] == jnp.arange(N_EXPERTS)[:, None, None], pad_starts[:, None, None], 0),
                         axis=0)
    dest2 = dest.reshape(TOP_K, t // tm, tm).transpose(1, 0, 2).reshape(t // tm, TOP_K * tm)
    xs = _moe_dispatch(xt, dest2, pad_starts + counts, padded - counts, n_blocks * MOE_TM)
    y_buf = _moe_experts(xs, ffn_norm, block_e, n_used, w_gate.astype(BF16), w_up.astype(BF16),
                         w_down.astype(BF16))
    return _moe_combine_norm(xt, y_buf, dest2, gcol, final_norm)


def _permute_w_in(w_in):
    s5 = w_in[:, 0:256]
    gm = w_in[:, 256:768]
    qkv = w_in[:, 768:1536]
    z = w_in[:, 1536:1792]
    a = w_in[:, 1792:1796]
    b = w_in[:, 1796:1800]
    sc = w_in[:, 1800:2568]
    ab = jnp.zeros((D_MODEL, LANES), F32).at[:, 0:GDN_HEADS].set(a).at[:, AB_B_LANE:AB_B_LANE + GDN_HEADS].set(b)
    return jnp.concatenate([gm, s5, z, qkv, sc, ab], axis=1).astype(BF16)


def kernel(x, mix_norm, w_in, s5_lam_re, s5_lam_im, s5_log_step, s5_b_re, s5_b_im, s5_c_re, s5_c_im, s5_d, s5_w_glu, s5_b_glu, s5_out_norm, sgu_ln_g, sgu_ln_b, sgu_w, sgu_b, gmlp_out_norm, gdn_conv, gdn_a_log, gdn_dt_bias, gdn_norm, sc_conv, sc_out_norm, w_out, ffn_norm, ffn_w_gate, ffn_w_up, ffn_w_down, moe_router, moe_w_gate, moe_w_up, moe_w_down, final_norm):
    bsz, seqlen, d = x.shape
    t = bsz * seqlen
    x2 = x.reshape(t, d)
    out = None
    for l in range(DEPTH):
        p2 = _in_proj(x2, mix_norm[l].reshape(1, -1), _permute_w_in(w_in[l]))
        p3 = p2.reshape(bsz, seqlen, P_COLS)
        wb, wc, pw = _s5_params(s5_lam_re[l], s5_lam_im[l], s5_log_step[l], s5_b_re[l], s5_b_im[l],
                                s5_c_re[l], s5_c_im[l])
        y_s5 = _s5_mixer(p3, wb, wc, pw, s5_d[l], s5_w_glu[l], s5_b_glu[l], s5_out_norm[l])
        y_gm = _gmlp_mixer(p3, sgu_ln_g[l], sgu_ln_b[l], sgu_w[l], sgu_b[l], gmlp_out_norm[l])
        y_gdn = _gdn_mixer(p3, gdn_conv[l], gdn_a_log[l], gdn_dt_bias[l], gdn_norm[l])
        y_sc = _shortconv_mixer(p3, sc_conv[l], sc_out_norm[l])
        ys = [y.reshape(t, BRANCH_W) for y in (y_s5, y_gm, y_gdn, y_sc)]
        i = l // 2
        if l % 2 == 0:
            x2 = _ffn_dense(x2, ys, w_out[l], ffn_norm[l], ffn_w_gate[i], ffn_w_up[i], ffn_w_down[i])
        else:
            out = _moe_layer(x2, ys, w_out[l], ffn_norm[l], moe_router[i], moe_w_gate[i], moe_w_up[i],
                             moe_w_down[i], final_norm)
    return out.reshape(bsz, seqlen, d)
```

```python
import functools
import math

import jax
import jax.numpy as jnp
from jax import lax
from jax.experimental import pallas as pl
from jax.experimental.pallas import tpu as pltpu

F32 = jnp.float32
BF16 = jnp.bfloat16

D_MODEL = 1024
DEPTH = 2
BRANCH_W = 256
S5_GROUP = 16
S5_GROUPS = 16
S5_STATE = 64
S5_NSTATE = S5_GROUPS * S5_STATE
GMLP_HEADS = 4
GMLP_HEAD_DIM = 64
GMLP_CHUNK = 128
GDN_HEAD_DIM = 64
GDN_HEADS = 4
GDN_CONV = 4
GDN_CHUNK = 64
SC_CONV = 3
D_FF = 2816
N_EXPERTS = 8
TOP_K = 2
D_FF_EXPERT = 3584
EPS = 1e-6

LANES = 128
SUBLANES = 8

COL_GM = 0
COL_S5 = 512
COL_Z = 768
COL_Q = 1024
COL_K = 1280
COL_V = 1536
COL_SCB = 1792
COL_SCC = 2048
COL_SCX = 2304
COL_AB = 2560
P_COLS = 2688
AB_B_LANE = 64

IN_TM = 512
S5_CHUNK = 128
S5_BATCH = 4
S5_SEG = S5_CHUNK // SUBLANES
S5_POW_ROW = 0
S5_SEG_ROW = S5_POW_ROW + S5_SEG
S5_CARRY_ROW = S5_SEG_ROW + 3 * SUBLANES
S5_TABLE_ROWS = S5_CARRY_ROW + SUBLANES
GM_TILE = 512
SC_TILE = 512
GDN_TILE = 128
GDN_BATCH = 4
FFN_TM = 512
FFN_TF = 256
MOE_TM = 512
MOE_TF = 512
ROUTE_TM = 512
ROW_TILE = D_MODEL // LANES


def _cparams(sem, vmem_mb):
    return pltpu.CompilerParams(dimension_semantics=sem, vmem_limit_bytes=vmem_mb * 1024 * 1024)


def _rms(x, g):
    return x * lax.rsqrt(jnp.mean(x * x, axis=-1, keepdims=True) + EPS) * g


def _silu(x):
    return x * jax.nn.sigmoid(x)


def _bdot(a, b):
    return jnp.dot(a.astype(BF16), b.astype(BF16), preferred_element_type=F32)


def _split_dot(a, b, passes, data_on_left):
    data = a if data_on_left else b
    acc = None
    for _ in range(passes):
        piece = data.astype(BF16)
        term = (jnp.dot(piece, b, preferred_element_type=F32) if data_on_left
                else jnp.dot(a, piece, preferred_element_type=F32))
        acc = term if acc is None else acc + term
        data = data - piece.astype(F32)
    return acc


def _bdot_nt(a, b):
    return lax.dot_general(a.astype(BF16), b.astype(BF16), (((1,), (1,)), ((), ())),
                           preferred_element_type=F32)


def _in_proj_kernel(x_ref, g_ref, w_ref, o_ref):
    h = _rms(x_ref[...], g_ref[...])
    o_ref[...] = _bdot(h, w_ref[...])


def _in_proj(x2, g, w):
    t = x2.shape[0]
    return pl.pallas_call(
        _in_proj_kernel,
        grid=(t // IN_TM,),
        in_specs=[pl.BlockSpec((IN_TM, D_MODEL), lambda i: (i, 0)),
                  pl.BlockSpec((1, D_MODEL), lambda i: (0, 0)),
                  pl.BlockSpec((D_MODEL, P_COLS), lambda i: (0, 0))],
        out_specs=pl.BlockSpec((IN_TM, P_COLS), lambda i: (i, 0)),
        out_shape=jax.ShapeDtypeStruct((t, P_COLS), F32),
        compiler_params=_cparams(("parallel",), 48),
        name="in_proj",
    )(x2, g, w)


def _s5_kernel(u0_ref, u1_ref, wb_ref, pw_ref, wc_ref, d_ref, wglu_ref, bglu_ref, on_ref, o_ref,
               sr_ref, si_ref, ys_ref):
    @pl.when(pl.program_id(1) == 0)
    def _():
        sr_ref[...] = jnp.zeros_like(sr_ref)
        si_ref[...] = jnp.zeros_like(si_ref)

    seg, c = S5_SEG, S5_CHUNK
    u = jnp.concatenate(
        [jnp.concatenate([jnp.concatenate([ref[bb, pl.ds(i, SUBLANES, stride=seg), :] for i in range(seg)], axis=0)
                          for ref in (u0_ref, u1_ref)], axis=1) for bb in range(S5_BATCH)], axis=0)
    bu = _bdot(u, wb_ref[...])
    xs = [_s5_scan(bu[bb * c:(bb + 1) * c], pw_ref, sr_ref.at[bb], si_ref.at[bb]) for bb in range(S5_BATCH)]
    x = jnp.concatenate(xs, axis=0)
    y = _bdot(x, wc_ref[...])
    y = y + d_ref[...] * u
    y = jax.nn.gelu(y)
    y = y * jax.nn.sigmoid(_bdot(y, wglu_ref[...]) + bglu_ref[...])
    y = _rms(y, on_ref[...])
    for bb in range(S5_BATCH):
        for slab in range(BRANCH_W // LANES):
            ys_ref[bb, slab] = y[bb * c:(bb + 1) * c, slab * LANES:(slab + 1) * LANES]
        for r in range(c // SUBLANES):
            start = (c // 2) * (r % 2) + r // 2
            for slab in range(BRANCH_W // LANES):
                o_ref[bb, r * SUBLANES:(r + 1) * SUBLANES, slab * LANES:(slab + 1) * LANES] = (
                    ys_ref[bb, slab, pl.ds(start, SUBLANES, stride=SUBLANES), :])


def _s5_scan(bu, pw_ref, sr_ref, si_ref):
    n, seg = S5_NSTATE, S5_SEG
    cmul = lambda ar, ai, xr, xi: (ar * xr - ai * xi, ar * xi + ai * xr)
    a_r, a_i = pw_ref[S5_POW_ROW:S5_POW_ROW + 1, :n], pw_ref[S5_POW_ROW:S5_POW_ROW + 1, n:]
    xr, xi = bu[0:SUBLANES, :n], bu[0:SUBLANES, n:]
    xrs, xis = [xr], [xi]
    for i in range(1, seg):
        rows = slice(i * SUBLANES, (i + 1) * SUBLANES)
        pr, pi = cmul(a_r, a_i, xr, xi)
        xr, xi = pr + bu[rows, :n], pi + bu[rows, n:]
        xrs.append(xr)
        xis.append(xi)
    fr, fi = xr, xi
    for s in range(int(math.log2(SUBLANES))):
        rows = slice(S5_SEG_ROW + s * SUBLANES, S5_SEG_ROW + (s + 1) * SUBLANES)
        pr, pi = cmul(pw_ref[rows, :n], pw_ref[rows, n:], pltpu.roll(fr, 1 << s, 0), pltpu.roll(fi, 1 << s, 0))
        fr, fi = fr + pr, fi + pi
    sbr = jnp.broadcast_to(sr_ref[...], (SUBLANES, n))
    sbi = jnp.broadcast_to(si_ref[...], (SUBLANES, n))
    rows = slice(S5_CARRY_ROW, S5_CARRY_ROW + SUBLANES)
    pr, pi = cmul(pw_ref[rows, :n], pw_ref[rows, n:], sbr, sbi)
    fr, fi = fr + pr, fi + pi
    sr_ref[...] = fr[SUBLANES - 1:, :]
    si_ref[...] = fi[SUBLANES - 1:, :]
    first = lax.broadcasted_iota(jnp.int32, (SUBLANES, n), 0) == 0
    cin_r = jnp.where(first, sbr, pltpu.roll(fr, 1, 0))
    cin_i = jnp.where(first, sbi, pltpu.roll(fi, 1, 0))
    for i in range(seg):
        row = S5_POW_ROW + i
        pr, pi = cmul(pw_ref[row:row + 1, :n], pw_ref[row:row + 1, n:], cin_r, cin_i)
        xrs[i], xis[i] = xrs[i] + pr, xis[i] + pi
    return jnp.concatenate([jnp.concatenate(xrs, axis=0), jnp.concatenate(xis, axis=0)], axis=1)


def _s5_params(lam_re, lam_im, log_step, b_re, b_im, c_re, c_im):
    g, n, p = S5_GROUPS, S5_STATE, S5_GROUP
    dt = jnp.exp(log_step)[:, None]
    mag = jnp.exp(lam_re * dt)
    ar, ai = mag * jnp.cos(lam_im * dt), mag * jnp.sin(lam_im * dt)
    den = lam_re * lam_re + lam_im * lam_im
    fr = ((ar - 1.0) * lam_re + ai * lam_im) / den
    fi = (ai * lam_re - (ar - 1.0) * lam_im) / den
    bbr = fr[..., None] * b_re - fi[..., None] * b_im
    bbi = fr[..., None] * b_im + fi[..., None] * b_re
    eye = jnp.eye(g, dtype=F32)
    wbr = jnp.einsum('gnp,gh->gphn', bbr, eye).reshape(g * p, g * n)
    wbi = jnp.einsum('gnp,gh->gphn', bbi, eye).reshape(g * p, g * n)
    wb = jnp.concatenate([wbr, wbi], axis=1)
    wcr = jnp.einsum('gpn,gh->gnhp', c_re, eye).reshape(g * n, g * p)
    wci = jnp.einsum('gpn,gh->gnhp', -c_im, eye).reshape(g * n, g * p)
    wc = jnp.concatenate([wcr, wci], axis=0)
    def powers(br, bi, count):
        rows_r, rows_i = [br], [bi]
        for _ in range(count - 1):
            qr, qi = rows_r[-1], rows_i[-1]
            rows_r.append(qr * br - qi * bi)
            rows_i.append(qr * bi + qi * br)
        return jnp.concatenate(rows_r, axis=0), jnp.concatenate(rows_i, axis=0)

    a1r, a1i = ar.reshape(1, g * n), ai.reshape(1, g * n)
    pos_r, pos_i = powers(a1r, a1i, S5_SEG)
    a16r, a16i = pos_r[S5_SEG - 1:], pos_i[S5_SEG - 1:]
    r_idx = jnp.arange(SUBLANES)[:, None]
    seg_r, seg_i = [], []
    pr, pi = a16r, a16i
    for s in range(int(math.log2(SUBLANES))):
        keep = r_idx >= (1 << s)
        seg_r.append(jnp.where(keep, pr, 0.0))
        seg_i.append(jnp.where(keep, pi, 0.0))
        pr, pi = pr * pr - pi * pi, 2.0 * pr * pi
    car_r, car_i = powers(a16r, a16i, SUBLANES)
    pw = jnp.concatenate([jnp.concatenate([pos_r] + seg_r + [car_r], axis=0),
                          jnp.concatenate([pos_i] + seg_i + [car_i], axis=0)], axis=1)
    return wb.astype(BF16), wc.astype(BF16), pw


def _s5_mixer(p3, wb, wc, pw, d_skip, w_glu, b_glu, out_norm):
    b, l, _ = p3.shape
    n2 = 2 * S5_NSTATE
    const = lambda shape: pl.BlockSpec(shape, lambda i, j: (0,) * len(shape))
    return pl.pallas_call(
        _s5_kernel,
        grid=(b // S5_BATCH, l // S5_CHUNK),
        in_specs=[pl.BlockSpec((S5_BATCH, S5_CHUNK, LANES), lambda i, j: (i, j, COL_S5 // LANES)),
                  pl.BlockSpec((S5_BATCH, S5_CHUNK, LANES), lambda i, j: (i, j, COL_S5 // LANES + 1)),
                  const((BRANCH_W, n2)), const((S5_TABLE_ROWS, n2)), const((n2, BRANCH_W)),
                  const((1, BRANCH_W)), const((BRANCH_W, BRANCH_W)), const((1, BRANCH_W)),
                  const((1, BRANCH_W))],
        out_specs=pl.BlockSpec((S5_BATCH, S5_CHUNK, BRANCH_W), lambda i, j: (i, j, 0)),
        out_shape=jax.ShapeDtypeStruct((b, l, BRANCH_W), F32),
        scratch_shapes=[pltpu.VMEM((S5_BATCH, 1, S5_NSTATE), F32), pltpu.VMEM((S5_BATCH, 1, S5_NSTATE), F32),
                        pltpu.VMEM((S5_BATCH, BRANCH_W // LANES, S5_CHUNK, LANES), F32)],
        compiler_params=_cparams(("parallel", "arbitrary"), 40),
        name="s5_mixer",
    )(p3, p3, wb, pw, wc, d_skip.reshape(1, -1), w_glu.astype(BF16), b_glu.reshape(1, -1),
      out_norm.reshape(1, -1))


def _gmlp_kernel(p_ref, lng_ref, lnb_ref, w_ref, bias_ref, on_ref, o_ref):
    z = jax.nn.gelu(p_ref[...])
    u, v = z[:, :BRANCH_W], z[:, BRANCH_W:]
    vc = v - jnp.mean(v, axis=-1, keepdims=True)
    v = vc * lax.rsqrt(jnp.mean(vc * vc, axis=-1, keepdims=True) + EPS) * lng_ref[...] + lnb_ref[...]
    ti = lax.broadcasted_iota(jnp.int32, (GMLP_CHUNK, GMLP_CHUNK), 0)
    si = lax.broadcasted_iota(jnp.int32, (GMLP_CHUNK, GMLP_CHUNK), 1)
    tril = ti >= si
    ws = jnp.concatenate([jnp.where(tril, w_ref[h], 0.0) for h in range(GMLP_HEADS)], axis=0).astype(BF16)
    lane = lax.broadcasted_iota(jnp.int32, (GMLP_CHUNK, BRANCH_W), 1)
    bias = bias_ref[...]
    outs = []
    for c in range(GM_TILE // GMLP_CHUNK):
        vb = v[c * GMLP_CHUNK:(c + 1) * GMLP_CHUNK, :].astype(BF16)
        mixed = jnp.dot(ws, vb, preferred_element_type=F32)
        s = mixed[(GMLP_HEADS - 1) * GMLP_CHUNK:]
        for h in range(GMLP_HEADS - 2, -1, -1):
            s = jnp.where(lane < (h + 1) * GMLP_HEAD_DIM, mixed[h * GMLP_CHUNK:(h + 1) * GMLP_CHUNK], s)
        outs.append(s + bias)
    s = jnp.concatenate(outs, axis=0)
    o_ref[...] = _rms(u * s, on_ref[...])


def _gmlp_mixer(p3, ln_g, ln_b, w_sp, b_sp, out_norm):
    b, l, _ = p3.shape
    bias = jnp.repeat(b_sp.T, GMLP_HEAD_DIM, axis=1)
    const = lambda shape: pl.BlockSpec(shape, lambda i, j: (0,) * len(shape))
    return pl.pallas_call(
        _gmlp_kernel,
        grid=(b, l // GM_TILE),
        in_specs=[pl.BlockSpec((None, GM_TILE, 2 * BRANCH_W), lambda i, j: (i, j, COL_GM // (2 * BRANCH_W))),
                  const((1, BRANCH_W)), const((1, BRANCH_W)),
                  const((GMLP_HEADS, GMLP_CHUNK, GMLP_CHUNK)), const((GMLP_CHUNK, BRANCH_W)),
                  const((1, BRANCH_W))],
        out_specs=pl.BlockSpec((None, GM_TILE, BRANCH_W), lambda i, j: (i, j, 0)),
        out_shape=jax.ShapeDtypeStruct((b, l, BRANCH_W), F32),
        compiler_params=_cparams(("parallel", "parallel"), 32),
        name="gmlp_mixer",
    )(p3, ln_g.reshape(1, -1), ln_b.reshape(1, -1), w_sp, bias, out_norm.reshape(1, -1))


def _shortconv_kernel(b_ref, c_ref, x_ref, w_ref, on_ref, o_ref, halo_ref):
    @pl.when(pl.program_id(1) == 0)
    def _():
        halo_ref[...] = jnp.zeros_like(halo_ref)

    cx = c_ref[...] * x_ref[...]
    ext = jnp.concatenate([halo_ref[...], cx], axis=0)
    halo_ref[...] = cx[SC_TILE - SUBLANES:, :]
    y = w_ref[SC_CONV - 1:SC_CONV, :] * cx
    for j in range(SC_CONV - 1):
        sh = SC_CONV - 1 - j
        y = y + w_ref[j:j + 1, :] * pltpu.roll(ext, sh, 0)[SUBLANES:, :]
    o_ref[...] = _rms(b_ref[...] * y, on_ref[...])


def _shortconv_mixer(p3, conv_w, out_norm):
    b, l, _ = p3.shape
    w = jnp.concatenate([conv_w, jnp.zeros((SUBLANES - SC_CONV, BRANCH_W), F32)], axis=0)
    col = lambda c: pl.BlockSpec((None, SC_TILE, BRANCH_W), lambda i, j: (i, j, c // BRANCH_W))
    const = lambda shape: pl.BlockSpec(shape, lambda i, j: (0,) * len(shape))
    return pl.pallas_call(
        _shortconv_kernel,
        grid=(b, l // SC_TILE),
        in_specs=[col(COL_SCB), col(COL_SCC), col(COL_SCX), const((SUBLANES, BRANCH_W)), const((1, BRANCH_W))],
        out_specs=pl.BlockSpec((None, SC_TILE, BRANCH_W), lambda i, j: (i, j, 0)),
        out_shape=jax.ShapeDtypeStruct((b, l, BRANCH_W), F32),
        scratch_shapes=[pltpu.VMEM((SUBLANES, BRANCH_W), F32)],
        compiler_params=_cparams(("parallel", "arbitrary"), 32),
        name="shortconv_mixer",
    )(p3, p3, p3, w, out_norm.reshape(1, -1))


def _gdn_kernel(q_ref, k_ref, v_ref, z_ref, ab_ref, cw_ref, alog_ref, dtb_ref, ng_ref, o_ref,
                halo_ref, state_ref):
    c_len, h_dim, nh, w = GDN_CHUNK, GDN_HEAD_DIM, GDN_HEADS, BRANCH_W
    n_chunks = GDN_TILE // c_len

    @pl.when(pl.program_id(1) == 0)
    def _():
        halo_ref[...] = jnp.zeros_like(halo_ref)
        state_ref[...] = jnp.zeros_like(state_ref)

    shift = int(math.log2(h_dim))
    r256 = lax.broadcasted_iota(jnp.int32, (w, w), 0)
    c256 = lax.broadcasted_iota(jnp.int32, (w, w), 1)
    same_head = (r256 >> shift) == (c256 >> shift)
    ebd = jnp.where(same_head, 1.0, 0.0).astype(BF16)
    causal = same_head & (r256 >= c256)
    strict = same_head & (r256 > c256)
    eye = jnp.where(r256 == c256, 1.0, 0.0)
    r128 = lax.broadcasted_iota(jnp.int32, (LANES, w), 0)
    c128 = lax.broadcasted_iota(jnp.int32, (LANES, w), 1)
    exp_a = jnp.where(r128 == (c128 >> shift), 1.0, 0.0).astype(BF16)
    exp_b = jnp.where(r128 == (c128 >> shift) + AB_B_LANE, 1.0, 0.0).astype(BF16)

    def stack(a):
        lane_head = (lax.broadcasted_iota(jnp.int32, a.shape, 1) >> shift) & (nh - 1)
        return jnp.concatenate([jnp.where(lane_head == h, a, 0.0) for h in range(nh)], axis=0)

    def unstack(a):
        return a[0:c_len] + a[c_len:2 * c_len] + a[2 * c_len:3 * c_len] + a[3 * c_len:4 * c_len]

    def conv_silu(ref, bb, col):
        cur = ref[bb]
        ext = jnp.concatenate([halo_ref[bb, :, col * w:(col + 1) * w], cur], axis=0)
        halo_ref[bb, :, col * w:(col + 1) * w] = cur[GDN_TILE - SUBLANES:, :]
        y = cw_ref[GDN_CONV - 1:GDN_CONV, col * w:(col + 1) * w] * cur
        for j in range(GDN_CONV - 1):
            sh = GDN_CONV - 1 - j
            y = y + cw_ref[j:j + 1, col * w:(col + 1) * w] * pltpu.roll(ext, sh, 0)[SUBLANES:, :]
        return _silu(y)

    rt = lax.broadcasted_iota(jnp.int32, (GDN_TILE, GDN_TILE), 0)
    ct = lax.broadcasted_iota(jnp.int32, (GDN_TILE, GDN_TILE), 1)
    ltri = jnp.where(((rt >> shift) == (ct >> shift)) & (rt >= ct), 1.0, 0.0).astype(BF16)
    tile4 = lambda a: jnp.concatenate([a] * nh, axis=0)

    q_decs, k_decs, g_tots, rhss, a_mats, qkds = [], [], [], [], [], []
    for bb in range(GDN_BATCH):
        q = conv_silu(q_ref, bb, 0)
        k = conv_silu(k_ref, bb, 1)
        v = conv_silu(v_ref, bb, 2)
        q = q * lax.rsqrt(jnp.dot((q * q).astype(BF16), ebd, preferred_element_type=F32) + EPS) * (h_dim ** -0.5)
        k = k * lax.rsqrt(jnp.dot((k * k).astype(BF16), ebd, preferred_element_type=F32) + EPS)
        ab = ab_ref[bb]
        beta = jax.nn.sigmoid(ab)
        xa = ab + dtb_ref[...]
        softplus = jnp.maximum(xa, 0.0) + jnp.log(1.0 + jnp.exp(-jnp.abs(xa)))
        g = -jnp.exp(alog_ref[...]) * softplus
        gcum = _split_dot(ltri, g, 3, data_on_left=False)
        gexp = _split_dot(gcum, exp_a, 2, data_on_left=True)
        bexp = _split_dot(beta, exp_b, 2, data_on_left=True)
        for c in range(n_chunks):
            sl = slice(c * c_len, (c + 1) * c_len)
            qc, kc, vc, gc, bc = q[sl], k[sl], v[sl], gexp[sl], bexp[sl]
            glast = gc[c_len - 1:c_len, :]
            eg = jnp.exp(gc)
            q_decs.append(qc * eg)
            k_decs.append(kc * jnp.exp(glast - gc))
            g_tots.append(jnp.exp(glast))
            kst = stack(kc).astype(BF16)
            kk = _bdot_nt(kst, kst)
            qk = _bdot_nt(stack(qc), kst)
            gcol = stack(gc)
            bcol = stack(bc)
            dec = jnp.exp(jnp.where(causal, gcol - gcol.T, -jnp.inf))
            a_mats.append(jnp.where(strict, bcol * kk * dec, 0.0))
            qkds.append((qk * dec).astype(BF16))
            rhss.append(jnp.concatenate([tile4(vc) * bcol, tile4(kc * eg) * bcol], axis=1).astype(BF16))
    t_invs = [eye - a for a in a_mats]
    pws = a_mats
    for _ in range(int(math.log2(c_len)) - 1):
        pws = [_bdot(p, p) for p in pws]
        t_invs = [t + _bdot(t, p) for t, p in zip(t_invs, pws)]
    sols = [unstack(_bdot(t, r)) for t, r in zip(t_invs, rhss)]
    states = [state_ref[bb] for bb in range(GDN_BATCH)]
    outs = [[] for _ in range(GDN_BATCH)]
    for c in range(n_chunks):
        for bb in range(GDN_BATCH):
            i = bb * n_chunks + c
            w_v, w_k = sols[i][:, :w], sols[i][:, w:]
            v_new = w_v - _bdot(w_k, states[bb])
            o = _bdot(q_decs[i], states[bb]) + unstack(_bdot(qkds[i], stack(v_new)))
            states[bb] = states[bb] * g_tots[i] + jnp.where(same_head, _bdot(k_decs[i].T, v_new), 0.0)
            outs[bb].append(o)
    for bb in range(GDN_BATCH):
        state_ref[bb] = states[bb]
        o = jnp.concatenate(outs[bb], axis=0)
        ms = jnp.dot((o * o).astype(BF16), ebd, preferred_element_type=F32) * (1.0 / h_dim)
        o = o * lax.rsqrt(ms + EPS) * ng_ref[...]
        o_ref[bb] = o * _silu(z_ref[bb])


def _gdn_mixer(p3, conv_w, a_log, dt_bias, norm_g):
    b, l, _ = p3.shape
    cw = jnp.concatenate([conv_w, jnp.zeros((SUBLANES - GDN_CONV, 3 * BRANCH_W), F32)], axis=0)
    lane_row = lambda vec: jnp.zeros((1, LANES), F32).at[0, :GDN_HEADS].set(vec)
    col = lambda c: pl.BlockSpec((GDN_BATCH, GDN_TILE, BRANCH_W), lambda i, j: (i, j, c // BRANCH_W))
    const = lambda shape: pl.BlockSpec(shape, lambda i, j: (0,) * len(shape))
    return pl.pallas_call(
        _gdn_kernel,
        grid=(b // GDN_BATCH, l // GDN_TILE),
        in_specs=[col(COL_Q), col(COL_K), col(COL_V), col(COL_Z),
                  pl.BlockSpec((GDN_BATCH, GDN_TILE, LANES), lambda i, j: (i, j, COL_AB // LANES)),
                  const((SUBLANES, 3 * BRANCH_W)), const((1, LANES)), const((1, LANES)), const((1, BRANCH_W))],
        out_specs=pl.BlockSpec((GDN_BATCH, GDN_TILE, BRANCH_W), lambda i, j: (i, j, 0)),
        out_shape=jax.ShapeDtypeStruct((b, l, BRANCH_W), F32),
        scratch_shapes=[pltpu.VMEM((GDN_BATCH, SUBLANES, 3 * BRANCH_W), F32),
                        pltpu.VMEM((GDN_BATCH, BRANCH_W, BRANCH_W), F32)],
        compiler_params=_cparams(("parallel", "arbitrary"), 48),
        name="gdn_mixer",
    )(p3, p3, p3, p3, p3, cw, lane_row(a_log), lane_row(dt_bias), jnp.tile(norm_g, GDN_HEADS).reshape(1, -1))


def _mix_residual(x_ref, y_refs, wo_ref):
    acc = x_ref[...]
    for i, y_ref in enumerate(y_refs):
        acc = acc + _bdot(y_ref[...], wo_ref[i * BRANCH_W:(i + 1) * BRANCH_W, :])
    return acc


def _ffn_kernel(x_ref, y0_ref, y1_ref, y2_ref, y3_ref, wo_ref, g_ref, wg_ref, wu_ref, wd_ref, o_ref):
    x = _mix_residual(x_ref, (y0_ref, y1_ref, y2_ref, y3_ref), wo_ref)
    hb = _rms(x, g_ref[...]).astype(BF16)
    acc = None
    for f in range(D_FF // FFN_TF):
        cols = slice(f * FFN_TF, (f + 1) * FFN_TF)
        a = jnp.dot(hb, wg_ref[:, cols], preferred_element_type=F32)
        u = jnp.dot(hb, wu_ref[:, cols], preferred_element_type=F32)
        y = jnp.dot((_silu(a) * u).astype(BF16), wd_ref[cols, :], preferred_element_type=F32)
        acc = y if acc is None else acc + y
    o_ref[...] = x + acc


def _ffn_dense(x2, ys, w_out, g, w_gate, w_up, w_down):
    t = x2.shape[0]
    row = lambda n: pl.BlockSpec((FFN_TM, n), lambda i: (i, 0))
    resident = lambda shape: pl.BlockSpec(shape, lambda i: (0, 0), pipeline_mode=pl.Buffered(1))
    return pl.pallas_call(
        _ffn_kernel,
        grid=(t // FFN_TM,),
        in_specs=[row(D_MODEL)] + [row(BRANCH_W)] * 4 + [pl.BlockSpec((D_MODEL, D_MODEL), lambda i: (0, 0)),
                  pl.BlockSpec((1, D_MODEL), lambda i: (0, 0)),
                  resident((D_MODEL, D_FF)), resident((D_MODEL, D_FF)), resident((D_FF, D_MODEL))],
        out_specs=row(D_MODEL),
        out_shape=jax.ShapeDtypeStruct((t, D_MODEL), F32),
        compiler_params=_cparams(("parallel",), 52),
        name="ffn_dense",
    )(x2, *ys, w_out.astype(BF16), g.reshape(1, -1), w_gate.astype(BF16), w_up.astype(BF16), w_down.astype(BF16))


def _tiles_to_rows(ref, m):
    return jnp.concatenate([ref[pl.ds(c, m, stride=ROW_TILE), :] for c in range(ROW_TILE)], axis=1)


def _rows_to_tiles(ref, value):
    for c in range(ROW_TILE):
        ref[pl.ds(c, value.shape[0], stride=ROW_TILE), :] = value[:, c * LANES:(c + 1) * LANES]


def _moe_kernel(be_ref, nu_ref, x_ref, g_ref, wg_ref, wu_ref, wd_ref, o_ref):
    del be_ref

    @pl.when(pl.program_id(0) < nu_ref[0])
    def _():
        xb = _rms(_tiles_to_rows(x_ref, MOE_TM), g_ref[...]).astype(BF16)
        acc = None
        for f in range(D_FF_EXPERT // MOE_TF):
            cols = slice(f * MOE_TF, (f + 1) * MOE_TF)
            a = jnp.dot(xb, wg_ref[:, cols], preferred_element_type=F32)
            u = jnp.dot(xb, wu_ref[:, cols], preferred_element_type=F32)
            y = jnp.dot((_silu(a) * u).astype(BF16), wd_ref[cols, :], preferred_element_type=F32)
            acc = y if acc is None else acc + y
        _rows_to_tiles(o_ref, acc)

    @pl.when(pl.program_id(0) >= nu_ref[0])
    def _():
        o_ref[...] = jnp.zeros_like(o_ref)


def _moe_experts(buf, g, block_e, n_used, wg, wu, wd):
    rows = buf.shape[0] // ROW_TILE
    expert = lambda shape: pl.BlockSpec((None,) + shape, lambda m, be, nu: (be[m], 0, 0),
                                        pipeline_mode=pl.Buffered(1))
    grid_spec = pltpu.PrefetchScalarGridSpec(
        num_scalar_prefetch=2,
        grid=(rows // MOE_TM,),
        in_specs=[pl.BlockSpec((MOE_TM * ROW_TILE, LANES), lambda m, be, nu: (jnp.minimum(m, nu[0] - 1), 0)),
                  pl.BlockSpec((1, D_MODEL), lambda m, be, nu: (0, 0)),
                  expert((D_MODEL, D_FF_EXPERT)), expert((D_MODEL, D_FF_EXPERT)), expert((D_FF_EXPERT, D_MODEL))],
        out_specs=pl.BlockSpec((MOE_TM * ROW_TILE, LANES), lambda m, be, nu: (m, 0)),
    )
    return pl.pallas_call(
        _moe_kernel,
        grid_spec=grid_spec,
        out_shape=jax.ShapeDtypeStruct((rows * ROW_TILE, LANES), F32),
        compiler_params=_cparams(("arbitrary",), 56),
        name="moe_experts",
    )(block_e, n_used, buf, g.reshape(1, -1), wg, wu, wd)


def _route_kernel(x_ref, y0_ref, y1_ref, y2_ref, y3_ref, wo_ref, g_ref, wt_ref,
                  xo_ref, e_ref, r_ref, gc_ref, cnt_ref, run_ref):
    tm = ROUTE_TM

    @pl.when(pl.program_id(0) == 0)
    def _():
        run_ref[...] = jnp.zeros_like(run_ref)

    x = _mix_residual(x_ref, (y0_ref, y1_ref, y2_ref, y3_ref), wo_ref)
    _rows_to_tiles(xo_ref, x)
    h = _rms(x, g_ref[...])
    w = wt_ref[...]
    w_hi = w.astype(BF16).astype(F32)
    w_parts = jnp.concatenate([w_hi, w - w_hi], axis=0).astype(BF16)
    h_hi = h.astype(BF16)
    h_lo = (h - h_hi.astype(F32)).astype(BF16)
    nt = lambda a, b: lax.dot_general(a, b, (((1,), (1,)), ((), ())), preferred_element_type=F32)
    parts = nt(w_parts, h_hi) + nt(w_parts, h_lo)
    lt = parts[:N_EXPERTS] + parts[N_EXPERTS:]
    sub = lax.broadcasted_iota(jnp.int32, (N_EXPERTS, tm), 0)
    m1 = jnp.max(lt, axis=0, keepdims=True)
    i1 = jnp.min(jnp.where(lt == m1, sub, N_EXPERTS), axis=0, keepdims=True)
    lt2 = jnp.where(sub == i1, -jnp.inf, lt)
    m2 = jnp.max(lt2, axis=0, keepdims=True)
    i2 = jnp.min(jnp.where(lt2 == m2, sub, N_EXPERTS), axis=0, keepdims=True)
    oh0 = jnp.where(sub == i1, 1.0, 0.0)
    oh1 = jnp.where(sub == i2, 1.0, 0.0)
    cnt = oh0 + oh1
    ti = lax.broadcasted_iota(jnp.int32, (tm, tm), 0)
    tj = lax.broadcasted_iota(jnp.int32, (tm, tm), 1)
    upper = jnp.where(ti < tj, 1.0, 0.0).astype(BF16)
    pre = jnp.dot(cnt.astype(BF16), upper, preferred_element_type=F32) + run_ref[:, 0:1]
    r0 = jnp.sum(oh0 * pre, axis=0, keepdims=True)
    r1 = jnp.sum(oh1 * pre, axis=0, keepdims=True)
    e_ref[...] = jnp.concatenate([i1, i2], axis=0)
    r_ref[...] = jnp.concatenate([r0, r1], axis=0).astype(jnp.int32)
    run = run_ref[...] + jnp.sum(cnt, axis=1, keepdims=True)
    run_ref[...] = run
    cnt_ref[...] = run
    ex = jnp.exp(m2 - m1)
    g0 = 1.0 / (1.0 + ex)
    gates = jnp.where(sub == 0, g0, jnp.where(sub == 1, ex * g0, 0.0))
    er = lax.broadcasted_iota(jnp.int32, (N_EXPERTS, LANES), 0)
    ec = lax.broadcasted_iota(jnp.int32, (N_EXPERTS, LANES), 1)
    eye = jnp.where(er == ec, 1.0, 0.0).astype(BF16)
    acc = None
    for _ in range(3):
        piece = gates.astype(BF16)
        term = lax.dot_general(piece, eye, (((0,), (0,)), ((), ())), preferred_element_type=F32)
        acc = term if acc is None else acc + term
        gates = gates - piece.astype(F32)
    gc_ref[...] = acc


def _moe_route(x2, ys, w_out, ffn_norm, w_router):
    t = x2.shape[0]
    tm = ROUTE_TM
    row = lambda n: pl.BlockSpec((tm, n), lambda i: (i, 0))
    return pl.pallas_call(
        _route_kernel,
        grid=(t // tm,),
        in_specs=[row(D_MODEL)] + [row(BRANCH_W)] * 4 + [pl.BlockSpec((D_MODEL, D_MODEL), lambda i: (0, 0)),
                  pl.BlockSpec((1, D_MODEL), lambda i: (0, 0)),
                  pl.BlockSpec((N_EXPERTS, D_MODEL), lambda i: (0, 0))],
        out_specs=[pl.BlockSpec((tm * ROW_TILE, LANES), lambda i: (i, 0)),
                   pl.BlockSpec((TOP_K, tm), lambda i: (0, i)),
                   pl.BlockSpec((TOP_K, tm), lambda i: (0, i)),
                   row(LANES),
                   pl.BlockSpec((N_EXPERTS, LANES), lambda i: (0, 0))],
        out_shape=[jax.ShapeDtypeStruct((t * ROW_TILE, LANES), F32),
                   jax.ShapeDtypeStruct((TOP_K, t), jnp.int32),
                   jax.ShapeDtypeStruct((TOP_K, t), jnp.int32),
                   jax.ShapeDtypeStruct((t, LANES), F32),
                   jax.ShapeDtypeStruct((N_EXPERTS, LANES), F32)],
        scratch_shapes=[pltpu.VMEM((N_EXPERTS, LANES), F32)],
        compiler_params=_cparams(("arbitrary",), 40),
        name="moe_route",
    )(x2, *ys, w_out.astype(BF16), ffn_norm.reshape(1, -1), w_router.T)


def _row_copy(src_ref, src_row, dst_ref, dst_row, sem):
    tile = lambda row: pl.ds(row * ROW_TILE if isinstance(row, int) else pl.multiple_of(row * ROW_TILE, ROW_TILE),
                             ROW_TILE)
    return pltpu.make_async_copy(src_ref.at[tile(src_row), :], dst_ref.at[tile(dst_row), :], sem)


def _dispatch_kernel(plo_ref, pn_ref, dest_hbm, x_ref, out_hbm, idx_ref, zero_ref, idx_sem, row_sem, pad_sem):
    tm = ROUTE_TM
    load = pltpu.make_async_copy(dest_hbm.at[pl.program_id(0)], idx_ref, idx_sem)
    load.start()

    @pl.when(pl.program_id(0) == 0)
    def _():
        zero_ref[...] = jnp.zeros_like(zero_ref)
        for e in range(N_EXPERTS):
            def fill(r, carry, e=e):
                _row_copy(zero_ref, 0, out_hbm, plo_ref[e] + r, pad_sem).start()
                return carry
            lax.fori_loop(0, pn_ref[e], fill, 0)
        for e in range(N_EXPERTS):
            def done(r, carry, e=e):
                _row_copy(zero_ref, 0, out_hbm, plo_ref[e] + r, pad_sem).wait()
                return carry
            lax.fori_loop(0, pn_ref[e], done, 0)
        first_free = (plo_ref[N_EXPERTS - 1] + pn_ref[N_EXPERTS - 1]) // MOE_TM

        block_rows = MOE_TM * ROW_TILE

        def block_copy(b):
            return pltpu.make_async_copy(
                zero_ref, out_hbm.at[pl.ds(pl.multiple_of(b * block_rows, block_rows), block_rows), :], pad_sem)

        def fill_block(b, carry):
            block_copy(b).start()
            return carry

        def done_block(b, carry):
            block_copy(b).wait()
            return carry

        lax.fori_loop(first_free, out_hbm.shape[0] // block_rows, fill_block, 0)
        lax.fori_loop(first_free, out_hbm.shape[0] // block_rows, done_block, 0)

    load.wait()
    for t in range(tm):
        for k in range(TOP_K):
            _row_copy(x_ref, t, out_hbm, idx_ref[k * tm + t], row_sem).start(priority=k)
    for t in range(tm):
        for k in range(TOP_K):
            _row_copy(x_ref, t, out_hbm, idx_ref[k * tm + t], row_sem).wait()


def _moe_dispatch(xt, dest2, pad_lo, pad_n, n_rows):
    t = xt.shape[0] // ROW_TILE
    tm = ROUTE_TM
    grid_spec = pltpu.PrefetchScalarGridSpec(
        num_scalar_prefetch=2,
        grid=(t // tm,),
        in_specs=[pl.BlockSpec(memory_space=pl.ANY),
                  pl.BlockSpec((tm * ROW_TILE, LANES), lambda i, lo, n: (i, 0))],
        out_specs=pl.BlockSpec(memory_space=pl.ANY),
        scratch_shapes=[pltpu.SMEM((TOP_K * tm,), jnp.int32), pltpu.VMEM((MOE_TM * ROW_TILE, LANES), F32),
                        pltpu.SemaphoreType.DMA, pltpu.SemaphoreType.DMA, pltpu.SemaphoreType.DMA],
    )
    return pl.pallas_call(
        _dispatch_kernel,
        grid_spec=grid_spec,
        out_shape=jax.ShapeDtypeStruct((n_rows * ROW_TILE, LANES), F32),
        compiler_params=_cparams(("arbitrary",), 32),
        name="moe_dispatch",
    )(pad_lo, pad_n, dest2, xt)


def _combine_kernel(dest_hbm, x_ref, gc_ref, fn_ref, y_hbm, o_ref, idx0_ref, idx1_ref, ybuf_ref, idx_sem, row_sem):
    tm = ROUTE_TM
    i, n = pl.program_id(0), pl.num_programs(0)
    idx_refs = (idx0_ref, idx1_ref)

    def gather(idx_ref, slot, t, k):
        return _row_copy(y_hbm, idx_ref[k * tm + t], ybuf_ref.at[slot, k], t, row_sem.at[slot])

    @pl.when(i == 0)
    def _():
        first = pltpu.make_async_copy(dest_hbm.at[0], idx0_ref, idx_sem.at[0])
        first.start()
        first.wait()

        def issue(t, carry):
            for k in range(TOP_K):
                gather(idx0_ref, 0, t, k).start()
            return carry

        lax.fori_loop(0, tm, issue, 0)

    def step(slot):
        nxt = 1 - slot
        load = pltpu.make_async_copy(dest_hbm.at[jnp.minimum(i + 1, n - 1)], idx_refs[nxt], idx_sem.at[nxt])
        load.start()
        for t in range(tm):
            for k in range(TOP_K):
                gather(idx_refs[slot], slot, t, k).wait()
        load.wait()
        for t in range(tm):
            for k in range(TOP_K):
                gather(idx_refs[nxt], nxt, t, k).start(priority=k)
        gc = gc_ref[...]
        x = (_tiles_to_rows(x_ref, tm) + gc[:, 0:1] * _tiles_to_rows(ybuf_ref.at[slot, 0], tm)
             + gc[:, 1:2] * _tiles_to_rows(ybuf_ref.at[slot, 1], tm))
        o_ref[...] = _rms(x, fn_ref[...])

        @pl.when(i == n - 1)
        def _():
            for t in range(tm):
                for k in range(TOP_K):
                    gather(idx_refs[nxt], nxt, t, k).wait()

    for slot in range(2):
        pl.when(i % 2 == slot)(functools.partial(step, slot))


def _moe_combine_norm(xt, y_buf, dest2, gcol, final_norm):
    t = xt.shape[0] // ROW_TILE
    tm = ROUTE_TM
    return pl.pallas_call(
        _combine_kernel,
        grid=(t // tm,),
        in_specs=[pl.BlockSpec(memory_space=pl.ANY),
                  pl.BlockSpec((tm * ROW_TILE, LANES), lambda i: (i, 0)),
                  pl.BlockSpec((tm, LANES), lambda i: (i, 0)),
                  pl.BlockSpec((1, D_MODEL), lambda i: (0, 0)),
                  pl.BlockSpec(memory_space=pl.ANY)],
        out_specs=pl.BlockSpec((tm, D_MODEL), lambda i: (i, 0)),
        out_shape=jax.ShapeDtypeStruct((t, D_MODEL), F32),
        scratch_shapes=[pltpu.SMEM((TOP_K * tm,), jnp.int32), pltpu.SMEM((TOP_K * tm,), jnp.int32),
                        pltpu.VMEM((2, TOP_K, tm * ROW_TILE, LANES), F32),
                        pltpu.SemaphoreType.DMA((2,)), pltpu.SemaphoreType.DMA((2,))],
        compiler_params=_cparams(("arbitrary",), 40),
        name="moe_combine_norm",
    )(dest2, xt, gcol, final_norm.reshape(1, -1), y_buf)


def _moe_layer(x2, ys, w_out, ffn_norm, w_router, w_gate, w_up, w_down, final_norm):
    t = x2.shape[0]
    tm = ROUTE_TM
    xt, e01, r01, gcol, cnt = _moe_route(x2, ys, w_out, ffn_norm, w_router)
    counts = cnt[:, 0].astype(jnp.int32)
    padded = (counts + MOE_TM - 1) // MOE_TM * MOE_TM
    pad_ends = jnp.cumsum(padded)
    pad_starts = pad_ends - padded
    n_blocks = -(-t * TOP_K // MOE_TM) + N_EXPERTS
    blk_start = jnp.arange(n_blocks, dtype=jnp.int32) * MOE_TM
    block_e = jnp.minimum(jnp.sum(blk_start[:, None] >= pad_ends[None, :], axis=1), N_EXPERTS - 1).astype(jnp.int32)
    n_used = (pad_ends[-1] // MOE_TM).astype(jnp.int32).reshape(1)
    dest = r01 + jnp.sum(jnp.where(e01[None] == jnp.arange(N_EXPERTS)[:, None, None], pad_starts[:, None, None], 0),
                         axis=0)
    dest2 = dest.reshape(TOP_K, t // tm, tm).transpose(1, 0, 2).reshape(t // tm, TOP_K * tm)
    xs = _moe_dispatch(xt, dest2, pad_starts + counts, padded - counts, n_blocks * MOE_TM)
    y_buf = _moe_experts(xs, ffn_norm, block_e, n_used, w_gate.astype(BF16), w_up.astype(BF16),
                         w_down.astype(BF16))
    return _moe_combine_norm(xt, y_buf, dest2, gcol, final_norm)


def _permute_w_in(w_in):
    s5 = w_in[:, 0:256]
    gm = w_in[:, 256:768]
    qkv = w_in[:, 768:1536]
    z = w_in[:, 1536:1792]
    a = w_in[:, 1792:1796]
    b = w_in[:, 1796:1800]
    sc = w_in[:, 1800:2568]
    ab = jnp.zeros((D_MODEL, LANES), F32).at[:, 0:GDN_HEADS].set(a).at[:, AB_B_LANE:AB_B_LANE + GDN_HEADS].set(b)
    return jnp.concatenate([gm, s5, z, qkv, sc, ab], axis=1).astype(BF16)


def kernel(x, mix_norm, w_in, s5_lam_re, s5_lam_im, s5_log_step, s5_b_re, s5_b_im, s5_c_re, s5_c_im, s5_d, s5_w_glu, s5_b_glu, s5_out_norm, sgu_ln_g, sgu_ln_b, sgu_w, sgu_b, gmlp_out_norm, gdn_conv, gdn_a_log, gdn_dt_bias, gdn_norm, sc_conv, sc_out_norm, w_out, ffn_norm, ffn_w_gate, ffn_w_up, ffn_w_down, moe_router, moe_w_gate, moe_w_up, moe_w_down, final_norm):
    bsz, seqlen, d = x.shape
    t = bsz * seqlen
    x2 = x.reshape(t, d)
    out = None
    for l in range(DEPTH):
        p2 = _in_proj(x2, mix_norm[l].reshape(1, -1), _permute_w_in(w_in[l]))
        p3 = p2.reshape(bsz, seqlen, P_COLS)
        wb, wc, pw = _s5_params(s5_lam_re[l], s5_lam_im[l], s5_log_step[l], s5_b_re[l], s5_b_im[l],
                                s5_c_re[l], s5_c_im[l])
        y_s5 = _s5_mixer(p3, wb, wc, pw, s5_d[l], s5_w_glu[l], s5_b_glu[l], s5_out_norm[l])
        y_gm = _gmlp_mixer(p3, sgu_ln_g[l], sgu_ln_b[l], sgu_w[l], sgu_b[l], gmlp_out_norm[l])
        y_gdn = _gdn_mixer(p3, gdn_conv[l], gdn_a_log[l], gdn_dt_bias[l], gdn_norm[l])
        y_sc = _shortconv_mixer(p3, sc_conv[l], sc_out_norm[l])
        ys = [y.reshape(t, BRANCH_W) for y in (y_s5, y_gm, y_gdn, y_sc)]
        i = l // 2
        if l % 2 == 0:
            x2 = _ffn_dense(x2, ys, w_out[l], ffn_norm[l], ffn_w_gate[i], ffn_w_up[i], ffn_w_down[i])
        else:
            out = _moe_layer(x2, ys, w_out[l], ffn_norm[l], moe_router[i], moe_w_gate[i], moe_w_up[i],
                             moe_w_down[i], final_norm)
    return out.reshape(bsz, seqlen, d)
```

```python
import functools
import math

import jax
import jax.numpy as jnp
from jax import lax
from jax.experimental import pallas as pl
from jax.experimental.pallas import tpu as pltpu

F32 = jnp.float32
BF16 = jnp.bfloat16

D_MODEL = 1024
DEPTH = 2
BRANCH_W = 256
S5_GROUP = 16
S5_GROUPS = 16
S5_STATE = 64
S5_NSTATE = S5_GROUPS * S5_STATE
GMLP_HEADS = 4
GMLP_HEAD_DIM = 64
GMLP_CHUNK = 128
GDN_HEAD_DIM = 64
GDN_HEADS = 4
GDN_CONV = 4
GDN_CHUNK = 64
SC_CONV = 3
D_FF = 2816
N_EXPERTS = 8
TOP_K = 2
D_FF_EXPERT = 3584
EPS = 1e-6

LANES = 128
SUBLANES = 8

COL_GM = 0
COL_S5 = 512
COL_Z = 768
COL_Q = 1024
COL_K = 1280
COL_V = 1536
COL_SCB = 1792
COL_SCC = 2048
COL_SCX = 2304
COL_AB = 2560
P_COLS = 2688
AB_B_LANE = 64

IN_TM = 1024
S5_CHUNK = 128
S5_BATCH = 4
S5_SEG = S5_CHUNK // SUBLANES
S5_POW_ROW = 0
S5_SEG_ROW = S5_POW_ROW + S5_SEG
S5_CARRY_ROW = S5_SEG_ROW + 3 * SUBLANES
S5_TABLE_ROWS = S5_CARRY_ROW + SUBLANES
GM_TILE = 512
SC_TILE = 512
GDN_TILE = 128
GDN_BATCH = 4
FFN_TM = 512
FFN_TF = 256
MOE_TM = 512
MOE_TF = 512
ROUTE_TM = 512
ROW_TILE = D_MODEL // LANES


def _cparams(sem, vmem_mb):
    return pltpu.CompilerParams(dimension_semantics=sem, vmem_limit_bytes=vmem_mb * 1024 * 1024)


def _rms(x, g):
    return x * lax.rsqrt(jnp.mean(x * x, axis=-1, keepdims=True) + EPS) * g


def _silu(x):
    return x * jax.nn.sigmoid(x)


def _bdot(a, b):
    return jnp.dot(a.astype(BF16), b.astype(BF16), preferred_element_type=F32)


def _split_dot(a, b, passes, data_on_left):
    data = a if data_on_left else b
    acc = None
    for _ in range(passes):
        piece = data.astype(BF16)
        term = (jnp.dot(piece, b, preferred_element_type=F32) if data_on_left
                else jnp.dot(a, piece, preferred_element_type=F32))
        acc = term if acc is None else acc + term
        data = data - piece.astype(F32)
    return acc


def _bdot_nt(a, b):
    return lax.dot_general(a.astype(BF16), b.astype(BF16), (((1,), (1,)), ((), ())),
                           preferred_element_type=F32)


def _in_proj_kernel(x_ref, g_ref, w_ref, o_ref):
    h = _rms(x_ref[...], g_ref[...])
    o_ref[...] = _bdot(h, w_ref[...])


def _in_proj(x2, g, w):
    t = x2.shape[0]
    return pl.pallas_call(
        _in_proj_kernel,
        grid=(t // IN_TM,),
        in_specs=[pl.BlockSpec((IN_TM, D_MODEL), lambda i: (i, 0)),
                  pl.BlockSpec((1, D_MODEL), lambda i: (0, 0)),
                  pl.BlockSpec((D_MODEL, P_COLS), lambda i: (0, 0))],
        out_specs=pl.BlockSpec((IN_TM, P_COLS), lambda i: (i, 0)),
        out_shape=jax.ShapeDtypeStruct((t, P_COLS), F32),
        compiler_params=_cparams(("parallel",), 48),
        name="in_proj",
    )(x2, g, w)


def _s5_kernel(u0_ref, u1_ref, wb_ref, pw_ref, wc_ref, d_ref, wglu_ref, bglu_ref, on_ref, o_ref,
               sr_ref, si_ref, ys_ref):
    @pl.when(pl.program_id(1) == 0)
    def _():
        sr_ref[...] = jnp.zeros_like(sr_ref)
        si_ref[...] = jnp.zeros_like(si_ref)

    seg, c = S5_SEG, S5_CHUNK
    u = jnp.concatenate(
        [jnp.concatenate([jnp.concatenate([ref[bb, pl.ds(i, SUBLANES, stride=seg), :] for i in range(seg)], axis=0)
                          for ref in (u0_ref, u1_ref)], axis=1) for bb in range(S5_BATCH)], axis=0)
    bu = _bdot(u, wb_ref[...])
    xs = [_s5_scan(bu[bb * c:(bb + 1) * c], pw_ref, sr_ref.at[bb], si_ref.at[bb]) for bb in range(S5_BATCH)]
    x = jnp.concatenate(xs, axis=0)
    y = _bdot(x, wc_ref[...])
    y = y + d_ref[...] * u
    y = jax.nn.gelu(y)
    y = y * jax.nn.sigmoid(_bdot(y, wglu_ref[...]) + bglu_ref[...])
    y = _rms(y, on_ref[...])
    for bb in range(S5_BATCH):
        for slab in range(BRANCH_W // LANES):
            ys_ref[bb, slab] = y[bb * c:(bb + 1) * c, slab * LANES:(slab + 1) * LANES]
        for r in range(c // SUBLANES):
            start = (c // 2) * (r % 2) + r // 2
            for slab in range(BRANCH_W // LANES):
                o_ref[bb, r * SUBLANES:(r + 1) * SUBLANES, slab * LANES:(slab + 1) * LANES] = (
                    ys_ref[bb, slab, pl.ds(start, SUBLANES, stride=SUBLANES), :])


def _s5_scan(bu, pw_ref, sr_ref, si_ref):
    n, seg = S5_NSTATE, S5_SEG
    cmul = lambda ar, ai, xr, xi: (ar * xr - ai * xi, ar * xi + ai * xr)
    a_r, a_i = pw_ref[S5_POW_ROW:S5_POW_ROW + 1, :n], pw_ref[S5_POW_ROW:S5_POW_ROW + 1, n:]
    xr, xi = bu[0:SUBLANES, :n], bu[0:SUBLANES, n:]
    xrs, xis = [xr], [xi]
    for i in range(1, seg):
        rows = slice(i * SUBLANES, (i + 1) * SUBLANES)
        pr, pi = cmul(a_r, a_i, xr, xi)
        xr, xi = pr + bu[rows, :n], pi + bu[rows, n:]
        xrs.append(xr)
        xis.append(xi)
    fr, fi = xr, xi
    for s in range(int(math.log2(SUBLANES))):
        rows = slice(S5_SEG_ROW + s * SUBLANES, S5_SEG_ROW + (s + 1) * SUBLANES)
        pr, pi = cmul(pw_ref[rows, :n], pw_ref[rows, n:], pltpu.roll(fr, 1 << s, 0), pltpu.roll(fi, 1 << s, 0))
        fr, fi = fr + pr, fi + pi
    sbr = jnp.broadcast_to(sr_ref[...], (SUBLANES, n))
    sbi = jnp.broadcast_to(si_ref[...], (SUBLANES, n))
    rows = slice(S5_CARRY_ROW, S5_CARRY_ROW + SUBLANES)
    pr, pi = cmul(pw_ref[rows, :n], pw_ref[rows, n:], sbr, sbi)
    fr, fi = fr + pr, fi + pi
    sr_ref[...] = fr[SUBLANES - 1:, :]
    si_ref[...] = fi[SUBLANES - 1:, :]
    first = lax.broadcasted_iota(jnp.int32, (SUBLANES, n), 0) == 0
    cin_r = jnp.where(first, sbr, pltpu.roll(fr, 1, 0))
    cin_i = jnp.where(first, sbi, pltpu.roll(fi, 1, 0))
    for i in range(seg):
        row = S5_POW_ROW + i
        pr, pi = cmul(pw_ref[row:row + 1, :n], pw_ref[row:row + 1, n:], cin_r, cin_i)
        xrs[i], xis[i] = xrs[i] + pr, xis[i] + pi
    return jnp.concatenate([jnp.concatenate(xrs, axis=0), jnp.concatenate(xis, axis=0)], axis=1)


def _s5_params(lam_re, lam_im, log_step, b_re, b_im, c_re, c_im):
    g, n, p = S5_GROUPS, S5_STATE, S5_GROUP
    dt = jnp.exp(log_step)[:, None]
    mag = jnp.exp(lam_re * dt)
    ar, ai = mag * jnp.cos(lam_im * dt), mag * jnp.sin(lam_im * dt)
    den = lam_re * lam_re + lam_im * lam_im
    fr = ((ar - 1.0) * lam_re + ai * lam_im) / den
    fi = (ai * lam_re - (ar - 1.0) * lam_im) / den
    bbr = fr[..., None] * b_re - fi[..., None] * b_im
    bbi = fr[..., None] * b_im + fi[..., None] * b_re
    eye = jnp.eye(g, dtype=F32)
    wbr = jnp.einsum('gnp,gh->gphn', bbr, eye).reshape(g * p, g * n)
    wbi = jnp.einsum('gnp,gh->gphn', bbi, eye).reshape(g * p, g * n)
    wb = jnp.concatenate([wbr, wbi], axis=1)
    wcr = jnp.einsum('gpn,gh->gnhp', c_re, eye).reshape(g * n, g * p)
    wci = jnp.einsum('gpn,gh->gnhp', -c_im, eye).reshape(g * n, g * p)
    wc = jnp.concatenate([wcr, wci], axis=0)
    def powers(br, bi, count):
        rows_r, rows_i = [br], [bi]
        for _ in range(count - 1):
            qr, qi = rows_r[-1], rows_i[-1]
            rows_r.append(qr * br - qi * bi)
            rows_i.append(qr * bi + qi * br)
        return jnp.concatenate(rows_r, axis=0), jnp.concatenate(rows_i, axis=0)

    a1r, a1i = ar.reshape(1, g * n), ai.reshape(1, g * n)
    pos_r, pos_i = powers(a1r, a1i, S5_SEG)
    a16r, a16i = pos_r[S5_SEG - 1:], pos_i[S5_SEG - 1:]
    r_idx = jnp.arange(SUBLANES)[:, None]
    seg_r, seg_i = [], []
    pr, pi = a16r, a16i
    for s in range(int(math.log2(SUBLANES))):
        keep = r_idx >= (1 << s)
        seg_r.append(jnp.where(keep, pr, 0.0))
        seg_i.append(jnp.where(keep, pi, 0.0))
        pr, pi = pr * pr - pi * pi, 2.0 * pr * pi
    car_r, car_i = powers(a16r, a16i, SUBLANES)
    pw = jnp.concatenate([jnp.concatenate([pos_r] + seg_r + [car_r], axis=0),
                          jnp.concatenate([pos_i] + seg_i + [car_i], axis=0)], axis=1)
    return wb.astype(BF16), wc.astype(BF16), pw


def _s5_mixer(p3, wb, wc, pw, d_skip, w_glu, b_glu, out_norm):
    b, l, _ = p3.shape
    n2 = 2 * S5_NSTATE
    const = lambda shape: pl.BlockSpec(shape, lambda i, j: (0,) * len(shape))
    return pl.pallas_call(
        _s5_kernel,
        grid=(b // S5_BATCH, l // S5_CHUNK),
        in_specs=[pl.BlockSpec((S5_BATCH, S5_CHUNK, LANES), lambda i, j: (i, j, COL_S5 // LANES)),
                  pl.BlockSpec((S5_BATCH, S5_CHUNK, LANES), lambda i, j: (i, j, COL_S5 // LANES + 1)),
                  const((BRANCH_W, n2)), const((S5_TABLE_ROWS, n2)), const((n2, BRANCH_W)),
                  const((1, BRANCH_W)), const((BRANCH_W, BRANCH_W)), const((1, BRANCH_W)),
                  const((1, BRANCH_W))],
        out_specs=pl.BlockSpec((S5_BATCH, S5_CHUNK, BRANCH_W), lambda i, j: (i, j, 0)),
        out_shape=jax.ShapeDtypeStruct((b, l, BRANCH_W), F32),
        scratch_shapes=[pltpu.VMEM((S5_BATCH, 1, S5_NSTATE), F32), pltpu.VMEM((S5_BATCH, 1, S5_NSTATE), F32),
                        pltpu.VMEM((S5_BATCH, BRANCH_W // LANES, S5_CHUNK, LANES), F32)],
        compiler_params=_cparams(("parallel", "arbitrary"), 40),
        name="s5_mixer",
    )(p3, p3, wb, pw, wc, d_skip.reshape(1, -1), w_glu.astype(BF16), b_glu.reshape(1, -1),
      out_norm.reshape(1, -1))


def _gmlp_kernel(p_ref, lng_ref, lnb_ref, w_ref, bias_ref, on_ref, o_ref):
    z = jax.nn.gelu(p_ref[...])
    u, v = z[:, :BRANCH_W], z[:, BRANCH_W:]
    vc = v - jnp.mean(v, axis=-1, keepdims=True)
    v = vc * lax.rsqrt(jnp.mean(vc * vc, axis=-1, keepdims=True) + EPS) * lng_ref[...] + lnb_ref[...]
    ti = lax.broadcasted_iota(jnp.int32, (GMLP_CHUNK, GMLP_CHUNK), 0)
    si = lax.broadcasted_iota(jnp.int32, (GMLP_CHUNK, GMLP_CHUNK), 1)
    tril = ti >= si
    ws = jnp.concatenate([jnp.where(tril, w_ref[h], 0.0) for h in range(GMLP_HEADS)], axis=0).astype(BF16)
    lane = lax.broadcasted_iota(jnp.int32, (GMLP_CHUNK, BRANCH_W), 1)
    bias = bias_ref[...]
    outs = []
    for c in range(GM_TILE // GMLP_CHUNK):
        vb = v[c * GMLP_CHUNK:(c + 1) * GMLP_CHUNK, :].astype(BF16)
        mixed = jnp.dot(ws, vb, preferred_element_type=F32)
        s = mixed[(GMLP_HEADS - 1) * GMLP_CHUNK:]
        for h in range(GMLP_HEADS - 2, -1, -1):
            s = jnp.where(lane < (h + 1) * GMLP_HEAD_DIM, mixed[h * GMLP_CHUNK:(h + 1) * GMLP_CHUNK], s)
        outs.append(s + bias)
    s = jnp.concatenate(outs, axis=0)
    o_ref[...] = _rms(u * s, on_ref[...])


def _gmlp_mixer(p3, ln_g, ln_b, w_sp, b_sp, out_norm):
    b, l, _ = p3.shape
    bias = jnp.repeat(b_sp.T, GMLP_HEAD_DIM, axis=1)
    const = lambda shape: pl.BlockSpec(shape, lambda i, j: (0,) * len(shape))
    return pl.pallas_call(
        _gmlp_kernel,
        grid=(b, l // GM_TILE),
        in_specs=[pl.BlockSpec((None, GM_TILE, 2 * BRANCH_W), lambda i, j: (i, j, COL_GM // (2 * BRANCH_W))),
                  const((1, BRANCH_W)), const((1, BRANCH_W)),
                  const((GMLP_HEADS, GMLP_CHUNK, GMLP_CHUNK)), const((GMLP_CHUNK, BRANCH_W)),
                  const((1, BRANCH_W))],
        out_specs=pl.BlockSpec((None, GM_TILE, BRANCH_W), lambda i, j: (i, j, 0)),
        out_shape=jax.ShapeDtypeStruct((b, l, BRANCH_W), F32),
        compiler_params=_cparams(("parallel", "parallel"), 32),
        name="gmlp_mixer",
    )(p3, ln_g.reshape(1, -1), ln_b.reshape(1, -1), w_sp, bias, out_norm.reshape(1, -1))


def _shortconv_kernel(b_ref, c_ref, x_ref, w_ref, on_ref, o_ref, halo_ref):
    @pl.when(pl.program_id(1) == 0)
    def _():
        halo_ref[...] = jnp.zeros_like(halo_ref)

    cx = c_ref[...] * x_ref[...]
    ext = jnp.concatenate([halo_ref[...], cx], axis=0)
    halo_ref[...] = cx[SC_TILE - SUBLANES:, :]
    y = w_ref[SC_CONV - 1:SC_CONV, :] * cx
    for j in range(SC_CONV - 1):
        sh = SC_CONV - 1 - j
        y = y + w_ref[j:j + 1, :] * pltpu.roll(ext, sh, 0)[SUBLANES:, :]
    o_ref[...] = _rms(b_ref[...] * y, on_ref[...])


def _shortconv_mixer(p3, conv_w, out_norm):
    b, l, _ = p3.shape
    w = jnp.concatenate([conv_w, jnp.zeros((SUBLANES - SC_CONV, BRANCH_W), F32)], axis=0)
    col = lambda c: pl.BlockSpec((None, SC_TILE, BRANCH_W), lambda i, j: (i, j, c // BRANCH_W))
    const = lambda shape: pl.BlockSpec(shape, lambda i, j: (0,) * len(shape))
    return pl.pallas_call(
        _shortconv_kernel,
        grid=(b, l // SC_TILE),
        in_specs=[col(COL_SCB), col(COL_SCC), col(COL_SCX), const((SUBLANES, BRANCH_W)), const((1, BRANCH_W))],
        out_specs=pl.BlockSpec((None, SC_TILE, BRANCH_W), lambda i, j: (i, j, 0)),
        out_shape=jax.ShapeDtypeStruct((b, l, BRANCH_W), F32),
        scratch_shapes=[pltpu.VMEM((SUBLANES, BRANCH_W), F32)],
        compiler_params=_cparams(("parallel", "arbitrary"), 32),
        name="shortconv_mixer",
    )(p3, p3, p3, w, out_norm.reshape(1, -1))


def _gdn_kernel(q_ref, k_ref, v_ref, z_ref, ab_ref, cw_ref, alog_ref, dtb_ref, ng_ref, o_ref,
                halo_ref, state_ref):
    c_len, h_dim, nh, w = GDN_CHUNK, GDN_HEAD_DIM, GDN_HEADS, BRANCH_W
    n_chunks = GDN_TILE // c_len

    @pl.when(pl.program_id(1) == 0)
    def _():
        halo_ref[...] = jnp.zeros_like(halo_ref)
        state_ref[...] = jnp.zeros_like(state_ref)

    shift = int(math.log2(h_dim))
    r256 = lax.broadcasted_iota(jnp.int32, (w, w), 0)
    c256 = lax.broadcasted_iota(jnp.int32, (w, w), 1)
    same_head = (r256 >> shift) == (c256 >> shift)
    ebd = jnp.where(same_head, 1.0, 0.0).astype(BF16)
    causal = same_head & (r256 >= c256)
    strict = same_head & (r256 > c256)
    eye = jnp.where(r256 == c256, 1.0, 0.0)
    r128 = lax.broadcasted_iota(jnp.int32, (LANES, w), 0)
    c128 = lax.broadcasted_iota(jnp.int32, (LANES, w), 1)
    exp_a = jnp.where(r128 == (c128 >> shift), 1.0, 0.0).astype(BF16)
    exp_b = jnp.where(r128 == (c128 >> shift) + AB_B_LANE, 1.0, 0.0).astype(BF16)

    def stack(a):
        lane_head = (lax.broadcasted_iota(jnp.int32, a.shape, 1) >> shift) & (nh - 1)
        return jnp.concatenate([jnp.where(lane_head == h, a, 0.0) for h in range(nh)], axis=0)

    def unstack(a):
        return a[0:c_len] + a[c_len:2 * c_len] + a[2 * c_len:3 * c_len] + a[3 * c_len:4 * c_len]

    def conv_silu(ref, bb, col):
        cur = ref[bb]
        ext = jnp.concatenate([halo_ref[bb, :, col * w:(col + 1) * w], cur], axis=0)
        halo_ref[bb, :, col * w:(col + 1) * w] = cur[GDN_TILE - SUBLANES:, :]
        y = cw_ref[GDN_CONV - 1:GDN_CONV, col * w:(col + 1) * w] * cur
        for j in range(GDN_CONV - 1):
            sh = GDN_CONV - 1 - j
            y = y + cw_ref[j:j + 1, col * w:(col + 1) * w] * pltpu.roll(ext, sh, 0)[SUBLANES:, :]
        return _silu(y)

    rt = lax.broadcasted_iota(jnp.int32, (GDN_TILE, GDN_TILE), 0)
    ct = lax.broadcasted_iota(jnp.int32, (GDN_TILE, GDN_TILE), 1)
    ltri = jnp.where(((rt >> shift) == (ct >> shift)) & (rt >= ct), 1.0, 0.0).astype(BF16)
    tile4 = lambda a: jnp.concatenate([a] * nh, axis=0)

    q_decs, k_decs, g_tots, rhss, a_mats, qkds = [], [], [], [], [], []
    for bb in range(GDN_BATCH):
        q = conv_silu(q_ref, bb, 0)
        k = conv_silu(k_ref, bb, 1)
        v = conv_silu(v_ref, bb, 2)
        q = q * lax.rsqrt(jnp.dot((q * q).astype(BF16), ebd, preferred_element_type=F32) + EPS) * (h_dim ** -0.5)
        k = k * lax.rsqrt(jnp.dot((k * k).astype(BF16), ebd, preferred_element_type=F32) + EPS)
        ab = ab_ref[bb]
        beta = jax.nn.sigmoid(ab)
        xa = ab + dtb_ref[...]
        softplus = jnp.maximum(xa, 0.0) + jnp.log(1.0 + jnp.exp(-jnp.abs(xa)))
        g = -jnp.exp(alog_ref[...]) * softplus
        gcum = _split_dot(ltri, g, 3, data_on_left=False)
        gexp = _split_dot(gcum, exp_a, 2, data_on_left=True)
        bexp = _split_dot(beta, exp_b, 2, data_on_left=True)
        for c in range(n_chunks):
            sl = slice(c * c_len, (c + 1) * c_len)
            qc, kc, vc, gc, bc = q[sl], k[sl], v[sl], gexp[sl], bexp[sl]
            glast = gc[c_len - 1:c_len, :]
            eg = jnp.exp(gc)
            q_decs.append(qc * eg)
            k_decs.append(kc * jnp.exp(glast - gc))
            g_tots.append(jnp.exp(glast))
            kst = stack(kc).astype(BF16)
            kk = _bdot_nt(kst, kst)
            qk = _bdot_nt(stack(qc), kst)
            gcol = stack(gc)
            bcol = stack(bc)
            dec = jnp.exp(jnp.where(causal, gcol - gcol.T, -jnp.inf))
            a_mats.append(jnp.where(strict, bcol * kk * dec, 0.0))
            qkds.append((qk * dec).astype(BF16))
            rhss.append(jnp.concatenate([tile4(vc) * bcol, tile4(kc * eg) * bcol], axis=1).astype(BF16))
    t_invs = [eye - a for a in a_mats]
    pws = a_mats
    for _ in range(int(math.log2(c_len)) - 1):
        pws = [_bdot(p, p) for p in pws]
        t_invs = [t + _bdot(t, p) for t, p in zip(t_invs, pws)]
    sols = [unstack(_bdot(t, r)) for t, r in zip(t_invs, rhss)]
    states = [state_ref[bb] for bb in range(GDN_BATCH)]
    outs = [[] for _ in range(GDN_BATCH)]
    for c in range(n_chunks):
        for bb in range(GDN_BATCH):
            i = bb * n_chunks + c
            w_v, w_k = sols[i][:, :w], sols[i][:, w:]
            v_new = w_v - _bdot(w_k, states[bb])
            o = _bdot(q_decs[i], states[bb]) + unstack(_bdot(qkds[i], stack(v_new)))
            states[bb] = states[bb] * g_tots[i] + jnp.where(same_head, _bdot(k_decs[i].T, v_new), 0.0)
            outs[bb].append(o)
    for bb in range(GDN_BATCH):
        state_ref[bb] = states[bb]
        o = jnp.concatenate(outs[bb], axis=0)
        ms = jnp.dot((o * o).astype(BF16), ebd, preferred_element_type=F32) * (1.0 / h_dim)
        o = o * lax.rsqrt(ms + EPS) * ng_ref[...]
        o_ref[bb] = o * _silu(z_ref[bb])


def _gdn_mixer(p3, conv_w, a_log, dt_bias, norm_g):
    b, l, _ = p3.shape
    cw = jnp.concatenate([conv_w, jnp.zeros((SUBLANES - GDN_CONV, 3 * BRANCH_W), F32)], axis=0)
    lane_row = lambda vec: jnp.zeros((1, LANES), F32).at[0, :GDN_HEADS].set(vec)
    col = lambda c: pl.BlockSpec((GDN_BATCH, GDN_TILE, BRANCH_W), lambda i, j: (i, j, c // BRANCH_W))
    const = lambda shape: pl.BlockSpec(shape, lambda i, j: (0,) * len(shape))
    return pl.pallas_call(
        _gdn_kernel,
        grid=(b // GDN_BATCH, l // GDN_TILE),
        in_specs=[col(COL_Q), col(COL_K), col(COL_V), col(COL_Z),
                  pl.BlockSpec((GDN_BATCH, GDN_TILE, LANES), lambda i, j: (i, j, COL_AB // LANES)),
                  const((SUBLANES, 3 * BRANCH_W)), const((1, LANES)), const((1, LANES)), const((1, BRANCH_W))],
        out_specs=pl.BlockSpec((GDN_BATCH, GDN_TILE, BRANCH_W), lambda i, j: (i, j, 0)),
        out_shape=jax.ShapeDtypeStruct((b, l, BRANCH_W), F32),
        scratch_shapes=[pltpu.VMEM((GDN_BATCH, SUBLANES, 3 * BRANCH_W), F32),
                        pltpu.VMEM((GDN_BATCH, BRANCH_W, BRANCH_W), F32)],
        compiler_params=_cparams(("parallel", "arbitrary"), 48),
        name="gdn_mixer",
    )(p3, p3, p3, p3, p3, cw, lane_row(a_log), lane_row(dt_bias), jnp.tile(norm_g, GDN_HEADS).reshape(1, -1))


def _mix_residual(x_ref, y_refs, wo_ref):
    acc = x_ref[...]
    for i, y_ref in enumerate(y_refs):
        acc = acc + _bdot(y_ref[...], wo_ref[i * BRANCH_W:(i + 1) * BRANCH_W, :])
    return acc


def _ffn_kernel(x_ref, y0_ref, y1_ref, y2_ref, y3_ref, wo_ref, g_ref, wg_ref, wu_ref, wd_ref, o_ref):
    x = _mix_residual(x_ref, (y0_ref, y1_ref, y2_ref, y3_ref), wo_ref)
    hb = _rms(x, g_ref[...]).astype(BF16)
    acc = None
    for f in range(D_FF // FFN_TF):
        cols = slice(f * FFN_TF, (f + 1) * FFN_TF)
        a = jnp.dot(hb, wg_ref[:, cols], preferred_element_type=F32)
        u = jnp.dot(hb, wu_ref[:, cols], preferred_element_type=F32)
        y = jnp.dot((_silu(a) * u).astype(BF16), wd_ref[cols, :], preferred_element_type=F32)
        acc = y if acc is None else acc + y
    o_ref[...] = x + acc


def _ffn_dense(x2, ys, w_out, g, w_gate, w_up, w_down):
    t = x2.shape[0]
    row = lambda n: pl.BlockSpec((FFN_TM, n), lambda i: (i, 0))
    resident = lambda shape: pl.BlockSpec(shape, lambda i: (0, 0), pipeline_mode=pl.Buffered(1))
    return pl.pallas_call(
        _ffn_kernel,
        grid=(t // FFN_TM,),
        in_specs=[row(D_MODEL)] + [row(BRANCH_W)] * 4 + [pl.BlockSpec((D_MODEL, D_MODEL), lambda i: (0, 0)),
                  pl.BlockSpec((1, D_MODEL), lambda i: (0, 0)),
                  resident((D_MODEL, D_FF)), resident((D_MODEL, D_FF)), resident((D_FF, D_MODEL))],
        out_specs=row(D_MODEL),
        out_shape=jax.ShapeDtypeStruct((t, D_MODEL), F32),
        compiler_params=_cparams(("parallel",), 52),
        name="ffn_dense",
    )(x2, *ys, w_out.astype(BF16), g.reshape(1, -1), w_gate.astype(BF16), w_up.astype(BF16), w_down.astype(BF16))


def _tiles_to_rows(ref, m):
    return jnp.concatenate([ref[pl.ds(c, m, stride=ROW_TILE), :] for c in range(ROW_TILE)], axis=1)


def _rows_to_tiles(ref, value):
    for c in range(ROW_TILE):
        ref[pl.ds(c, value.shape[0], stride=ROW_TILE), :] = value[:, c * LANES:(c + 1) * LANES]


def _moe_kernel(be_ref, nu_ref, x_ref, g_ref, wg_ref, wu_ref, wd_ref, o_ref):
    del be_ref

    @pl.when(pl.program_id(0) < nu_ref[0])
    def _():
        xb = _rms(_tiles_to_rows(x_ref, MOE_TM), g_ref[...]).astype(BF16)
        acc = None
        for f in range(D_FF_EXPERT // MOE_TF):
            cols = slice(f * MOE_TF, (f + 1) * MOE_TF)
            a = jnp.dot(xb, wg_ref[:, cols], preferred_element_type=F32)
            u = jnp.dot(xb, wu_ref[:, cols], preferred_element_type=F32)
            y = jnp.dot((_silu(a) * u).astype(BF16), wd_ref[cols, :], preferred_element_type=F32)
            acc = y if acc is None else acc + y
        _rows_to_tiles(o_ref, acc)

    @pl.when(pl.program_id(0) >= nu_ref[0])
    def _():
        o_ref[...] = jnp.zeros_like(o_ref)


def _moe_experts(buf, g, block_e, n_used, wg, wu, wd):
    rows = buf.shape[0] // ROW_TILE
    expert = lambda shape: pl.BlockSpec((None,) + shape, lambda m, be, nu: (be[m], 0, 0),
                                        pipeline_mode=pl.Buffered(1))
    grid_spec = pltpu.PrefetchScalarGridSpec(
        num_scalar_prefetch=2,
        grid=(rows // MOE_TM,),
        in_specs=[pl.BlockSpec((MOE_TM * ROW_TILE, LANES), lambda m, be, nu: (jnp.minimum(m, nu[0] - 1), 0)),
                  pl.BlockSpec((1, D_MODEL), lambda m, be, nu: (0, 0)),
                  expert((D_MODEL, D_FF_EXPERT)), expert((D_MODEL, D_FF_EXPERT)), expert((D_FF_EXPERT, D_MODEL))],
        out_specs=pl.BlockSpec((MOE_TM * ROW_TILE, LANES), lambda m, be, nu: (m, 0)),
    )
    return pl.pallas_call(
        _moe_kernel,
        grid_spec=grid_spec,
        out_shape=jax.ShapeDtypeStruct((rows * ROW_TILE, LANES), F32),
        compiler_params=_cparams(("arbitrary",), 56),
        name="moe_experts",
    )(block_e, n_used, buf, g.reshape(1, -1), wg, wu, wd)


def _route_kernel(x_ref, y0_ref, y1_ref, y2_ref, y3_ref, wo_ref, g_ref, wt_ref,
                  xo_ref, e_ref, r_ref, gc_ref, cnt_ref, run_ref):
    tm = ROUTE_TM

    @pl.when(pl.program_id(0) == 0)
    def _():
        run_ref[...] = jnp.zeros_like(run_ref)

    x = _mix_residual(x_ref, (y0_ref, y1_ref, y2_ref, y3_ref), wo_ref)
    _rows_to_tiles(xo_ref, x)
    h = _rms(x, g_ref[...])
    w = wt_ref[...]
    w_hi = w.astype(BF16).astype(F32)
    w_parts = jnp.concatenate([w_hi, w - w_hi], axis=0).astype(BF16)
    h_hi = h.astype(BF16)
    h_lo = (h - h_hi.astype(F32)).astype(BF16)
    nt = lambda a, b: lax.dot_general(a, b, (((1,), (1,)), ((), ())), preferred_element_type=F32)
    parts = nt(w_parts, h_hi) + nt(w_parts, h_lo)
    lt = parts[:N_EXPERTS] + parts[N_EXPERTS:]
    sub = lax.broadcasted_iota(jnp.int32, (N_EXPERTS, tm), 0)
    m1 = jnp.max(lt, axis=0, keepdims=True)
    i1 = jnp.min(jnp.where(lt == m1, sub, N_EXPERTS), axis=0, keepdims=True)
    lt2 = jnp.where(sub == i1, -jnp.inf, lt)
    m2 = jnp.max(lt2, axis=0, keepdims=True)
    i2 = jnp.min(jnp.where(lt2 == m2, sub, N_EXPERTS), axis=0, keepdims=True)
    oh0 = jnp.where(sub == i1, 1.0, 0.0)
    oh1 = jnp.where(sub == i2, 1.0, 0.0)
    cnt = oh0 + oh1
    ti = lax.broadcasted_iota(jnp.int32, (tm, tm), 0)
    tj = lax.broadcasted_iota(jnp.int32, (tm, tm), 1)
    upper = jnp.where(ti < tj, 1.0, 0.0).astype(BF16)
    pre = jnp.dot(cnt.astype(BF16), upper, preferred_element_type=F32) + run_ref[:, 0:1]
    r0 = jnp.sum(oh0 * pre, axis=0, keepdims=True)
    r1 = jnp.sum(oh1 * pre, axis=0, keepdims=True)
    e_ref[...] = jnp.concatenate([i1, i2], axis=0)
    r_ref[...] = jnp.concatenate([r0, r1], axis=0).astype(jnp.int32)
    run = run_ref[...] + jnp.sum(cnt, axis=1, keepdims=True)
    run_ref[...] = run
    cnt_ref[...] = run
    ex = jnp.exp(m2 - m1)
    g0 = 1.0 / (1.0 + ex)
    gates = jnp.where(sub == 0, g0, jnp.where(sub == 1, ex * g0, 0.0))
    er = lax.broadcasted_iota(jnp.int32, (N_EXPERTS, LANES), 0)
    ec = lax.broadcasted_iota(jnp.int32, (N_EXPERTS, LANES), 1)
    eye = jnp.where(er == ec, 1.0, 0.0).astype(BF16)
    acc = None
    for _ in range(3):
        piece = gates.astype(BF16)
        term = lax.dot_general(piece, eye, (((0,), (0,)), ((), ())), preferred_element_type=F32)
        acc = term if acc is None else acc + term
        gates = gates - piece.astype(F32)
    gc_ref[...] = acc


def _moe_route(x2, ys, w_out, ffn_norm, w_router):
    t = x2.shape[0]
    tm = ROUTE_TM
    row = lambda n: pl.BlockSpec((tm, n), lambda i: (i, 0))
    return pl.pallas_call(
        _route_kernel,
        grid=(t // tm,),
        in_specs=[row(D_MODEL)] + [row(BRANCH_W)] * 4 + [pl.BlockSpec((D_MODEL, D_MODEL), lambda i: (0, 0)),
                  pl.BlockSpec((1, D_MODEL), lambda i: (0, 0)),
                  pl.BlockSpec((N_EXPERTS, D_MODEL), lambda i: (0, 0))],
        out_specs=[pl.BlockSpec((tm * ROW_TILE, LANES), lambda i: (i, 0)),
                   pl.BlockSpec((TOP_K, tm), lambda i: (0, i)),
                   pl.BlockSpec((TOP_K, tm), lambda i: (0, i)),
                   row(LANES),
                   pl.BlockSpec((N_EXPERTS, LANES), lambda i: (0, 0))],
        out_shape=[jax.ShapeDtypeStruct((t * ROW_TILE, LANES), F32),
                   jax.ShapeDtypeStruct((TOP_K, t), jnp.int32),
                   jax.ShapeDtypeStruct((TOP_K, t), jnp.int32),
                   jax.ShapeDtypeStruct((t, LANES), F32),
                   jax.ShapeDtypeStruct((N_EXPERTS, LANES), F32)],
        scratch_shapes=[pltpu.VMEM((N_EXPERTS, LANES), F32)],
        compiler_params=_cparams(("arbitrary",), 40),
        name="moe_route",
    )(x2, *ys, w_out.astype(BF16), ffn_norm.reshape(1, -1), w_router.T)


def _row_copy(src_ref, src_row, dst_ref, dst_row, sem):
    tile = lambda row: pl.ds(row * ROW_TILE if isinstance(row, int) else pl.multiple_of(row * ROW_TILE, ROW_TILE),
                             ROW_TILE)
    return pltpu.make_async_copy(src_ref.at[tile(src_row), :], dst_ref.at[tile(dst_row), :], sem)


def _dispatch_kernel(plo_ref, pn_ref, dest_hbm, x_ref, out_hbm, idx_ref, zero_ref, idx_sem, row_sem, pad_sem):
    tm = ROUTE_TM
    load = pltpu.make_async_copy(dest_hbm.at[pl.program_id(0)], idx_ref, idx_sem)
    load.start()

    @pl.when(pl.program_id(0) == 0)
    def _():
        zero_ref[...] = jnp.zeros_like(zero_ref)
        for e in range(N_EXPERTS):
            def fill(r, carry, e=e):
                _row_copy(zero_ref, 0, out_hbm, plo_ref[e] + r, pad_sem).start()
                return carry
            lax.fori_loop(0, pn_ref[e], fill, 0)
        for e in range(N_EXPERTS):
            def done(r, carry, e=e):
                _row_copy(zero_ref, 0, out_hbm, plo_ref[e] + r, pad_sem).wait()
                return carry
            lax.fori_loop(0, pn_ref[e], done, 0)
        first_free = (plo_ref[N_EXPERTS - 1] + pn_ref[N_EXPERTS - 1]) // MOE_TM

        block_rows = MOE_TM * ROW_TILE

        def block_copy(b):
            return pltpu.make_async_copy(
                zero_ref, out_hbm.at[pl.ds(pl.multiple_of(b * block_rows, block_rows), block_rows), :], pad_sem)

        def fill_block(b, carry):
            block_copy(b).start()
            return carry

        def done_block(b, carry):
            block_copy(b).wait()
            return carry

        lax.fori_loop(first_free, out_hbm.shape[0] // block_rows, fill_block, 0)
        lax.fori_loop(first_free, out_hbm.shape[0] // block_rows, done_block, 0)

    load.wait()
    for t in range(tm):
        for k in range(TOP_K):
            _row_copy(x_ref, t, out_hbm, idx_ref[k * tm + t], row_sem).start(priority=k)
    for t in range(tm):
        for k in range(TOP_K):
            _row_copy(x_ref, t, out_hbm, idx_ref[k * tm + t], row_sem).wait()


def _moe_dispatch(xt, dest2, pad_lo, pad_n, n_rows):
    t = xt.shape[0] // ROW_TILE
    tm = ROUTE_TM
    grid_spec = pltpu.PrefetchScalarGridSpec(
        num_scalar_prefetch=2,
        grid=(t // tm,),
        in_specs=[pl.BlockSpec(memory_space=pl.ANY),
                  pl.BlockSpec((tm * ROW_TILE, LANES), lambda i, lo, n: (i, 0))],
        out_specs=pl.BlockSpec(memory_space=pl.ANY),
        scratch_shapes=[pltpu.SMEM((TOP_K * tm,), jnp.int32), pltpu.VMEM((MOE_TM * ROW_TILE, LANES), F32),
                        pltpu.SemaphoreType.DMA, pltpu.SemaphoreType.DMA, pltpu.SemaphoreType.DMA],
    )
    return pl.pallas_call(
        _dispatch_kernel,
        grid_spec=grid_spec,
        out_shape=jax.ShapeDtypeStruct((n_rows * ROW_TILE, LANES), F32),
        compiler_params=_cparams(("arbitrary",), 32),
        name="moe_dispatch",
    )(pad_lo, pad_n, dest2, xt)


def _combine_kernel(dest_hbm, x_ref, gc_ref, fn_ref, y_hbm, o_ref, idx0_ref, idx1_ref, ybuf_ref, idx_sem, row_sem):
    tm = ROUTE_TM
    i, n = pl.program_id(0), pl.num_programs(0)
    idx_refs = (idx0_ref, idx1_ref)

    def gather(idx_ref, slot, t, k):
        return _row_copy(y_hbm, idx_ref[k * tm + t], ybuf_ref.at[slot, k], t, row_sem.at[slot])

    @pl.when(i == 0)
    def _():
        first = pltpu.make_async_copy(dest_hbm.at[0], idx0_ref, idx_sem.at[0])
        first.start()
        first.wait()

        def issue(t, carry):
            for k in range(TOP_K):
                gather(idx0_ref, 0, t, k).start()
            return carry

        lax.fori_loop(0, tm, issue, 0)

    def step(slot):
        nxt = 1 - slot
        load = pltpu.make_async_copy(dest_hbm.at[jnp.minimum(i + 1, n - 1)], idx_refs[nxt], idx_sem.at[nxt])
        load.start()
        for t in range(tm):
            for k in range(TOP_K):
                gather(idx_refs[slot], slot, t, k).wait()
        load.wait()
        for t in range(tm):
            for k in range(TOP_K):
                gather(idx_refs[nxt], nxt, t, k).start(priority=k)
        gc = gc_ref[...]
        x = (_tiles_to_rows(x_ref, tm) + gc[:, 0:1] * _tiles_to_rows(ybuf_ref.at[slot, 0], tm)
             + gc[:, 1:2] * _tiles_to_rows(ybuf_ref.at[slot, 1], tm))
        o_ref[...] = _rms(x, fn_ref[...])

        @pl.when(i == n - 1)
        def _():
            for t in range(tm):
                for k in range(TOP_K):
                    gather(idx_refs[nxt], nxt, t, k).wait()

    for slot in range(2):
        pl.when(i % 2 == slot)(functools.partial(step, slot))


def _moe_combine_norm(xt, y_buf, dest2, gcol, final_norm):
    t = xt.shape[0] // ROW_TILE
    tm = ROUTE_TM
    return pl.pallas_call(
        _combine_kernel,
        grid=(t // tm,),
        in_specs=[pl.BlockSpec(memory_space=pl.ANY),
                  pl.BlockSpec((tm * ROW_TILE, LANES), lambda i: (i, 0)),
                  pl.BlockSpec((tm, LANES), lambda i: (i, 0)),
                  pl.BlockSpec((1, D_MODEL), lambda i: (0, 0)),
                  pl.BlockSpec(memory_space=pl.ANY)],
        out_specs=pl.BlockSpec((tm, D_MODEL), lambda i: (i, 0)),
        out_shape=jax.ShapeDtypeStruct((t, D_MODEL), F32),
        scratch_shapes=[pltpu.SMEM((TOP_K * tm,), jnp.int32), pltpu.SMEM((TOP_K * tm,), jnp.int32),
                        pltpu.VMEM((2, TOP_K, tm * ROW_TILE, LANES), F32),
                        pltpu.SemaphoreType.DMA((2,)), pltpu.SemaphoreType.DMA((2,))],
        compiler_params=_cparams(("arbitrary",), 40),
        name="moe_combine_norm",
    )(dest2, xt, gcol, final_norm.reshape(1, -1), y_buf)


def _moe_layer(x2, ys, w_out, ffn_norm, w_router, w_gate, w_up, w_down, final_norm):
    t = x2.shape[0]
    tm = ROUTE_TM
    xt, e01, r01, gcol, cnt = _moe_route(x2, ys, w_out, ffn_norm, w_router)
    counts = cnt[:, 0].astype(jnp.int32)
    padded = (counts + MOE_TM - 1) // MOE_TM * MOE_TM
    pad_ends = jnp.cumsum(padded)
    pad_starts = pad_ends - padded
    n_blocks = -(-t * TOP_K // MOE_TM) + N_EXPERTS
    blk_start = jnp.arange(n_blocks, dtype=jnp.int32) * MOE_TM
    block_e = jnp.minimum(jnp.sum(blk_start[:, None] >= pad_ends[None, :], axis=1), N_EXPERTS - 1).astype(jnp.int32)
    n_used = (pad_ends[-1] // MOE_TM).astype(jnp.int32).reshape(1)
    dest = r01 + jnp.sum(jnp.where(e01[None] == jnp.arange(N_EXPERTS)[:, None, None], pad_starts[:, None, None], 0),
                         axis=0)
    dest2 = dest.reshape(TOP_K, t // tm, tm).transpose(1, 0, 2).reshape(t // tm, TOP_K * tm)
    xs = _moe_dispatch(xt, dest2, pad_starts + counts, padded - counts, n_blocks * MOE_TM)
    y_buf = _moe_experts(xs, ffn_norm, block_e, n_used, w_gate.astype(BF16), w_up.astype(BF16),
                         w_down.astype(BF16))
    return _moe_combine_norm(xt, y_buf, dest2, gcol, final_norm)


def _permute_w_in(w_in):
    s5 = w_in[:, 0:256]
    gm = w_in[:, 256:768]
    qkv = w_in[:, 768:1536]
    z = w_in[:, 1536:1792]
    a = w_in[:, 1792:1796]
    b = w_in[:, 1796:1800]
    sc = w_in[:, 1800:2568]
    ab = jnp.zeros((D_MODEL, LANES), F32).at[:, 0:GDN_HEADS].set(a).at[:, AB_B_LANE:AB_B_LANE + GDN_HEADS].set(b)
    return jnp.concatenate([gm, s5, z, qkv, sc, ab], axis=1).astype(BF16)


def kernel(x, mix_norm, w_in, s5_lam_re, s5_lam_im, s5_log_step, s5_b_re, s5_b_im, s5_c_re, s5_c_im, s5_d, s5_w_glu, s5_b_glu, s5_out_norm, sgu_ln_g, sgu_ln_b, sgu_w, sgu_b, gmlp_out_norm, gdn_conv, gdn_a_log, gdn_dt_bias, gdn_norm, sc_conv, sc_out_norm, w_out, ffn_norm, ffn_w_gate, ffn_w_up, ffn_w_down, moe_router, moe_w_gate, moe_w_up, moe_w_down, final_norm):
    bsz, seqlen, d = x.shape
    t = bsz * seqlen
    x2 = x.reshape(t, d)
    out = None
    for l in range(DEPTH):
        p2 = _in_proj(x2, mix_norm[l].reshape(1, -1), _permute_w_in(w_in[l]))
        p3 = p2.reshape(bsz, seqlen, P_COLS)
        wb, wc, pw = _s5_params(s5_lam_re[l], s5_lam_im[l], s5_log_step[l], s5_b_re[l], s5_b_im[l],
                                s5_c_re[l], s5_c_im[l])
        y_s5 = _s5_mixer(p3, wb, wc, pw, s5_d[l], s5_w_glu[l], s5_b_glu[l], s5_out_norm[l])
        y_gm = _gmlp_mixer(p3, sgu_ln_g[l], sgu_ln_b[l], sgu_w[l], sgu_b[l], gmlp_out_norm[l])
        y_gdn = _gdn_mixer(p3, gdn_conv[l], gdn_a_log[l], gdn_dt_bias[l], gdn_norm[l])
        y_sc = _shortconv_mixer(p3, sc_conv[l], sc_out_norm[l])
        ys = [y.reshape(t, BRANCH_W) for y in (y_s5, y_gm, y_gdn, y_sc)]
        i = l // 2
        if l % 2 == 0:
            x2 = _ffn_dense(x2, ys, w_out[l], ffn_norm[l], ffn_w_gate[i], ffn_w_up[i], ffn_w_down[i])
        else:
            out = _moe_layer(x2, ys, w_out[l], ffn_norm[l], moe_router[i], moe_w_gate[i], moe_w_up[i],
                             moe_w_down[i], final_norm)
    return out.reshape(bsz, seqlen, d)
```

```python
import functools
import math

import jax
import jax.numpy as jnp
from jax import lax
from jax.experimental import pallas as pl
from jax.experimental.pallas import tpu as pltpu

F32 = jnp.float32
BF16 = jnp.bfloat16

D_MODEL = 1024
DEPTH = 2
BRANCH_W = 256
S5_GROUP = 16
S5_GROUPS = 16
S5_STATE = 64
S5_NSTATE = S5_GROUPS * S5_STATE
GMLP_HEADS = 4
GMLP_HEAD_DIM = 64
GMLP_CHUNK = 128
GDN_HEAD_DIM = 64
GDN_HEADS = 4
GDN_CONV = 4
GDN_CHUNK = 64
SC_CONV = 3
D_FF = 2816
N_EXPERTS = 8
TOP_K = 2
D_FF_EXPERT = 3584
EPS = 1e-6

LANES = 128
SUBLANES = 8

COL_GM = 0
COL_S5 = 512
COL_Z = 768
COL_Q = 1024
COL_K = 1280
COL_V = 1536
COL_SCB = 1792
COL_SCC = 2048
COL_SCX = 2304
COL_AB = 2560
P_COLS = 2688
AB_B_LANE = 64

IN_TM = 1024
S5_CHUNK = 128
S5_BATCH = 4
S5_SEG = S5_CHUNK // SUBLANES
S5_POW_ROW = 0
S5_SEG_ROW = S5_POW_ROW + S5_SEG
S5_CARRY_ROW = S5_SEG_ROW + 3 * SUBLANES
S5_TABLE_ROWS = S5_CARRY_ROW + SUBLANES
GM_TILE = 512
SC_TILE = 512
GDN_TILE = 128
GDN_BATCH = 4
FFN_TM = 512
FFN_TF = 256
MOE_TM = 512
MOE_TF = 512
ROUTE_TM = 512
ROW_TILE = D_MODEL // LANES


def _cparams(sem, vmem_mb):
    return pltpu.CompilerParams(dimension_semantics=sem, vmem_limit_bytes=vmem_mb * 1024 * 1024)


def _rms(x, g):
    return x * lax.rsqrt(jnp.mean(x * x, axis=-1, keepdims=True) + EPS) * g


def _silu(x):
    return x * jax.nn.sigmoid(x)


def _bdot(a, b):
    return jnp.dot(a.astype(BF16), b.astype(BF16), preferred_element_type=F32)


def _split_dot(a, b, passes, data_on_left):
    data = a if data_on_left else b
    acc = None
    for _ in range(passes):
        piece = data.astype(BF16)
        term = (jnp.dot(piece, b, preferred_element_type=F32) if data_on_left
                else jnp.dot(a, piece, preferred_element_type=F32))
        acc = term if acc is None else acc + term
        data = data - piece.astype(F32)
    return acc


def _bdot_nt(a, b):
    return lax.dot_general(a.astype(BF16), b.astype(BF16), (((1,), (1,)), ((), ())),
                           preferred_element_type=F32)


def _in_proj_kernel(x_ref, g_ref, w_ref, o_ref):
    h = _rms(x_ref[...], g_ref[...])
    o_ref[...] = _bdot(h, w_ref[...])


def _in_proj(x2, g, w):
    t = x2.shape[0]
    return pl.pallas_call(
        _in_proj_kernel,
        grid=(t // IN_TM,),
        in_specs=[pl.BlockSpec((IN_TM, D_MODEL), lambda i: (i, 0)),
                  pl.BlockSpec((1, D_MODEL), lambda i: (0, 0)),
                  pl.BlockSpec((D_MODEL, P_COLS), lambda i: (0, 0))],
        out_specs=pl.BlockSpec((IN_TM, P_COLS), lambda i: (i, 0)),
        out_shape=jax.ShapeDtypeStruct((t, P_COLS), F32),
        compiler_params=_cparams(("parallel",), 48),
        name="in_proj",
    )(x2, g, w)


def _s5_kernel(u0_ref, u1_ref, wb_ref, pw_ref, wc_ref, d_ref, wglu_ref, bglu_ref, on_ref, o_ref,
               sr_ref, si_ref, ys_ref):
    @pl.when(pl.program_id(1) == 0)
    def _():
        sr_ref[...] = jnp.zeros_like(sr_ref)
        si_ref[...] = jnp.zeros_like(si_ref)

    seg, c = S5_SEG, S5_CHUNK
    u = jnp.concatenate(
        [jnp.concatenate([jnp.concatenate([ref[bb, pl.ds(i, SUBLANES, stride=seg), :] for i in range(seg)], axis=0)
                          for ref in (u0_ref, u1_ref)], axis=1) for bb in range(S5_BATCH)], axis=0)
    bu = _bdot(u, wb_ref[...])
    xs = [_s5_scan(bu[bb * c:(bb + 1) * c], pw_ref, sr_ref.at[bb], si_ref.at[bb]) for bb in range(S5_BATCH)]
    x = jnp.concatenate(xs, axis=0)
    y = _bdot(x, wc_ref[...])
    y = y + d_ref[...] * u
    y = jax.nn.gelu(y)
    y = y * jax.nn.sigmoid(_bdot(y, wglu_ref[...]) + bglu_ref[...])
    y = _rms(y, on_ref[...])
    for bb in range(S5_BATCH):
        for slab in range(BRANCH_W // LANES):
            ys_ref[bb, slab] = y[bb * c:(bb + 1) * c, slab * LANES:(slab + 1) * LANES]
        for r in range(c // SUBLANES):
            start = (c // 2) * (r % 2) + r // 2
            for slab in range(BRANCH_W // LANES):
                o_ref[bb, r * SUBLANES:(r + 1) * SUBLANES, slab * LANES:(slab + 1) * LANES] = (
                    ys_ref[bb, slab, pl.ds(start, SUBLANES, stride=SUBLANES), :])


def _s5_scan(bu, pw_ref, sr_ref, si_ref):
    n, seg = S5_NSTATE, S5_SEG
    cmul = lambda ar, ai, xr, xi: (ar * xr - ai * xi, ar * xi + ai * xr)
    a_r, a_i = pw_ref[S5_POW_ROW:S5_POW_ROW + 1, :n], pw_ref[S5_POW_ROW:S5_POW_ROW + 1, n:]
    xr, xi = bu[0:SUBLANES, :n], bu[0:SUBLANES, n:]
    xrs, xis = [xr], [xi]
    for i in range(1, seg):
        rows = slice(i * SUBLANES, (i + 1) * SUBLANES)
        pr, pi = cmul(a_r, a_i, xr, xi)
        xr, xi = pr + bu[rows, :n], pi + bu[rows, n:]
        xrs.append(xr)
        xis.append(xi)
    fr, fi = xr, xi
    for s in range(int(math.log2(SUBLANES))):
        rows = slice(S5_SEG_ROW + s * SUBLANES, S5_SEG_ROW + (s + 1) * SUBLANES)
        pr, pi = cmul(pw_ref[rows, :n], pw_ref[rows, n:], pltpu.roll(fr, 1 << s, 0), pltpu.roll(fi, 1 << s, 0))
        fr, fi = fr + pr, fi + pi
    sbr = jnp.broadcast_to(sr_ref[...], (SUBLANES, n))
    sbi = jnp.broadcast_to(si_ref[...], (SUBLANES, n))
    rows = slice(S5_CARRY_ROW, S5_CARRY_ROW + SUBLANES)
    pr, pi = cmul(pw_ref[rows, :n], pw_ref[rows, n:], sbr, sbi)
    fr, fi = fr + pr, fi + pi
    sr_ref[...] = fr[SUBLANES - 1:, :]
    si_ref[...] = fi[SUBLANES - 1:, :]
    first = lax.broadcasted_iota(jnp.int32, (SUBLANES, n), 0) == 0
    cin_r = jnp.where(first, sbr, pltpu.roll(fr, 1, 0))
    cin_i = jnp.where(first, sbi, pltpu.roll(fi, 1, 0))
    for i in range(seg):
        row = S5_POW_ROW + i
        pr, pi = cmul(pw_ref[row:row + 1, :n], pw_ref[row:row + 1, n:], cin_r, cin_i)
        xrs[i], xis[i] = xrs[i] + pr, xis[i] + pi
    return jnp.concatenate([jnp.concatenate(xrs, axis=0), jnp.concatenate(xis, axis=0)], axis=1)


def _s5_params(lam_re, lam_im, log_step, b_re, b_im, c_re, c_im):
    g, n, p = S5_GROUPS, S5_STATE, S5_GROUP
    dt = jnp.exp(log_step)[:, None]
    mag = jnp.exp(lam_re * dt)
    ar, ai = mag * jnp.cos(lam_im * dt), mag * jnp.sin(lam_im * dt)
    den = lam_re * lam_re + lam_im * lam_im
    fr = ((ar - 1.0) * lam_re + ai * lam_im) / den
    fi = (ai * lam_re - (ar - 1.0) * lam_im) / den
    bbr = fr[..., None] * b_re - fi[..., None] * b_im
    bbi = fr[..., None] * b_im + fi[..., None] * b_re
    eye = jnp.eye(g, dtype=F32)
    wbr = jnp.einsum('gnp,gh->gphn', bbr, eye).reshape(g * p, g * n)
    wbi = jnp.einsum('gnp,gh->gphn', bbi, eye).reshape(g * p, g * n)
    wb = jnp.concatenate([wbr, wbi], axis=1)
    wcr = jnp.einsum('gpn,gh->gnhp', c_re, eye).reshape(g * n, g * p)
    wci = jnp.einsum('gpn,gh->gnhp', -c_im, eye).reshape(g * n, g * p)
    wc = jnp.concatenate([wcr, wci], axis=0)
    def powers(br, bi, count):
        rows_r, rows_i = [br], [bi]
        for _ in range(count - 1):
            qr, qi = rows_r[-1], rows_i[-1]
            rows_r.append(qr * br - qi * bi)
            rows_i.append(qr * bi + qi * br)
        return jnp.concatenate(rows_r, axis=0), jnp.concatenate(rows_i, axis=0)

    a1r, a1i = ar.reshape(1, g * n), ai.reshape(1, g * n)
    pos_r, pos_i = powers(a1r, a1i, S5_SEG)
    a16r, a16i = pos_r[S5_SEG - 1:], pos_i[S5_SEG - 1:]
    r_idx = jnp.arange(SUBLANES)[:, None]
    seg_r, seg_i = [], []
    pr, pi = a16r, a16i
    for s in range(int(math.log2(SUBLANES))):
        keep = r_idx >= (1 << s)
        seg_r.append(jnp.where(keep, pr, 0.0))
        seg_i.append(jnp.where(keep, pi, 0.0))
        pr, pi = pr * pr - pi * pi, 2.0 * pr * pi
    car_r, car_i = powers(a16r, a16i, SUBLANES)
    pw = jnp.concatenate([jnp.concatenate([pos_r] + seg_r + [car_r], axis=0),
                          jnp.concatenate([pos_i] + seg_i + [car_i], axis=0)], axis=1)
    return wb.astype(BF16), wc.astype(BF16), pw


def _s5_mixer(p3, wb, wc, pw, d_skip, w_glu, b_glu, out_norm):
    b, l, _ = p3.shape
    n2 = 2 * S5_NSTATE
    const = lambda shape: pl.BlockSpec(shape, lambda i, j: (0,) * len(shape))
    return pl.pallas_call(
        _s5_kernel,
        grid=(b // S5_BATCH, l // S5_CHUNK),
        in_specs=[pl.BlockSpec((S5_BATCH, S5_CHUNK, LANES), lambda i, j: (i, j, COL_S5 // LANES)),
                  pl.BlockSpec((S5_BATCH, S5_CHUNK, LANES), lambda i, j: (i, j, COL_S5 // LANES + 1)),
                  const((BRANCH_W, n2)), const((S5_TABLE_ROWS, n2)), const((n2, BRANCH_W)),
                  const((1, BRANCH_W)), const((BRANCH_W, BRANCH_W)), const((1, BRANCH_W)),
                  const((1, BRANCH_W))],
        out_specs=pl.BlockSpec((S5_BATCH, S5_CHUNK, BRANCH_W), lambda i, j: (i, j, 0)),
        out_shape=jax.ShapeDtypeStruct((b, l, BRANCH_W), F32),
        scratch_shapes=[pltpu.VMEM((S5_BATCH, 1, S5_NSTATE), F32), pltpu.VMEM((S5_BATCH, 1, S5_NSTATE), F32),
                        pltpu.VMEM((S5_BATCH, BRANCH_W // LANES, S5_CHUNK, LANES), F32)],
        compiler_params=_cparams(("parallel", "arbitrary"), 40),
        name="s5_mixer",
    )(p3, p3, wb, pw, wc, d_skip.reshape(1, -1), w_glu.astype(BF16), b_glu.reshape(1, -1),
      out_norm.reshape(1, -1))


def _gmlp_kernel(p_ref, lng_ref, lnb_ref, w_ref, bias_ref, on_ref, o_ref):
    z = jax.nn.gelu(p_ref[...])
    u, v = z[:, :BRANCH_W], z[:, BRANCH_W:]
    vc = v - jnp.mean(v, axis=-1, keepdims=True)
    v = vc * lax.rsqrt(jnp.mean(vc * vc, axis=-1, keepdims=True) + EPS) * lng_ref[...] + lnb_ref[...]
    ti = lax.broadcasted_iota(jnp.int32, (GMLP_CHUNK, GMLP_CHUNK), 0)
    si = lax.broadcasted_iota(jnp.int32, (GMLP_CHUNK, GMLP_CHUNK), 1)
    tril = ti >= si
    ws = jnp.concatenate([jnp.where(tril, w_ref[h], 0.0) for h in range(GMLP_HEADS)], axis=0).astype(BF16)
    lane = lax.broadcasted_iota(jnp.int32, (GMLP_CHUNK, BRANCH_W), 1)
    bias = bias_ref[...]
    outs = []
    for c in range(GM_TILE // GMLP_CHUNK):
        vb = v[c * GMLP_CHUNK:(c + 1) * GMLP_CHUNK, :].astype(BF16)
        mixed = jnp.dot(ws, vb, preferred_element_type=F32)
        s = mixed[(GMLP_HEADS - 1) * GMLP_CHUNK:]
        for h in range(GMLP_HEADS - 2, -1, -1):
            s = jnp.where(lane < (h + 1) * GMLP_HEAD_DIM, mixed[h * GMLP_CHUNK:(h + 1) * GMLP_CHUNK], s)
        outs.append(s + bias)
    s = jnp.concatenate(outs, axis=0)
    o_ref[...] = _rms(u * s, on_ref[...])


def _gmlp_mixer(p3, ln_g, ln_b, w_sp, b_sp, out_norm):
    b, l, _ = p3.shape
    bias = jnp.repeat(b_sp.T, GMLP_HEAD_DIM, axis=1)
    const = lambda shape: pl.BlockSpec(shape, lambda i, j: (0,) * len(shape))
    return pl.pallas_call(
        _gmlp_kernel,
        grid=(b, l // GM_TILE),
        in_specs=[pl.BlockSpec((None, GM_TILE, 2 * BRANCH_W), lambda i, j: (i, j, COL_GM // (2 * BRANCH_W))),
                  const((1, BRANCH_W)), const((1, BRANCH_W)),
                  const((GMLP_HEADS, GMLP_CHUNK, GMLP_CHUNK)), const((GMLP_CHUNK, BRANCH_W)),
                  const((1, BRANCH_W))],
        out_specs=pl.BlockSpec((None, GM_TILE, BRANCH_W), lambda i, j: (i, j, 0)),
        out_shape=jax.ShapeDtypeStruct((b, l, BRANCH_W), F32),
        compiler_params=_cparams(("parallel", "parallel"), 32),
        name="gmlp_mixer",
    )(p3, ln_g.reshape(1, -1), ln_b.reshape(1, -1), w_sp, bias, out_norm.reshape(1, -1))


def _shortconv_kernel(b_ref, c_ref, x_ref, w_ref, on_ref, o_ref, halo_ref):
    @pl.when(pl.program_id(1) == 0)
    def _():
        halo_ref[...] = jnp.zeros_like(halo_ref)

    cx = c_ref[...] * x_ref[...]
    ext = jnp.concatenate([halo_ref[...], cx], axis=0)
    halo_ref[...] = cx[SC_TILE - SUBLANES:, :]
    y = w_ref[SC_CONV - 1:SC_CONV, :] * cx
    for j in range(SC_CONV - 1):
        sh = SC_CONV - 1 - j
        y = y + w_ref[j:j + 1, :] * pltpu.roll(ext, sh, 0)[SUBLANES:, :]
    o_ref[...] = _rms(b_ref[...] * y, on_ref[...])


def _shortconv_mixer(p3, conv_w, out_norm):
    b, l, _ = p3.shape
    w = jnp.concatenate([conv_w, jnp.zeros((SUBLANES - SC_CONV, BRANCH_W), F32)], axis=0)
    col = lambda c: pl.BlockSpec((None, SC_TILE, BRANCH_W), lambda i, j: (i, j, c // BRANCH_W))
    const = lambda shape: pl.BlockSpec(shape, lambda i, j: (0,) * len(shape))
    return pl.pallas_call(
        _shortconv_kernel,
        grid=(b, l // SC_TILE),
        in_specs=[col(COL_SCB), col(COL_SCC), col(COL_SCX), const((SUBLANES, BRANCH_W)), const((1, BRANCH_W))],
        out_specs=pl.BlockSpec((None, SC_TILE, BRANCH_W), lambda i, j: (i, j, 0)),
        out_shape=jax.ShapeDtypeStruct((b, l, BRANCH_W), F32),
        scratch_shapes=[pltpu.VMEM((SUBLANES, BRANCH_W), F32)],
        compiler_params=_cparams(("parallel", "arbitrary"), 32),
        name="shortconv_mixer",
    )(p3, p3, p3, w, out_norm.reshape(1, -1))


def _gdn_kernel(q_ref, k_ref, v_ref, z_ref, ab_ref, cw_ref, alog_ref, dtb_ref, ng_ref, o_ref,
                halo_ref, state_ref):
    c_len, h_dim, nh, w = GDN_CHUNK, GDN_HEAD_DIM, GDN_HEADS, BRANCH_W
    n_chunks = GDN_TILE // c_len

    @pl.when(pl.program_id(1) == 0)
    def _():
        halo_ref[...] = jnp.zeros_like(halo_ref)
        state_ref[...] = jnp.zeros_like(state_ref)

    shift = int(math.log2(h_dim))
    r256 = lax.broadcasted_iota(jnp.int32, (w, w), 0)
    c256 = lax.broadcasted_iota(jnp.int32, (w, w), 1)
    same_head = (r256 >> shift) == (c256 >> shift)
    ebd = jnp.where(same_head, 1.0, 0.0).astype(BF16)
    causal = same_head & (r256 >= c256)
    strict = same_head & (r256 > c256)
    eye = jnp.where(r256 == c256, 1.0, 0.0)
    r128 = lax.broadcasted_iota(jnp.int32, (LANES, w), 0)
    c128 = lax.broadcasted_iota(jnp.int32, (LANES, w), 1)
    exp_a = jnp.where(r128 == (c128 >> shift), 1.0, 0.0).astype(BF16)
    exp_b = jnp.where(r128 == (c128 >> shift) + AB_B_LANE, 1.0, 0.0).astype(BF16)

    def stack(a):
        lane_head = (lax.broadcasted_iota(jnp.int32, a.shape, 1) >> shift) & (nh - 1)
        return jnp.concatenate([jnp.where(lane_head == h, a, 0.0) for h in range(nh)], axis=0)

    def unstack(a):
        return a[0:c_len] + a[c_len:2 * c_len] + a[2 * c_len:3 * c_len] + a[3 * c_len:4 * c_len]

    def conv_silu(ref, bb, col):
        cur = ref[bb]
        ext = jnp.concatenate([halo_ref[bb, :, col * w:(col + 1) * w], cur], axis=0)
        halo_ref[bb, :, col * w:(col + 1) * w] = cur[GDN_TILE - SUBLANES:, :]
        y = cw_ref[GDN_CONV - 1:GDN_CONV, col * w:(col + 1) * w] * cur
        for j in range(GDN_CONV - 1):
            sh = GDN_CONV - 1 - j
            y = y + cw_ref[j:j + 1, col * w:(col + 1) * w] * pltpu.roll(ext, sh, 0)[SUBLANES:, :]
        return _silu(y)

    rt = lax.broadcasted_iota(jnp.int32, (GDN_TILE, GDN_TILE), 0)
    ct = lax.broadcasted_iota(jnp.int32, (GDN_TILE, GDN_TILE), 1)
    ltri = jnp.where(((rt >> shift) == (ct >> shift)) & (rt >= ct), 1.0, 0.0).astype(BF16)
    tile4 = lambda a: jnp.concatenate([a] * nh, axis=0)

    q_decs, k_decs, g_tots, rhss, a_mats, qkds = [], [], [], [], [], []
    for bb in range(GDN_BATCH):
        q = conv_silu(q_ref, bb, 0)
        k = conv_silu(k_ref, bb, 1)
        v = conv_silu(v_ref, bb, 2)
        q = q * lax.rsqrt(jnp.dot((q * q).astype(BF16), ebd, preferred_element_type=F32) + EPS) * (h_dim ** -0.5)
        k = k * lax.rsqrt(jnp.dot((k * k).astype(BF16), ebd, preferred_element_type=F32) + EPS)
        ab = ab_ref[bb]
        beta = jax.nn.sigmoid(ab)
        xa = ab + dtb_ref[...]
        softplus = jnp.maximum(xa, 0.0) + jnp.log(1.0 + jnp.exp(-jnp.abs(xa)))
        g = -jnp.exp(alog_ref[...]) * softplus
        gcum = _split_dot(ltri, g, 3, data_on_left=False)
        gexp = _split_dot(gcum, exp_a, 2, data_on_left=True)
        bexp = _split_dot(beta, exp_b, 2, data_on_left=True)
        for c in range(n_chunks):
            sl = slice(c * c_len, (c + 1) * c_len)
            qc, kc, vc, gc, bc = q[sl], k[sl], v[sl], gexp[sl], bexp[sl]
            glast = gc[c_len - 1:c_len, :]
            eg = jnp.exp(gc)
            q_decs.append(qc * eg)
            k_decs.append(kc * jnp.exp(glast - gc))
            g_tots.append(jnp.exp(glast))
            kst = stack(kc).astype(BF16)
            kk = _bdot_nt(kst, kst)
            qk = _bdot_nt(stack(qc), kst)
            gcol = stack(gc)
            bcol = stack(bc)
            dec = jnp.exp(jnp.where(causal, gcol - gcol.T, -jnp.inf))
            a_mats.append(jnp.where(strict, bcol * kk * dec, 0.0))
            qkds.append((qk * dec).astype(BF16))
            rhss.append(jnp.concatenate([tile4(vc) * bcol, tile4(kc * eg) * bcol], axis=1).astype(BF16))
    t_invs = [eye - a for a in a_mats]
    pws = a_mats
    for _ in range(int(math.log2(c_len)) - 1):
        pws = [_bdot(p, p) for p in pws]
        t_invs = [t + _bdot(t, p) for t, p in zip(t_invs, pws)]
    sols = [unstack(_bdot(t, r)) for t, r in zip(t_invs, rhss)]
    states = [state_ref[bb] for bb in range(GDN_BATCH)]
    outs = [[] for _ in range(GDN_BATCH)]
    for c in range(n_chunks):
        for bb in range(GDN_BATCH):
            i = bb * n_chunks + c
            w_v, w_k = sols[i][:, :w], sols[i][:, w:]
            v_new = w_v - _bdot(w_k, states[bb])
            o = _bdot(q_decs[i], states[bb]) + unstack(_bdot(qkds[i], stack(v_new)))
            states[bb] = states[bb] * g_tots[i] + jnp.where(same_head, _bdot(k_decs[i].T, v_new), 0.0)
            outs[bb].append(o)
    for bb in range(GDN_BATCH):
        state_ref[bb] = states[bb]
        o = jnp.concatenate(outs[bb], axis=0)
        ms = jnp.dot((o * o).astype(BF16), ebd, preferred_element_type=F32) * (1.0 / h_dim)
        o = o * lax.rsqrt(ms + EPS) * ng_ref[...]
        o_ref[bb] = o * _silu(z_ref[bb])


def _gdn_mixer(p3, conv_w, a_log, dt_bias, norm_g):
    b, l, _ = p3.shape
    cw = jnp.concatenate([conv_w, jnp.zeros((SUBLANES - GDN_CONV, 3 * BRANCH_W), F32)], axis=0)
    lane_row = lambda vec: jnp.zeros((1, LANES), F32).at[0, :GDN_HEADS].set(vec)
    col = lambda c: pl.BlockSpec((GDN_BATCH, GDN_TILE, BRANCH_W), lambda i, j: (i, j, c // BRANCH_W))
    const = lambda shape: pl.BlockSpec(shape, lambda i, j: (0,) * len(shape))
    return pl.pallas_call(
        _gdn_kernel,
        grid=(b // GDN_BATCH, l // GDN_TILE),
        in_specs=[col(COL_Q), col(COL_K), col(COL_V), col(COL_Z),
                  pl.BlockSpec((GDN_BATCH, GDN_TILE, LANES), lambda i, j: (i, j, COL_AB // LANES)),
                  const((SUBLANES, 3 * BRANCH_W)), const((1, LANES)), const((1, LANES)), const((1, BRANCH_W))],
        out_specs=pl.BlockSpec((GDN_BATCH, GDN_TILE, BRANCH_W), lambda i, j: (i, j, 0)),
        out_shape=jax.ShapeDtypeStruct((b, l, BRANCH_W), F32),
        scratch_shapes=[pltpu.VMEM((GDN_BATCH, SUBLANES, 3 * BRANCH_W), F32),
                        pltpu.VMEM((GDN_BATCH, BRANCH_W, BRANCH_W), F32)],
        compiler_params=_cparams(("parallel", "arbitrary"), 48),
        name="gdn_mixer",
    )(p3, p3, p3, p3, p3, cw, lane_row(a_log), lane_row(dt_bias), jnp.tile(norm_g, GDN_HEADS).reshape(1, -1))


def _mix_residual(x_ref, y_refs, wo_ref):
    acc = x_ref[...]
    for i, y_ref in enumerate(y_refs):
        acc = acc + _bdot(y_ref[...], wo_ref[i * BRANCH_W:(i + 1) * BRANCH_W, :])
    return acc


def _ffn_kernel(x_ref, y0_ref, y1_ref, y2_ref, y3_ref, wo_ref, g_ref, wg_ref, wu_ref, wd_ref, o_ref):
    x = _mix_residual(x_ref, (y0_ref, y1_ref, y2_ref, y3_ref), wo_ref)
    hb = _rms(x, g_ref[...]).astype(BF16)
    acc = None
    for f in range(D_FF // FFN_TF):
        cols = slice(f * FFN_TF, (f + 1) * FFN_TF)
        a = jnp.dot(hb, wg_ref[:, cols], preferred_element_type=F32)
        u = jnp.dot(hb, wu_ref[:, cols], preferred_element_type=F32)
        y = jnp.dot((_silu(a) * u).astype(BF16), wd_ref[cols, :], preferred_element_type=F32)
        acc = y if acc is None else acc + y
    o_ref[...] = x + acc


def _ffn_dense(x2, ys, w_out, g, w_gate, w_up, w_down):
    t = x2.shape[0]
    row = lambda n: pl.BlockSpec((FFN_TM, n), lambda i: (i, 0))
    resident = lambda shape: pl.BlockSpec(shape, lambda i: (0, 0), pipeline_mode=pl.Buffered(1))
    return pl.pallas_call(
        _ffn_kernel,
        grid=(t // FFN_TM,),
        in_specs=[row(D_MODEL)] + [row(BRANCH_W)] * 4 + [pl.BlockSpec((D_MODEL, D_MODEL), lambda i: (0, 0)),
                  pl.BlockSpec((1, D_MODEL), lambda i: (0, 0)),
                  resident((D_MODEL, D_FF)), resident((D_MODEL, D_FF)), resident((D_FF, D_MODEL))],
        out_specs=row(D_MODEL),
        out_shape=jax.ShapeDtypeStruct((t, D_MODEL), F32),
        compiler_params=_cparams(("parallel",), 52),
        name="ffn_dense",
    )(x2, *ys, w_out.astype(BF16), g.reshape(1, -1), w_gate.astype(BF16), w_up.astype(BF16), w_down.astype(BF16))


def _tiles_to_rows(ref, m):
    return jnp.concatenate([ref[pl.ds(c, m, stride=ROW_TILE), :] for c in range(ROW_TILE)], axis=1)


def _rows_to_tiles(ref, value):
    for c in range(ROW_TILE):
        ref[pl.ds(c, value.shape[0], stride=ROW_TILE), :] = value[:, c * LANES:(c + 1) * LANES]


def _moe_kernel(be_ref, nu_ref, x_ref, g_ref, wg_ref, wu_ref, wd_ref, o_ref):
    del be_ref

    @pl.when(pl.program_id(0) < nu_ref[0])
    def _():
        xb = _rms(_tiles_to_rows(x_ref, MOE_TM), g_ref[...]).astype(BF16)
        acc = None
        for f in range(D_FF_EXPERT // MOE_TF):
            cols = slice(f * MOE_TF, (f + 1) * MOE_TF)
            a = jnp.dot(xb, wg_ref[:, cols], preferred_element_type=F32)
            u = jnp.dot(xb, wu_ref[:, cols], preferred_element_type=F32)
            y = jnp.dot((_silu(a) * u).astype(BF16), wd_ref[cols, :], preferred_element_type=F32)
            acc = y if acc is None else acc + y
        _rows_to_tiles(o_ref, acc)

    @pl.when(pl.program_id(0) >= nu_ref[0])
    def _():
        o_ref[...] = jnp.zeros_like(o_ref)


def _moe_experts(buf, g, block_e, n_used, wg, wu, wd):
    rows = buf.shape[0] // ROW_TILE
    expert = lambda shape, buffers=2: pl.BlockSpec((None,) + shape, lambda m, be, nu: (be[m], 0, 0),
                                                   pipeline_mode=pl.Buffered(buffers))
    grid_spec = pltpu.PrefetchScalarGridSpec(
        num_scalar_prefetch=2,
        grid=(rows // MOE_TM,),
        in_specs=[pl.BlockSpec((MOE_TM * ROW_TILE, LANES), lambda m, be, nu: (jnp.minimum(m, nu[0] - 1), 0)),
                  pl.BlockSpec((1, D_MODEL), lambda m, be, nu: (0, 0)),
                  expert((D_MODEL, D_FF_EXPERT), 1), expert((D_MODEL, D_FF_EXPERT)), expert((D_FF_EXPERT, D_MODEL))],
        out_specs=pl.BlockSpec((MOE_TM * ROW_TILE, LANES), lambda m, be, nu: (m, 0)),
    )
    return pl.pallas_call(
        _moe_kernel,
        grid_spec=grid_spec,
        out_shape=jax.ShapeDtypeStruct((rows * ROW_TILE, LANES), F32),
        compiler_params=_cparams(("arbitrary",), 56),
        name="moe_experts",
    )(block_e, n_used, buf, g.reshape(1, -1), wg, wu, wd)


def _route_kernel(x_ref, y0_ref, y1_ref, y2_ref, y3_ref, wo_ref, g_ref, wt_ref,
                  xo_ref, e_ref, r_ref, gc_ref, cnt_ref, run_ref):
    tm = ROUTE_TM

    @pl.when(pl.program_id(0) == 0)
    def _():
        run_ref[...] = jnp.zeros_like(run_ref)

    x = _mix_residual(x_ref, (y0_ref, y1_ref, y2_ref, y3_ref), wo_ref)
    _rows_to_tiles(xo_ref, x)
    h = _rms(x, g_ref[...])
    w = wt_ref[...]
    w_hi = w.astype(BF16).astype(F32)
    w_parts = jnp.concatenate([w_hi, w - w_hi], axis=0).astype(BF16)
    h_hi = h.astype(BF16)
    h_lo = (h - h_hi.astype(F32)).astype(BF16)
    nt = lambda a, b: lax.dot_general(a, b, (((1,), (1,)), ((), ())), preferred_element_type=F32)
    parts = nt(w_parts, h_hi) + nt(w_parts, h_lo)
    lt = parts[:N_EXPERTS] + parts[N_EXPERTS:]
    sub = lax.broadcasted_iota(jnp.int32, (N_EXPERTS, tm), 0)
    m1 = jnp.max(lt, axis=0, keepdims=True)
    i1 = jnp.min(jnp.where(lt == m1, sub, N_EXPERTS), axis=0, keepdims=True)
    lt2 = jnp.where(sub == i1, -jnp.inf, lt)
    m2 = jnp.max(lt2, axis=0, keepdims=True)
    i2 = jnp.min(jnp.where(lt2 == m2, sub, N_EXPERTS), axis=0, keepdims=True)
    oh0 = jnp.where(sub == i1, 1.0, 0.0)
    oh1 = jnp.where(sub == i2, 1.0, 0.0)
    cnt = oh0 + oh1
    ti = lax.broadcasted_iota(jnp.int32, (tm, tm), 0)
    tj = lax.broadcasted_iota(jnp.int32, (tm, tm), 1)
    upper = jnp.where(ti < tj, 1.0, 0.0).astype(BF16)
    pre = jnp.dot(cnt.astype(BF16), upper, preferred_element_type=F32) + run_ref[:, 0:1]
    r0 = jnp.sum(oh0 * pre, axis=0, keepdims=True)
    r1 = jnp.sum(oh1 * pre, axis=0, keepdims=True)
    e_ref[...] = jnp.concatenate([i1, i2], axis=0)
    r_ref[...] = jnp.concatenate([r0, r1], axis=0).astype(jnp.int32)
    run = run_ref[...] + jnp.sum(cnt, axis=1, keepdims=True)
    run_ref[...] = run
    cnt_ref[...] = run
    ex = jnp.exp(m2 - m1)
    g0 = 1.0 / (1.0 + ex)
    gates = jnp.where(sub == 0, g0, jnp.where(sub == 1, ex * g0, 0.0))
    er = lax.broadcasted_iota(jnp.int32, (N_EXPERTS, LANES), 0)
    ec = lax.broadcasted_iota(jnp.int32, (N_EXPERTS, LANES), 1)
    eye = jnp.where(er == ec, 1.0, 0.0).astype(BF16)
    acc = None
    for _ in range(3):
        piece = gates.astype(BF16)
        term = lax.dot_general(piece, eye, (((0,), (0,)), ((), ())), preferred_element_type=F32)
        acc = term if acc is None else acc + term
        gates = gates - piece.astype(F32)
    gc_ref[...] = acc


def _moe_route(x2, ys, w_out, ffn_norm, w_router):
    t = x2.shape[0]
    tm = ROUTE_TM
    row = lambda n: pl.BlockSpec((tm, n), lambda i: (i, 0))
    return pl.pallas_call(
        _route_kernel,
        grid=(t // tm,),
        in_specs=[row(D_MODEL)] + [row(BRANCH_W)] * 4 + [pl.BlockSpec((D_MODEL, D_MODEL), lambda i: (0, 0)),
                  pl.BlockSpec((1, D_MODEL), lambda i: (0, 0)),
                  pl.BlockSpec((N_EXPERTS, D_MODEL), lambda i: (0, 0))],
        out_specs=[pl.BlockSpec((tm * ROW_TILE, LANES), lambda i: (i, 0)),
                   pl.BlockSpec((TOP_K, tm), lambda i: (0, i)),
                   pl.BlockSpec((TOP_K, tm), lambda i: (0, i)),
                   row(LANES),
                   pl.BlockSpec((N_EXPERTS, LANES), lambda i: (0, 0))],
        out_shape=[jax.ShapeDtypeStruct((t * ROW_TILE, LANES), F32),
                   jax.ShapeDtypeStruct((TOP_K, t), jnp.int32),
                   jax.ShapeDtypeStruct((TOP_K, t), jnp.int32),
                   jax.ShapeDtypeStruct((t, LANES), F32),
                   jax.ShapeDtypeStruct((N_EXPERTS, LANES), F32)],
        scratch_shapes=[pltpu.VMEM((N_EXPERTS, LANES), F32)],
        compiler_params=_cparams(("arbitrary",), 40),
        name="moe_route",
    )(x2, *ys, w_out.astype(BF16), ffn_norm.reshape(1, -1), w_router.T)


def _row_copy(src_ref, src_row, dst_ref, dst_row, sem):
    tile = lambda row: pl.ds(row * ROW_TILE if isinstance(row, int) else pl.multiple_of(row * ROW_TILE, ROW_TILE),
                             ROW_TILE)
    return pltpu.make_async_copy(src_ref.at[tile(src_row), :], dst_ref.at[tile(dst_row), :], sem)


def _dispatch_kernel(plo_ref, pn_ref, dest_hbm, x_ref, out_hbm, idx_ref, zero_ref, idx_sem, row_sem, pad_sem):
    tm = ROUTE_TM
    load = pltpu.make_async_copy(dest_hbm.at[pl.program_id(0)], idx_ref, idx_sem)
    load.start()

    @pl.when(pl.program_id(0) == 0)
    def _():
        zero_ref[...] = jnp.zeros_like(zero_ref)
        for e in range(N_EXPERTS):
            def fill(r, carry, e=e):
                _row_copy(zero_ref, 0, out_hbm, plo_ref[e] + r, pad_sem).start()
                return carry
            lax.fori_loop(0, pn_ref[e], fill, 0)
        for e in range(N_EXPERTS):
            def done(r, carry, e=e):
                _row_copy(zero_ref, 0, out_hbm, plo_ref[e] + r, pad_sem).wait()
                return carry
            lax.fori_loop(0, pn_ref[e], done, 0)
        first_free = (plo_ref[N_EXPERTS - 1] + pn_ref[N_EXPERTS - 1]) // MOE_TM

        block_rows = MOE_TM * ROW_TILE

        def block_copy(b):
            return pltpu.make_async_copy(
                zero_ref, out_hbm.at[pl.ds(pl.multiple_of(b * block_rows, block_rows), block_rows), :], pad_sem)

        def fill_block(b, carry):
            block_copy(b).start()
            return carry

        def done_block(b, carry):
            block_copy(b).wait()
            return carry

        lax.fori_loop(first_free, out_hbm.shape[0] // block_rows, fill_block, 0)
        lax.fori_loop(first_free, out_hbm.shape[0] // block_rows, done_block, 0)

    load.wait()
    for t in range(tm):
        for k in range(TOP_K):
            _row_copy(x_ref, t, out_hbm, idx_ref[k * tm + t], row_sem).start(priority=k)
    for t in range(tm):
        for k in range(TOP_K):
            _row_copy(x_ref, t, out_hbm, idx_ref[k * tm + t], row_sem).wait()


def _moe_dispatch(xt, dest2, pad_lo, pad_n, n_rows):
    t = xt.shape[0] // ROW_TILE
    tm = ROUTE_TM
    grid_spec = pltpu.PrefetchScalarGridSpec(
        num_scalar_prefetch=2,
        grid=(t // tm,),
        in_specs=[pl.BlockSpec(memory_space=pl.ANY),
                  pl.BlockSpec((tm * ROW_TILE, LANES), lambda i, lo, n: (i, 0))],
        out_specs=pl.BlockSpec(memory_space=pl.ANY),
        scratch_shapes=[pltpu.SMEM((TOP_K * tm,), jnp.int32), pltpu.VMEM((MOE_TM * ROW_TILE, LANES), F32),
                        pltpu.SemaphoreType.DMA, pltpu.SemaphoreType.DMA, pltpu.SemaphoreType.DMA],
    )
    return pl.pallas_call(
        _dispatch_kernel,
        grid_spec=grid_spec,
        out_shape=jax.ShapeDtypeStruct((n_rows * ROW_TILE, LANES), F32),
        compiler_params=_cparams(("arbitrary",), 32),
        name="moe_dispatch",
    )(pad_lo, pad_n, dest2, xt)


def _combine_kernel(dest_hbm, x_ref, gc_ref, fn_ref, y_hbm, o_ref, idx0_ref, idx1_ref, ybuf_ref, idx_sem, row_sem):
    tm = ROUTE_TM
    i, n = pl.program_id(0), pl.num_programs(0)
    idx_refs = (idx0_ref, idx1_ref)

    def gather(idx_ref, slot, t, k):
        return _row_copy(y_hbm, idx_ref[k * tm + t], ybuf_ref.at[slot, k], t, row_sem.at[slot])

    @pl.when(i == 0)
    def _():
        first = pltpu.make_async_copy(dest_hbm.at[0], idx0_ref, idx_sem.at[0])
        first.start()
        first.wait()

        def issue(t, carry):
            for k in range(TOP_K):
                gather(idx0_ref, 0, t, k).start()
            return carry

        lax.fori_loop(0, tm, issue, 0)

    def step(slot):
        nxt = 1 - slot
        load = pltpu.make_async_copy(dest_hbm.at[jnp.minimum(i + 1, n - 1)], idx_refs[nxt], idx_sem.at[nxt])
        load.start()
        for t in range(tm):
            for k in range(TOP_K):
                gather(idx_refs[slot], slot, t, k).wait()
        load.wait()
        for t in range(tm):
            for k in range(TOP_K):
                gather(idx_refs[nxt], nxt, t, k).start(priority=k)
        gc = gc_ref[...]
        x = (_tiles_to_rows(x_ref, tm) + gc[:, 0:1] * _tiles_to_rows(ybuf_ref.at[slot, 0], tm)
             + gc[:, 1:2] * _tiles_to_rows(ybuf_ref.at[slot, 1], tm))
        o_ref[...] = _rms(x, fn_ref[...])

        @pl.when(i == n - 1)
        def _():
            for t in range(tm):
                for k in range(TOP_K):
                    gather(idx_refs[nxt], nxt, t, k).wait()

    for slot in range(2):
        pl.when(i % 2 == slot)(functools.partial(step, slot))


def _moe_combine_norm(xt, y_buf, dest2, gcol, final_norm):
    t = xt.shape[0] // ROW_TILE
    tm = ROUTE_TM
    return pl.pallas_call(
        _combine_kernel,
        grid=(t // tm,),
        in_specs=[pl.BlockSpec(memory_space=pl.ANY),
                  pl.BlockSpec((tm * ROW_TILE, LANES), lambda i: (i, 0)),
                  pl.BlockSpec((tm, LANES), lambda i: (i, 0)),
                  pl.BlockSpec((1, D_MODEL), lambda i: (0, 0)),
                  pl.BlockSpec(memory_space=pl.ANY)],
        out_specs=pl.BlockSpec((tm, D_MODEL), lambda i: (i, 0)),
        out_shape=jax.ShapeDtypeStruct((t, D_MODEL), F32),
        scratch_shapes=[pltpu.SMEM((TOP_K * tm,), jnp.int32), pltpu.SMEM((TOP_K * tm,), jnp.int32),
                        pltpu.VMEM((2, TOP_K, tm * ROW_TILE, LANES), F32),
                        pltpu.SemaphoreType.DMA((2,)), pltpu.SemaphoreType.DMA((2,))],
        compiler_params=_cparams(("arbitrary",), 40),
        name="moe_combine_norm",
    )(dest2, xt, gcol, final_norm.reshape(1, -1), y_buf)


def _moe_layer(x2, ys, w_out, ffn_norm, w_router, w_gate, w_up, w_down, final_norm):
    t = x2.shape[0]
    tm = ROUTE_TM
    xt, e01, r01, gcol, cnt = _moe_route(x2, ys, w_out, ffn_norm, w_router)
    counts = cnt[:, 0].astype(jnp.int32)
    padded = (counts + MOE_TM - 1) // MOE_TM * MOE_TM
    pad_ends = jnp.cumsum(padded)
    pad_starts = pad_ends - padded
    n_blocks = -(-t * TOP_K // MOE_TM) + N_EXPERTS
    blk_start = jnp.arange(n_blocks, dtype=jnp.int32) * MOE_TM
    block_e = jnp.minimum(jnp.sum(blk_start[:, None] >= pad_ends[None, :], axis=1), N_EXPERTS - 1).astype(jnp.int32)
    n_used = (pad_ends[-1] // MOE_TM).astype(jnp.int32).reshape(1)
    dest = r01 + jnp.sum(jnp.where(e01[None] == jnp.arange(N_EXPERTS)[:, None, None], pad_starts[:, None, None], 0),
                         axis=0)
    dest2 = dest.reshape(TOP_K, t // tm, tm).transpose(1, 0, 2).reshape(t // tm, TOP_K * tm)
    xs = _moe_dispatch(xt, dest2, pad_starts + counts, padded - counts, n_blocks * MOE_TM)
    y_buf = _moe_experts(xs, ffn_norm, block_e, n_used, w_gate.astype(BF16), w_up.astype(BF16),
                         w_down.astype(BF16))
    return _moe_combine_norm(xt, y_buf, dest2, gcol, final_norm)


def _permute_w_in(w_in):
    s5 = w_in[:, 0:256]
    gm = w_in[:, 256:768]
    qkv = w_in[:, 768:1536]
    z = w_in[:, 1536:1792]
    a = w_in[:, 1792:1796]
    b = w_in[:, 1796:1800]
    sc = w_in[:, 1800:2568]
    ab = jnp.zeros((D_MODEL, LANES), F32).at[:, 0:GDN_HEADS].set(a).at[:, AB_B_LANE:AB_B_LANE + GDN_HEADS].set(b)
    return jnp.concatenate([gm, s5, z, qkv, sc, ab], axis=1).astype(BF16)


def kernel(x, mix_norm, w_in, s5_lam_re, s5_lam_im, s5_log_step, s5_b_re, s5_b_im, s5_c_re, s5_c_im, s5_d, s5_w_glu, s5_b_glu, s5_out_norm, sgu_ln_g, sgu_ln_b, sgu_w, sgu_b, gmlp_out_norm, gdn_conv, gdn_a_log, gdn_dt_bias, gdn_norm, sc_conv, sc_out_norm, w_out, ffn_norm, ffn_w_gate, ffn_w_up, ffn_w_down, moe_router, moe_w_gate, moe_w_up, moe_w_down, final_norm):
    bsz, seqlen, d = x.shape
    t = bsz * seqlen
    x2 = x.reshape(t, d)
    out = None
    for l in range(DEPTH):
        p2 = _in_proj(x2, mix_norm[l].reshape(1, -1), _permute_w_in(w_in[l]))
        p3 = p2.reshape(bsz, seqlen, P_COLS)
        wb, wc, pw = _s5_params(s5_lam_re[l], s5_lam_im[l], s5_log_step[l], s5_b_re[l], s5_b_im[l],
                                s5_c_re[l], s5_c_im[l])
        y_s5 = _s5_mixer(p3, wb, wc, pw, s5_d[l], s5_w_glu[l], s5_b_glu[l], s5_out_norm[l])
        y_gm = _gmlp_mixer(p3, sgu_ln_g[l], sgu_ln_b[l], sgu_w[l], sgu_b[l], gmlp_out_norm[l])
        y_gdn = _gdn_mixer(p3, gdn_conv[l], gdn_a_log[l], gdn_dt_bias[l], gdn_norm[l])
        y_sc = _shortconv_mixer(p3, sc_conv[l], sc_out_norm[l])
        ys = [y.reshape(t, BRANCH_W) for y in (y_s5, y_gm, y_gdn, y_sc)]
        i = l // 2
        if l % 2 == 0:
            x2 = _ffn_dense(x2, ys, w_out[l], ffn_norm[l], ffn_w_gate[i], ffn_w_up[i], ffn_w_down[i])
        else:
            out = _moe_layer(x2, ys, w_out[l], ffn_norm[l], moe_router[i], moe_w_gate[i], moe_w_up[i],
                             moe_w_down[i], final_norm)
    return out.reshape(bsz, seqlen, d)
```

```python
import functools
import math

import jax
import jax.numpy as jnp
from jax import lax
from jax.experimental import pallas as pl
from jax.experimental.pallas import tpu as pltpu

F32 = jnp.float32
BF16 = jnp.bfloat16

D_MODEL = 1024
DEPTH = 2
BRANCH_W = 256
S5_GROUP = 16
S5_GROUPS = 16
S5_STATE = 64
S5_NSTATE = S5_GROUPS * S5_STATE
GMLP_HEADS = 4
GMLP_HEAD_DIM = 64
GMLP_CHUNK = 128
GDN_HEAD_DIM = 64
GDN_HEADS = 4
GDN_CONV = 4
GDN_CHUNK = 64
SC_CONV = 3
D_FF = 2816
N_EXPERTS = 8
TOP_K = 2
D_FF_EXPERT = 3584
EPS = 1e-6

LANES = 128
SUBLANES = 8

COL_GM = 0
COL_S5 = 512
COL_Z = 768
COL_Q = 1024
COL_K = 1280
COL_V = 1536
COL_SCB = 1792
COL_SCC = 2048
COL_SCX = 2304
COL_AB = 2560
P_COLS = 2688
AB_B_LANE = 64

IN_TM = 1024
S5_CHUNK = 128
S5_BATCH = 4
S5_SEG = S5_CHUNK // SUBLANES
S5_POW_ROW = 0
S5_SEG_ROW = S5_POW_ROW + S5_SEG
S5_CARRY_ROW = S5_SEG_ROW + 3 * SUBLANES
S5_TABLE_ROWS = S5_CARRY_ROW + SUBLANES
GM_TILE = 1024
SC_TILE = 2048
GDN_TILE = 128
GDN_BATCH = 4
FFN_TM = 512
FFN_TF = 256
MOE_TM = 512
MOE_TF = 512
ROUTE_TM = 512
ROW_TILE = D_MODEL // LANES


def _cparams(sem, vmem_mb):
    return pltpu.CompilerParams(dimension_semantics=sem, vmem_limit_bytes=vmem_mb * 1024 * 1024)


def _rms(x, g):
    return x * lax.rsqrt(jnp.mean(x * x, axis=-1, keepdims=True) + EPS) * g


def _silu(x):
    return x * jax.nn.sigmoid(x)


def _bdot(a, b):
    return jnp.dot(a.astype(BF16), b.astype(BF16), preferred_element_type=F32)


def _split_dot(a, b, passes, data_on_left):
    data = a if data_on_left else b
    acc = None
    for _ in range(passes):
        piece = data.astype(BF16)
        term = (jnp.dot(piece, b, preferred_element_type=F32) if data_on_left
                else jnp.dot(a, piece, preferred_element_type=F32))
        acc = term if acc is None else acc + term
        data = data - piece.astype(F32)
    return acc


def _bdot_nt(a, b):
    return lax.dot_general(a.astype(BF16), b.astype(BF16), (((1,), (1,)), ((), ())),
                           preferred_element_type=F32)


def _in_proj_kernel(x_ref, g_ref, w_ref, o_ref):
    h = _rms(x_ref[...], g_ref[...])
    o_ref[...] = _bdot(h, w_ref[...])


def _in_proj(x2, g, w):
    t = x2.shape[0]
    return pl.pallas_call(
        _in_proj_kernel,
        grid=(t // IN_TM,),
        in_specs=[pl.BlockSpec((IN_TM, D_MODEL), lambda i: (i, 0)),
                  pl.BlockSpec((1, D_MODEL), lambda i: (0, 0)),
                  pl.BlockSpec((D_MODEL, P_COLS), lambda i: (0, 0))],
        out_specs=pl.BlockSpec((IN_TM, P_COLS), lambda i: (i, 0)),
        out_shape=jax.ShapeDtypeStruct((t, P_COLS), F32),
        compiler_params=_cparams(("parallel",), 48),
        name="in_proj",
    )(x2, g, w)


def _s5_kernel(u0_ref, u1_ref, wb_ref, pw_ref, wc_ref, d_ref, wglu_ref, bglu_ref, on_ref, o_ref,
               sr_ref, si_ref, ys_ref):
    @pl.when(pl.program_id(1) == 0)
    def _():
        sr_ref[...] = jnp.zeros_like(sr_ref)
        si_ref[...] = jnp.zeros_like(si_ref)

    seg, c = S5_SEG, S5_CHUNK
    u = jnp.concatenate(
        [jnp.concatenate([jnp.concatenate([ref[bb, pl.ds(i, SUBLANES, stride=seg), :] for i in range(seg)], axis=0)
                          for ref in (u0_ref, u1_ref)], axis=1) for bb in range(S5_BATCH)], axis=0)
    bu = _bdot(u, wb_ref[...])
    xs = [_s5_scan(bu[bb * c:(bb + 1) * c], pw_ref, sr_ref.at[bb], si_ref.at[bb]) for bb in range(S5_BATCH)]
    x = jnp.concatenate(xs, axis=0)
    y = _bdot(x, wc_ref[...])
    y = y + d_ref[...] * u
    y = jax.nn.gelu(y)
    y = y * jax.nn.sigmoid(_bdot(y, wglu_ref[...]) + bglu_ref[...])
    y = _rms(y, on_ref[...])
    for bb in range(S5_BATCH):
        for slab in range(BRANCH_W // LANES):
            ys_ref[bb, slab] = y[bb * c:(bb + 1) * c, slab * LANES:(slab + 1) * LANES]
        for r in range(c // SUBLANES):
            start = (c // 2) * (r % 2) + r // 2
            for slab in range(BRANCH_W // LANES):
                o_ref[bb, r * SUBLANES:(r + 1) * SUBLANES, slab * LANES:(slab + 1) * LANES] = (
                    ys_ref[bb, slab, pl.ds(start, SUBLANES, stride=SUBLANES), :])


def _s5_scan(bu, pw_ref, sr_ref, si_ref):
    n, seg = S5_NSTATE, S5_SEG
    cmul = lambda ar, ai, xr, xi: (ar * xr - ai * xi, ar * xi + ai * xr)
    a_r, a_i = pw_ref[S5_POW_ROW:S5_POW_ROW + 1, :n], pw_ref[S5_POW_ROW:S5_POW_ROW + 1, n:]
    xr, xi = bu[0:SUBLANES, :n], bu[0:SUBLANES, n:]
    xrs, xis = [xr], [xi]
    for i in range(1, seg):
        rows = slice(i * SUBLANES, (i + 1) * SUBLANES)
        pr, pi = cmul(a_r, a_i, xr, xi)
        xr, xi = pr + bu[rows, :n], pi + bu[rows, n:]
        xrs.append(xr)
        xis.append(xi)
    fr, fi = xr, xi
    for s in range(int(math.log2(SUBLANES))):
        rows = slice(S5_SEG_ROW + s * SUBLANES, S5_SEG_ROW + (s + 1) * SUBLANES)
        pr, pi = cmul(pw_ref[rows, :n], pw_ref[rows, n:], pltpu.roll(fr, 1 << s, 0), pltpu.roll(fi, 1 << s, 0))
        fr, fi = fr + pr, fi + pi
    sbr = jnp.broadcast_to(sr_ref[...], (SUBLANES, n))
    sbi = jnp.broadcast_to(si_ref[...], (SUBLANES, n))
    rows = slice(S5_CARRY_ROW, S5_CARRY_ROW + SUBLANES)
    pr, pi = cmul(pw_ref[rows, :n], pw_ref[rows, n:], sbr, sbi)
    fr, fi = fr + pr, fi + pi
    sr_ref[...] = fr[SUBLANES - 1:, :]
    si_ref[...] = fi[SUBLANES - 1:, :]
    first = lax.broadcasted_iota(jnp.int32, (SUBLANES, n), 0) == 0
    cin_r = jnp.where(first, sbr, pltpu.roll(fr, 1, 0))
    cin_i = jnp.where(first, sbi, pltpu.roll(fi, 1, 0))
    for i in range(seg):
        row = S5_POW_ROW + i
        pr, pi = cmul(pw_ref[row:row + 1, :n], pw_ref[row:row + 1, n:], cin_r, cin_i)
        xrs[i], xis[i] = xrs[i] + pr, xis[i] + pi
    return jnp.concatenate([jnp.concatenate(xrs, axis=0), jnp.concatenate(xis, axis=0)], axis=1)


def _s5_params(lam_re, lam_im, log_step, b_re, b_im, c_re, c_im):
    g, n, p = S5_GROUPS, S5_STATE, S5_GROUP
    dt = jnp.exp(log_step)[:, None]
    mag = jnp.exp(lam_re * dt)
    ar, ai = mag * jnp.cos(lam_im * dt), mag * jnp.sin(lam_im * dt)
    den = lam_re * lam_re + lam_im * lam_im
    fr = ((ar - 1.0) * lam_re + ai * lam_im) / den
    fi = (ai * lam_re - (ar - 1.0) * lam_im) / den
    bbr = fr[..., None] * b_re - fi[..., None] * b_im
    bbi = fr[..., None] * b_im + fi[..., None] * b_re
    eye = jnp.eye(g, dtype=F32)
    wbr = jnp.einsum('gnp,gh->gphn', bbr, eye).reshape(g * p, g * n)
    wbi = jnp.einsum('gnp,gh->gphn', bbi, eye).reshape(g * p, g * n)
    wb = jnp.concatenate([wbr, wbi], axis=1)
    wcr = jnp.einsum('gpn,gh->gnhp', c_re, eye).reshape(g * n, g * p)
    wci = jnp.einsum('gpn,gh->gnhp', -c_im, eye).reshape(g * n, g * p)
    wc = jnp.concatenate([wcr, wci], axis=0)
    def powers(br, bi, count):
        rows_r, rows_i = [br], [bi]
        for _ in range(count - 1):
            qr, qi = rows_r[-1], rows_i[-1]
            rows_r.append(qr * br - qi * bi)
            rows_i.append(qr * bi + qi * br)
        return jnp.concatenate(rows_r, axis=0), jnp.concatenate(rows_i, axis=0)

    a1r, a1i = ar.reshape(1, g * n), ai.reshape(1, g * n)
    pos_r, pos_i = powers(a1r, a1i, S5_SEG)
    a16r, a16i = pos_r[S5_SEG - 1:], pos_i[S5_SEG - 1:]
    r_idx = jnp.arange(SUBLANES)[:, None]
    seg_r, seg_i = [], []
    pr, pi = a16r, a16i
    for s in range(int(math.log2(SUBLANES))):
        keep = r_idx >= (1 << s)
        seg_r.append(jnp.where(keep, pr, 0.0))
        seg_i.append(jnp.where(keep, pi, 0.0))
        pr, pi = pr * pr - pi * pi, 2.0 * pr * pi
    car_r, car_i = powers(a16r, a16i, SUBLANES)
    pw = jnp.concatenate([jnp.concatenate([pos_r] + seg_r + [car_r], axis=0),
                          jnp.concatenate([pos_i] + seg_i + [car_i], axis=0)], axis=1)
    return wb.astype(BF16), wc.astype(BF16), pw


def _s5_mixer(p3, wb, wc, pw, d_skip, w_glu, b_glu, out_norm):
    b, l, _ = p3.shape
    n2 = 2 * S5_NSTATE
    const = lambda shape: pl.BlockSpec(shape, lambda i, j: (0,) * len(shape))
    return pl.pallas_call(
        _s5_kernel,
        grid=(b // S5_BATCH, l // S5_CHUNK),
        in_specs=[pl.BlockSpec((S5_BATCH, S5_CHUNK, LANES), lambda i, j: (i, j, COL_S5 // LANES)),
                  pl.BlockSpec((S5_BATCH, S5_CHUNK, LANES), lambda i, j: (i, j, COL_S5 // LANES + 1)),
                  const((BRANCH_W, n2)), const((S5_TABLE_ROWS, n2)), const((n2, BRANCH_W)),
                  const((1, BRANCH_W)), const((BRANCH_W, BRANCH_W)), const((1, BRANCH_W)),
                  const((1, BRANCH_W))],
        out_specs=pl.BlockSpec((S5_BATCH, S5_CHUNK, BRANCH_W), lambda i, j: (i, j, 0)),
        out_shape=jax.ShapeDtypeStruct((b, l, BRANCH_W), F32),
        scratch_shapes=[pltpu.VMEM((S5_BATCH, 1, S5_NSTATE), F32), pltpu.VMEM((S5_BATCH, 1, S5_NSTATE), F32),
                        pltpu.VMEM((S5_BATCH, BRANCH_W // LANES, S5_CHUNK, LANES), F32)],
        compiler_params=_cparams(("parallel", "arbitrary"), 40),
        name="s5_mixer",
    )(p3, p3, wb, pw, wc, d_skip.reshape(1, -1), w_glu.astype(BF16), b_glu.reshape(1, -1),
      out_norm.reshape(1, -1))


def _gmlp_kernel(p_ref, lng_ref, lnb_ref, w_ref, bias_ref, on_ref, o_ref):
    z = jax.nn.gelu(p_ref[...])
    u, v = z[:, :BRANCH_W], z[:, BRANCH_W:]
    vc = v - jnp.mean(v, axis=-1, keepdims=True)
    v = vc * lax.rsqrt(jnp.mean(vc * vc, axis=-1, keepdims=True) + EPS) * lng_ref[...] + lnb_ref[...]
    ti = lax.broadcasted_iota(jnp.int32, (GMLP_CHUNK, GMLP_CHUNK), 0)
    si = lax.broadcasted_iota(jnp.int32, (GMLP_CHUNK, GMLP_CHUNK), 1)
    tril = ti >= si
    ws = jnp.concatenate([jnp.where(tril, w_ref[h], 0.0) for h in range(GMLP_HEADS)], axis=0).astype(BF16)
    lane = lax.broadcasted_iota(jnp.int32, (GMLP_CHUNK, BRANCH_W), 1)
    bias = bias_ref[...]
    outs = []
    for c in range(GM_TILE // GMLP_CHUNK):
        vb = v[c * GMLP_CHUNK:(c + 1) * GMLP_CHUNK, :].astype(BF16)
        mixed = jnp.dot(ws, vb, preferred_element_type=F32)
        s = mixed[(GMLP_HEADS - 1) * GMLP_CHUNK:]
        for h in range(GMLP_HEADS - 2, -1, -1):
            s = jnp.where(lane < (h + 1) * GMLP_HEAD_DIM, mixed[h * GMLP_CHUNK:(h + 1) * GMLP_CHUNK], s)
        outs.append(s + bias)
    s = jnp.concatenate(outs, axis=0)
    o_ref[...] = _rms(u * s, on_ref[...])


def _gmlp_mixer(p3, ln_g, ln_b, w_sp, b_sp, out_norm):
    b, l, _ = p3.shape
    bias = jnp.repeat(b_sp.T, GMLP_HEAD_DIM, axis=1)
    const = lambda shape: pl.BlockSpec(shape, lambda i, j: (0,) * len(shape))
    return pl.pallas_call(
        _gmlp_kernel,
        grid=(b, l // GM_TILE),
        in_specs=[pl.BlockSpec((None, GM_TILE, 2 * BRANCH_W), lambda i, j: (i, j, COL_GM // (2 * BRANCH_W))),
                  const((1, BRANCH_W)), const((1, BRANCH_W)),
                  const((GMLP_HEADS, GMLP_CHUNK, GMLP_CHUNK)), const((GMLP_CHUNK, BRANCH_W)),
                  const((1, BRANCH_W))],
        out_specs=pl.BlockSpec((None, GM_TILE, BRANCH_W), lambda i, j: (i, j, 0)),
        out_shape=jax.ShapeDtypeStruct((b, l, BRANCH_W), F32),
        compiler_params=_cparams(("parallel", "parallel"), 32),
        name="gmlp_mixer",
    )(p3, ln_g.reshape(1, -1), ln_b.reshape(1, -1), w_sp, bias, out_norm.reshape(1, -1))


def _shortconv_kernel(b_ref, c_ref, x_ref, w_ref, on_ref, o_ref, halo_ref):
    @pl.when(pl.program_id(1) == 0)
    def _():
        halo_ref[...] = jnp.zeros_like(halo_ref)

    cx = c_ref[...] * x_ref[...]
    ext = jnp.concatenate([halo_ref[...], cx], axis=0)
    halo_ref[...] = cx[SC_TILE - SUBLANES:, :]
    y = w_ref[SC_CONV - 1:SC_CONV, :] * cx
    for j in range(SC_CONV - 1):
        sh = SC_CONV - 1 - j
        y = y + w_ref[j:j + 1, :] * pltpu.roll(ext, sh, 0)[SUBLANES:, :]
    o_ref[...] = _rms(b_ref[...] * y, on_ref[...])


def _shortconv_mixer(p3, conv_w, out_norm):
    b, l, _ = p3.shape
    w = jnp.concatenate([conv_w, jnp.zeros((SUBLANES - SC_CONV, BRANCH_W), F32)], axis=0)
    col = lambda c: pl.BlockSpec((None, SC_TILE, BRANCH_W), lambda i, j: (i, j, c // BRANCH_W))
    const = lambda shape: pl.BlockSpec(shape, lambda i, j: (0,) * len(shape))
    return pl.pallas_call(
        _shortconv_kernel,
        grid=(b, l // SC_TILE),
        in_specs=[col(COL_SCB), col(COL_SCC), col(COL_SCX), const((SUBLANES, BRANCH_W)), const((1, BRANCH_W))],
        out_specs=pl.BlockSpec((None, SC_TILE, BRANCH_W), lambda i, j: (i, j, 0)),
        out_shape=jax.ShapeDtypeStruct((b, l, BRANCH_W), F32),
        scratch_shapes=[pltpu.VMEM((SUBLANES, BRANCH_W), F32)],
        compiler_params=_cparams(("parallel", "arbitrary"), 32),
        name="shortconv_mixer",
    )(p3, p3, p3, w, out_norm.reshape(1, -1))


def _gdn_kernel(q_ref, k_ref, v_ref, z_ref, ab_ref, cw_ref, alog_ref, dtb_ref, ng_ref, o_ref,
                halo_ref, state_ref):
    c_len, h_dim, nh, w = GDN_CHUNK, GDN_HEAD_DIM, GDN_HEADS, BRANCH_W
    n_chunks = GDN_TILE // c_len

    @pl.when(pl.program_id(1) == 0)
    def _():
        halo_ref[...] = jnp.zeros_like(halo_ref)
        state_ref[...] = jnp.zeros_like(state_ref)

    shift = int(math.log2(h_dim))
    r256 = lax.broadcasted_iota(jnp.int32, (w, w), 0)
    c256 = lax.broadcasted_iota(jnp.int32, (w, w), 1)
    same_head = (r256 >> shift) == (c256 >> shift)
    ebd = jnp.where(same_head, 1.0, 0.0).astype(BF16)
    causal = same_head & (r256 >= c256)
    strict = same_head & (r256 > c256)
    eye = jnp.where(r256 == c256, 1.0, 0.0)
    r128 = lax.broadcasted_iota(jnp.int32, (LANES, w), 0)
    c128 = lax.broadcasted_iota(jnp.int32, (LANES, w), 1)
    exp_a = jnp.where(r128 == (c128 >> shift), 1.0, 0.0).astype(BF16)
    exp_b = jnp.where(r128 == (c128 >> shift) + AB_B_LANE, 1.0, 0.0).astype(BF16)

    def stack(a):
        lane_head = (lax.broadcasted_iota(jnp.int32, a.shape, 1) >> shift) & (nh - 1)
        return jnp.concatenate([jnp.where(lane_head == h, a, 0.0) for h in range(nh)], axis=0)

    def unstack(a):
        return a[0:c_len] + a[c_len:2 * c_len] + a[2 * c_len:3 * c_len] + a[3 * c_len:4 * c_len]

    def conv_silu(ref, bb, col):
        cur = ref[bb]
        ext = jnp.concatenate([halo_ref[bb, :, col * w:(col + 1) * w], cur], axis=0)
        halo_ref[bb, :, col * w:(col + 1) * w] = cur[GDN_TILE - SUBLANES:, :]
        y = cw_ref[GDN_CONV - 1:GDN_CONV, col * w:(col + 1) * w] * cur
        for j in range(GDN_CONV - 1):
            sh = GDN_CONV - 1 - j
            y = y + cw_ref[j:j + 1, col * w:(col + 1) * w] * pltpu.roll(ext, sh, 0)[SUBLANES:, :]
        return _silu(y)

    rt = lax.broadcasted_iota(jnp.int32, (GDN_TILE, GDN_TILE), 0)
    ct = lax.broadcasted_iota(jnp.int32, (GDN_TILE, GDN_TILE), 1)
    ltri = jnp.where(((rt >> shift) == (ct >> shift)) & (rt >= ct), 1.0, 0.0).astype(BF16)
    tile4 = lambda a: jnp.concatenate([a] * nh, axis=0)

    q_decs, k_decs, g_tots, rhss, a_mats, qkds = [], [], [], [], [], []
    for bb in range(GDN_BATCH):
        q = conv_silu(q_ref, bb, 0)
        k = conv_silu(k_ref, bb, 1)
        v = conv_silu(v_ref, bb, 2)
        q = q * lax.rsqrt(jnp.dot((q * q).astype(BF16), ebd, preferred_element_type=F32) + EPS) * (h_dim ** -0.5)
        k = k * lax.rsqrt(jnp.dot((k * k).astype(BF16), ebd, preferred_element_type=F32) + EPS)
        ab = ab_ref[bb]
        beta = jax.nn.sigmoid(ab)
        xa = ab + dtb_ref[...]
        softplus = jnp.maximum(xa, 0.0) + jnp.log(1.0 + jnp.exp(-jnp.abs(xa)))
        g = -jnp.exp(alog_ref[...]) * softplus
        gcum = _split_dot(ltri, g, 3, data_on_left=False)
        gexp = _split_dot(gcum, exp_a, 2, data_on_left=True)
        bexp = _split_dot(beta, exp_b, 2, data_on_left=True)
        for c in range(n_chunks):
            sl = slice(c * c_len, (c + 1) * c_len)
            qc, kc, vc, gc, bc = q[sl], k[sl], v[sl], gexp[sl], bexp[sl]
            glast = gc[c_len - 1:c_len, :]
            eg = jnp.exp(gc)
            q_decs.append(qc * eg)
            k_decs.append(kc * jnp.exp(glast - gc))
            g_tots.append(jnp.exp(glast))
            kst = stack(kc).astype(BF16)
            kk = _bdot_nt(kst, kst)
            qk = _bdot_nt(stack(qc), kst)
            gcol = stack(gc)
            bcol = stack(bc)
            dec = jnp.exp(jnp.where(causal, gcol - gcol.T, -jnp.inf))
            a_mats.append(jnp.where(strict, bcol * kk * dec, 0.0))
            qkds.append((qk * dec).astype(BF16))
            rhss.append(jnp.concatenate([tile4(vc) * bcol, tile4(kc * eg) * bcol], axis=1).astype(BF16))
    t_invs = [eye - a for a in a_mats]
    pws = a_mats
    for _ in range(int(math.log2(c_len)) - 1):
        pws = [_bdot(p, p) for p in pws]
        t_invs = [t + _bdot(t, p) for t, p in zip(t_invs, pws)]
    sols = [unstack(_bdot(t, r)) for t, r in zip(t_invs, rhss)]
    states = [state_ref[bb] for bb in range(GDN_BATCH)]
    outs = [[] for _ in range(GDN_BATCH)]
    for c in range(n_chunks):
        for bb in range(GDN_BATCH):
            i = bb * n_chunks + c
            w_v, w_k = sols[i][:, :w], sols[i][:, w:]
            v_new = w_v - _bdot(w_k, states[bb])
            o = _bdot(q_decs[i], states[bb]) + unstack(_bdot(qkds[i], stack(v_new)))
            states[bb] = states[bb] * g_tots[i] + jnp.where(same_head, _bdot(k_decs[i].T, v_new), 0.0)
            outs[bb].append(o)
    for bb in range(GDN_BATCH):
        state_ref[bb] = states[bb]
        o = jnp.concatenate(outs[bb], axis=0)
        ms = jnp.dot((o * o).astype(BF16), ebd, preferred_element_type=F32) * (1.0 / h_dim)
        o = o * lax.rsqrt(ms + EPS) * ng_ref[...]
        o_ref[bb] = o * _silu(z_ref[bb])


def _gdn_mixer(p3, conv_w, a_log, dt_bias, norm_g):
    b, l, _ = p3.shape
    cw = jnp.concatenate([conv_w, jnp.zeros((SUBLANES - GDN_CONV, 3 * BRANCH_W), F32)], axis=0)
    lane_row = lambda vec: jnp.zeros((1, LANES), F32).at[0, :GDN_HEADS].set(vec)
    col = lambda c: pl.BlockSpec((GDN_BATCH, GDN_TILE, BRANCH_W), lambda i, j: (i, j, c // BRANCH_W))
    const = lambda shape: pl.BlockSpec(shape, lambda i, j: (0,) * len(shape))
    return pl.pallas_call(
        _gdn_kernel,
        grid=(b // GDN_BATCH, l // GDN_TILE),
        in_specs=[col(COL_Q), col(COL_K), col(COL_V), col(COL_Z),
                  pl.BlockSpec((GDN_BATCH, GDN_TILE, LANES), lambda i, j: (i, j, COL_AB // LANES)),
                  const((SUBLANES, 3 * BRANCH_W)), const((1, LANES)), const((1, LANES)), const((1, BRANCH_W))],
        out_specs=pl.BlockSpec((GDN_BATCH, GDN_TILE, BRANCH_W), lambda i, j: (i, j, 0)),
        out_shape=jax.ShapeDtypeStruct((b, l, BRANCH_W), F32),
        scratch_shapes=[pltpu.VMEM((GDN_BATCH, SUBLANES, 3 * BRANCH_W), F32),
                        pltpu.VMEM((GDN_BATCH, BRANCH_W, BRANCH_W), F32)],
        compiler_params=_cparams(("parallel", "arbitrary"), 48),
        name="gdn_mixer",
    )(p3, p3, p3, p3, p3, cw, lane_row(a_log), lane_row(dt_bias), jnp.tile(norm_g, GDN_HEADS).reshape(1, -1))


def _mix_residual(x_ref, y_refs, wo_ref):
    acc = x_ref[...]
    for i, y_ref in enumerate(y_refs):
        acc = acc + _bdot(y_ref[...], wo_ref[i * BRANCH_W:(i + 1) * BRANCH_W, :])
    return acc


def _ffn_kernel(x_ref, y0_ref, y1_ref, y2_ref, y3_ref, wo_ref, g_ref, wg_ref, wu_ref, wd_ref, o_ref):
    x = _mix_residual(x_ref, (y0_ref, y1_ref, y2_ref, y3_ref), wo_ref)
    hb = _rms(x, g_ref[...]).astype(BF16)
    acc = None
    for f in range(D_FF // FFN_TF):
        cols = slice(f * FFN_TF, (f + 1) * FFN_TF)
        a = jnp.dot(hb, wg_ref[:, cols], preferred_element_type=F32)
        u = jnp.dot(hb, wu_ref[:, cols], preferred_element_type=F32)
        y = jnp.dot((_silu(a) * u).astype(BF16), wd_ref[cols, :], preferred_element_type=F32)
        acc = y if acc is None else acc + y
    o_ref[...] = x + acc


def _ffn_dense(x2, ys, w_out, g, w_gate, w_up, w_down):
    t = x2.shape[0]
    row = lambda n: pl.BlockSpec((FFN_TM, n), lambda i: (i, 0))
    resident = lambda shape: pl.BlockSpec(shape, lambda i: (0, 0), pipeline_mode=pl.Buffered(1))
    return pl.pallas_call(
        _ffn_kernel,
        grid=(t // FFN_TM,),
        in_specs=[row(D_MODEL)] + [row(BRANCH_W)] * 4 + [pl.BlockSpec((D_MODEL, D_MODEL), lambda i: (0, 0)),
                  pl.BlockSpec((1, D_MODEL), lambda i: (0, 0)),
                  resident((D_MODEL, D_FF)), resident((D_MODEL, D_FF)), resident((D_FF, D_MODEL))],
        out_specs=row(D_MODEL),
        out_shape=jax.ShapeDtypeStruct((t, D_MODEL), F32),
        compiler_params=_cparams(("parallel",), 52),
        name="ffn_dense",
    )(x2, *ys, w_out.astype(BF16), g.reshape(1, -1), w_gate.astype(BF16), w_up.astype(BF16), w_down.astype(BF16))


def _tiles_to_rows(ref, m):
    return jnp.concatenate([ref[pl.ds(c, m, stride=ROW_TILE), :] for c in range(ROW_TILE)], axis=1)


def _rows_to_tiles(ref, value):
    for c in range(ROW_TILE):
        ref[pl.ds(c, value.shape[0], stride=ROW_TILE), :] = value[:, c * LANES:(c + 1) * LANES]


def _moe_kernel(be_ref, nu_ref, x_ref, g_ref, wg_ref, wu_ref, wd_ref, o_ref):
    del be_ref

    @pl.when(pl.program_id(0) < nu_ref[0])
    def _():
        xb = _rms(_tiles_to_rows(x_ref, MOE_TM), g_ref[...]).astype(BF16)
        acc = None
        for f in range(D_FF_EXPERT // MOE_TF):
            cols = slice(f * MOE_TF, (f + 1) * MOE_TF)
            a = jnp.dot(xb, wg_ref[:, cols], preferred_element_type=F32)
            u = jnp.dot(xb, wu_ref[:, cols], preferred_element_type=F32)
            y = jnp.dot((_silu(a) * u).astype(BF16), wd_ref[cols, :], preferred_element_type=F32)
            acc = y if acc is None else acc + y
        _rows_to_tiles(o_ref, acc)

    @pl.when(pl.program_id(0) >= nu_ref[0])
    def _():
        o_ref[...] = jnp.zeros_like(o_ref)


def _moe_experts(buf, g, block_e, n_used, wg, wu, wd):
    rows = buf.shape[0] // ROW_TILE
    expert = lambda shape, buffers=2: pl.BlockSpec((None,) + shape, lambda m, be, nu: (be[m], 0, 0),
                                                   pipeline_mode=pl.Buffered(buffers))
    grid_spec = pltpu.PrefetchScalarGridSpec(
        num_scalar_prefetch=2,
        grid=(rows // MOE_TM,),
        in_specs=[pl.BlockSpec((MOE_TM * ROW_TILE, LANES), lambda m, be, nu: (jnp.minimum(m, nu[0] - 1), 0)),
                  pl.BlockSpec((1, D_MODEL), lambda m, be, nu: (0, 0)),
                  expert((D_MODEL, D_FF_EXPERT), 1), expert((D_MODEL, D_FF_EXPERT)), expert((D_FF_EXPERT, D_MODEL))],
        out_specs=pl.BlockSpec((MOE_TM * ROW_TILE, LANES), lambda m, be, nu: (m, 0)),
    )
    return pl.pallas_call(
        _moe_kernel,
        grid_spec=grid_spec,
        out_shape=jax.ShapeDtypeStruct((rows * ROW_TILE, LANES), F32),
        compiler_params=_cparams(("arbitrary",), 56),
        name="moe_experts",
    )(block_e, n_used, buf, g.reshape(1, -1), wg, wu, wd)


def _route_kernel(x_ref, y0_ref, y1_ref, y2_ref, y3_ref, wo_ref, g_ref, wt_ref,
                  xo_ref, e_ref, r_ref, gc_ref, cnt_ref, run_ref):
    tm = ROUTE_TM

    @pl.when(pl.program_id(0) == 0)
    def _():
        run_ref[...] = jnp.zeros_like(run_ref)

    x = _mix_residual(x_ref, (y0_ref, y1_ref, y2_ref, y3_ref), wo_ref)
    _rows_to_tiles(xo_ref, x)
    h = _rms(x, g_ref[...])
    w = wt_ref[...]
    w_hi = w.astype(BF16).astype(F32)
    w_parts = jnp.concatenate([w_hi, w - w_hi], axis=0).astype(BF16)
    h_hi = h.astype(BF16)
    h_lo = (h - h_hi.astype(F32)).astype(BF16)
    nt = lambda a, b: lax.dot_general(a, b, (((1,), (1,)), ((), ())), preferred_element_type=F32)
    parts = nt(w_parts, h_hi) + nt(w_parts, h_lo)
    lt = parts[:N_EXPERTS] + parts[N_EXPERTS:]
    sub = lax.broadcasted_iota(jnp.int32, (N_EXPERTS, tm), 0)
    m1 = jnp.max(lt, axis=0, keepdims=True)
    i1 = jnp.min(jnp.where(lt == m1, sub, N_EXPERTS), axis=0, keepdims=True)
    lt2 = jnp.where(sub == i1, -jnp.inf, lt)
    m2 = jnp.max(lt2, axis=0, keepdims=True)
    i2 = jnp.min(jnp.where(lt2 == m2, sub, N_EXPERTS), axis=0, keepdims=True)
    oh0 = jnp.where(sub == i1, 1.0, 0.0)
    oh1 = jnp.where(sub == i2, 1.0, 0.0)
    cnt = oh0 + oh1
    ti = lax.broadcasted_iota(jnp.int32, (tm, tm), 0)
    tj = lax.broadcasted_iota(jnp.int32, (tm, tm), 1)
    upper = jnp.where(ti < tj, 1.0, 0.0).astype(BF16)
    pre = jnp.dot(cnt.astype(BF16), upper, preferred_element_type=F32) + run_ref[:, 0:1]
    r0 = jnp.sum(oh0 * pre, axis=0, keepdims=True)
    r1 = jnp.sum(oh1 * pre, axis=0, keepdims=True)
    e_ref[...] = jnp.concatenate([i1, i2], axis=0)
    r_ref[...] = jnp.concatenate([r0, r1], axis=0).astype(jnp.int32)
    run = run_ref[...] + jnp.sum(cnt, axis=1, keepdims=True)
    run_ref[...] = run
    cnt_ref[...] = run
    ex = jnp.exp(m2 - m1)
    g0 = 1.0 / (1.0 + ex)
    gates = jnp.where(sub == 0, g0, jnp.where(sub == 1, ex * g0, 0.0))
    er = lax.broadcasted_iota(jnp.int32, (N_EXPERTS, LANES), 0)
    ec = lax.broadcasted_iota(jnp.int32, (N_EXPERTS, LANES), 1)
    eye = jnp.where(er == ec, 1.0, 0.0).astype(BF16)
    acc = None
    for _ in range(3):
        piece = gates.astype(BF16)
        term = lax.dot_general(piece, eye, (((0,), (0,)), ((), ())), preferred_element_type=F32)
        acc = term if acc is None else acc + term
        gates = gates - piece.astype(F32)
    gc_ref[...] = acc


def _moe_route(x2, ys, w_out, ffn_norm, w_router):
    t = x2.shape[0]
    tm = ROUTE_TM
    row = lambda n: pl.BlockSpec((tm, n), lambda i: (i, 0))
    return pl.pallas_call(
        _route_kernel,
        grid=(t // tm,),
        in_specs=[row(D_MODEL)] + [row(BRANCH_W)] * 4 + [pl.BlockSpec((D_MODEL, D_MODEL), lambda i: (0, 0)),
                  pl.BlockSpec((1, D_MODEL), lambda i: (0, 0)),
                  pl.BlockSpec((N_EXPERTS, D_MODEL), lambda i: (0, 0))],
        out_specs=[pl.BlockSpec((tm * ROW_TILE, LANES), lambda i: (i, 0)),
                   pl.BlockSpec((TOP_K, tm), lambda i: (0, i)),
                   pl.BlockSpec((TOP_K, tm), lambda i: (0, i)),
                   row(LANES),
                   pl.BlockSpec((N_EXPERTS, LANES), lambda i: (0, 0))],
        out_shape=[jax.ShapeDtypeStruct((t * ROW_TILE, LANES), F32),
                   jax.ShapeDtypeStruct((TOP_K, t), jnp.int32),
                   jax.ShapeDtypeStruct((TOP_K, t), jnp.int32),
                   jax.ShapeDtypeStruct((t, LANES), F32),
                   jax.ShapeDtypeStruct((N_EXPERTS, LANES), F32)],
        scratch_shapes=[pltpu.VMEM((N_EXPERTS, LANES), F32)],
        compiler_params=_cparams(("arbitrary",), 40),
        name="moe_route",
    )(x2, *ys, w_out.astype(BF16), ffn_norm.reshape(1, -1), w_router.T)


def _row_copy(src_ref, src_row, dst_ref, dst_row, sem):
    tile = lambda row: pl.ds(row * ROW_TILE if isinstance(row, int) else pl.multiple_of(row * ROW_TILE, ROW_TILE),
                             ROW_TILE)
    return pltpu.make_async_copy(src_ref.at[tile(src_row), :], dst_ref.at[tile(dst_row), :], sem)


def _dispatch_kernel(plo_ref, pn_ref, dest_hbm, x_ref, out_hbm, idx_ref, zero_ref, idx_sem, row_sem, pad_sem):
    tm = ROUTE_TM
    load = pltpu.make_async_copy(dest_hbm.at[pl.program_id(0)], idx_ref, idx_sem)
    load.start()

    @pl.when(pl.program_id(0) == 0)
    def _():
        zero_ref[...] = jnp.zeros_like(zero_ref)
        for e in range(N_EXPERTS):
            def fill(r, carry, e=e):
                _row_copy(zero_ref, 0, out_hbm, plo_ref[e] + r, pad_sem).start()
                return carry
            lax.fori_loop(0, pn_ref[e], fill, 0)
        for e in range(N_EXPERTS):
            def done(r, carry, e=e):
                _row_copy(zero_ref, 0, out_hbm, plo_ref[e] + r, pad_sem).wait()
                return carry
            lax.fori_loop(0, pn_ref[e], done, 0)
        first_free = (plo_ref[N_EXPERTS - 1] + pn_ref[N_EXPERTS - 1]) // MOE_TM

        block_rows = MOE_TM * ROW_TILE

        def block_copy(b):
            return pltpu.make_async_copy(
                zero_ref, out_hbm.at[pl.ds(pl.multiple_of(b * block_rows, block_rows), block_rows), :], pad_sem)

        def fill_block(b, carry):
            block_copy(b).start()
            return carry

        def done_block(b, carry):
            block_copy(b).wait()
            return carry

        lax.fori_loop(first_free, out_hbm.shape[0] // block_rows, fill_block, 0)
        lax.fori_loop(first_free, out_hbm.shape[0] // block_rows, done_block, 0)

    load.wait()
    for t in range(tm):
        for k in range(TOP_K):
            _row_copy(x_ref, t, out_hbm, idx_ref[k * tm + t], row_sem).start(priority=k)
    for t in range(tm):
        for k in range(TOP_K):
            _row_copy(x_ref, t, out_hbm, idx_ref[k * tm + t], row_sem).wait()


def _moe_dispatch(xt, dest2, pad_lo, pad_n, n_rows):
    t = xt.shape[0] // ROW_TILE
    tm = ROUTE_TM
    grid_spec = pltpu.PrefetchScalarGridSpec(
        num_scalar_prefetch=2,
        grid=(t // tm,),
        in_specs=[pl.BlockSpec(memory_space=pl.ANY),
                  pl.BlockSpec((tm * ROW_TILE, LANES), lambda i, lo, n: (i, 0))],
        out_specs=pl.BlockSpec(memory_space=pl.ANY),
        scratch_shapes=[pltpu.SMEM((TOP_K * tm,), jnp.int32), pltpu.VMEM((MOE_TM * ROW_TILE, LANES), F32),
                        pltpu.SemaphoreType.DMA, pltpu.SemaphoreType.DMA, pltpu.SemaphoreType.DMA],
    )
    return pl.pallas_call(
        _dispatch_kernel,
        grid_spec=grid_spec,
        out_shape=jax.ShapeDtypeStruct((n_rows * ROW_TILE, LANES), F32),
        compiler_params=_cparams(("arbitrary",), 32),
        name="moe_dispatch",
    )(pad_lo, pad_n, dest2, xt)


def _combine_kernel(dest_hbm, x_ref, gc_ref, fn_ref, y_hbm, o_ref, idx0_ref, idx1_ref, ybuf_ref, idx_sem, row_sem):
    tm = ROUTE_TM
    i, n = pl.program_id(0), pl.num_programs(0)
    idx_refs = (idx0_ref, idx1_ref)

    def gather(idx_ref, slot, t, k):
        return _row_copy(y_hbm, idx_ref[k * tm + t], ybuf_ref.at[slot, k], t, row_sem.at[slot])

    @pl.when(i == 0)
    def _():
        first = pltpu.make_async_copy(dest_hbm.at[0], idx0_ref, idx_sem.at[0])
        first.start()
        first.wait()

        def issue(t, carry):
            for k in range(TOP_K):
                gather(idx0_ref, 0, t, k).start()
            return carry

        lax.fori_loop(0, tm, issue, 0)

    def step(slot):
        nxt = 1 - slot
        load = pltpu.make_async_copy(dest_hbm.at[jnp.minimum(i + 1, n - 1)], idx_refs[nxt], idx_sem.at[nxt])
        load.start()
        for t in range(tm):
            for k in range(TOP_K):
                gather(idx_refs[slot], slot, t, k).wait()
        load.wait()
        for t in range(tm):
            for k in range(TOP_K):
                gather(idx_refs[nxt], nxt, t, k).start(priority=k)
        gc = gc_ref[...]
        x = (_tiles_to_rows(x_ref, tm) + gc[:, 0:1] * _tiles_to_rows(ybuf_ref.at[slot, 0], tm)
             + gc[:, 1:2] * _tiles_to_rows(ybuf_ref.at[slot, 1], tm))
        o_ref[...] = _rms(x, fn_ref[...])

        @pl.when(i == n - 1)
        def _():
            for t in range(tm):
                for k in range(TOP_K):
                    gather(idx_refs[nxt], nxt, t, k).wait()

    for slot in range(2):
        pl.when(i % 2 == slot)(functools.partial(step, slot))


def _moe_combine_norm(xt, y_buf, dest2, gcol, final_norm):
    t = xt.shape[0] // ROW_TILE
    tm = ROUTE_TM
    return pl.pallas_call(
        _combine_kernel,
        grid=(t // tm,),
        in_specs=[pl.BlockSpec(memory_space=pl.ANY),
                  pl.BlockSpec((tm * ROW_TILE, LANES), lambda i: (i, 0)),
                  pl.BlockSpec((tm, LANES), lambda i: (i, 0)),
                  pl.BlockSpec((1, D_MODEL), lambda i: (0, 0)),
                  pl.BlockSpec(memory_space=pl.ANY)],
        out_specs=pl.BlockSpec((tm, D_MODEL), lambda i: (i, 0)),
        out_shape=jax.ShapeDtypeStruct((t, D_MODEL), F32),
        scratch_shapes=[pltpu.SMEM((TOP_K * tm,), jnp.int32), pltpu.SMEM((TOP_K * tm,), jnp.int32),
                        pltpu.VMEM((2, TOP_K, tm * ROW_TILE, LANES), F32),
                        pltpu.SemaphoreType.DMA((2,)), pltpu.SemaphoreType.DMA((2,))],
        compiler_params=_cparams(("arbitrary",), 40),
        name="moe_combine_norm",
    )(dest2, xt, gcol, final_norm.reshape(1, -1), y_buf)


def _moe_layer(x2, ys, w_out, ffn_norm, w_router, w_gate, w_up, w_down, final_norm):
    t = x2.shape[0]
    tm = ROUTE_TM
    xt, e01, r01, gcol, cnt = _moe_route(x2, ys, w_out, ffn_norm, w_router)
    counts = cnt[:, 0].astype(jnp.int32)
    padded = (counts + MOE_TM - 1) // MOE_TM * MOE_TM
    pad_ends = jnp.cumsum(padded)
    pad_starts = pad_ends - padded
    n_blocks = -(-t * TOP_K // MOE_TM) + N_EXPERTS
    blk_start = jnp.arange(n_blocks, dtype=jnp.int32) * MOE_TM
    block_e = jnp.minimum(jnp.sum(blk_start[:, None] >= pad_ends[None, :], axis=1), N_EXPERTS - 1).astype(jnp.int32)
    n_used = (pad_ends[-1] // MOE_TM).astype(jnp.int32).reshape(1)
    dest = r01 + jnp.sum(jnp.where(e01[None] == jnp.arange(N_EXPERTS)[:, None, None], pad_starts[:, None, None], 0),
                         axis=0)
    dest2 = dest.reshape(TOP_K, t // tm, tm).transpose(1, 0, 2).reshape(t // tm, TOP_K * tm)
    xs = _moe_dispatch(xt, dest2, pad_starts + counts, padded - counts, n_blocks * MOE_TM)
    y_buf = _moe_experts(xs, ffn_norm, block_e, n_used, w_gate.astype(BF16), w_up.astype(BF16),
                         w_down.astype(BF16))
    return _moe_combine_norm(xt, y_buf, dest2, gcol, final_norm)


def _permute_w_in(w_in):
    s5 = w_in[:, 0:256]
    gm = w_in[:, 256:768]
    qkv = w_in[:, 768:1536]
    z = w_in[:, 1536:1792]
    a = w_in[:, 1792:1796]
    b = w_in[:, 1796:1800]
    sc = w_in[:, 1800:2568]
    ab = jnp.zeros((D_MODEL, LANES), F32).at[:, 0:GDN_HEADS].set(a).at[:, AB_B_LANE:AB_B_LANE + GDN_HEADS].set(b)
    return jnp.concatenate([gm, s5, z, qkv, sc, ab], axis=1).astype(BF16)


def kernel(x, mix_norm, w_in, s5_lam_re, s5_lam_im, s5_log_step, s5_b_re, s5_b_im, s5_c_re, s5_c_im, s5_d, s5_w_glu, s5_b_glu, s5_out_norm, sgu_ln_g, sgu_ln_b, sgu_w, sgu_b, gmlp_out_norm, gdn_conv, gdn_a_log, gdn_dt_bias, gdn_norm, sc_conv, sc_out_norm, w_out, ffn_norm, ffn_w_gate, ffn_w_up, ffn_w_down, moe_router, moe_w_gate, moe_w_up, moe_w_down, final_norm):
    bsz, seqlen, d = x.shape
    t = bsz * seqlen
    x2 = x.reshape(t, d)
    out = None
    for l in range(DEPTH):
        p2 = _in_proj(x2, mix_norm[l].reshape(1, -1), _permute_w_in(w_in[l]))
        p3 = p2.reshape(bsz, seqlen, P_COLS)
        wb, wc, pw = _s5_params(s5_lam_re[l], s5_lam_im[l], s5_log_step[l], s5_b_re[l], s5_b_im[l],
                                s5_c_re[l], s5_c_im[l])
        y_s5 = _s5_mixer(p3, wb, wc, pw, s5_d[l], s5_w_glu[l], s5_b_glu[l], s5_out_norm[l])
        y_gm = _gmlp_mixer(p3, sgu_ln_g[l], sgu_ln_b[l], sgu_w[l], sgu_b[l], gmlp_out_norm[l])
        y_gdn = _gdn_mixer(p3, gdn_conv[l], gdn_a_log[l], gdn_dt_bias[l], gdn_norm[l])
        y_sc = _shortconv_mixer(p3, sc_conv[l], sc_out_norm[l])
        ys = [y.reshape(t, BRANCH_W) for y in (y_s5, y_gm, y_gdn, y_sc)]
        i = l // 2
        if l % 2 == 0:
            x2 = _ffn_dense(x2, ys, w_out[l], ffn_norm[l], ffn_w_gate[i], ffn_w_up[i], ffn_w_down[i])
        else:
            out = _moe_layer(x2, ys, w_out[l], ffn_norm[l], moe_router[i], moe_w_gate[i], moe_w_up[i],
                             moe_w_down[i], final_norm)
    return out.reshape(bsz, seqlen, d)
```

```python
import functools
import math

import jax
import jax.numpy as jnp
from jax import lax
from jax.experimental import pallas as pl
from jax.experimental.pallas import tpu as pltpu

F32 = jnp.float32
BF16 = jnp.bfloat16

D_MODEL = 1024
DEPTH = 2
BRANCH_W = 256
S5_GROUP = 16
S5_GROUPS = 16
S5_STATE = 64
S5_NSTATE = S5_GROUPS * S5_STATE
GMLP_HEADS = 4
GMLP_HEAD_DIM = 64
GMLP_CHUNK = 128
GDN_HEAD_DIM = 64
GDN_HEADS = 4
GDN_CONV = 4
GDN_CHUNK = 64
SC_CONV = 3
D_FF = 2816
N_EXPERTS = 8
TOP_K = 2
D_FF_EXPERT = 3584
EPS = 1e-6

LANES = 128
SUBLANES = 8

COL_GM = 0
COL_S5 = 512
COL_Z = 768
COL_Q = 1024
COL_K = 1280
COL_V = 1536
COL_SCB = 1792
COL_SCC = 2048
COL_SCX = 2304
COL_AB = 2560
P_COLS = 2688
AB_B_LANE = 64

IN_TM = 1024
S5_CHUNK = 128
S5_BATCH = 4
S5_SEG = S5_CHUNK // SUBLANES
S5_POW_ROW = 0
S5_SEG_ROW = S5_POW_ROW + S5_SEG
S5_CARRY_ROW = S5_SEG_ROW + 3 * SUBLANES
S5_TABLE_ROWS = S5_CARRY_ROW + SUBLANES
GM_TILE = 2048
SC_TILE = 2048
GDN_TILE = 128
GDN_BATCH = 4
FFN_TM = 512
FFN_TF = 256
MOE_TM = 512
MOE_TF = 512
ROUTE_TM = 512
ROW_TILE = D_MODEL // LANES


def _cparams(sem, vmem_mb):
    return pltpu.CompilerParams(dimension_semantics=sem, vmem_limit_bytes=vmem_mb * 1024 * 1024)


def _rms(x, g):
    return x * lax.rsqrt(jnp.mean(x * x, axis=-1, keepdims=True) + EPS) * g


def _silu(x):
    return x * jax.nn.sigmoid(x)


def _bdot(a, b):
    return jnp.dot(a.astype(BF16), b.astype(BF16), preferred_element_type=F32)


def _split_dot(a, b, passes, data_on_left):
    data = a if data_on_left else b
    acc = None
    for _ in range(passes):
        piece = data.astype(BF16)
        term = (jnp.dot(piece, b, preferred_element_type=F32) if data_on_left
                else jnp.dot(a, piece, preferred_element_type=F32))
        acc = term if acc is None else acc + term
        data = data - piece.astype(F32)
    return acc


def _bdot_nt(a, b):
    return lax.dot_general(a.astype(BF16), b.astype(BF16), (((1,), (1,)), ((), ())),
                           preferred_element_type=F32)


def _in_proj_kernel(x_ref, g_ref, w_ref, o_ref):
    h = _rms(x_ref[...], g_ref[...])
    o_ref[...] = _bdot(h, w_ref[...])


def _in_proj(x2, g, w):
    t = x2.shape[0]
    return pl.pallas_call(
        _in_proj_kernel,
        grid=(t // IN_TM,),
        in_specs=[pl.BlockSpec((IN_TM, D_MODEL), lambda i: (i, 0)),
                  pl.BlockSpec((1, D_MODEL), lambda i: (0, 0)),
                  pl.BlockSpec((D_MODEL, P_COLS), lambda i: (0, 0))],
        out_specs=pl.BlockSpec((IN_TM, P_COLS), lambda i: (i, 0)),
        out_shape=jax.ShapeDtypeStruct((t, P_COLS), F32),
        compiler_params=_cparams(("parallel",), 48),
        name="in_proj",
    )(x2, g, w)


def _s5_kernel(u0_ref, u1_ref, wb_ref, pw_ref, wc_ref, d_ref, wglu_ref, bglu_ref, on_ref, o_ref,
               sr_ref, si_ref, ys_ref):
    @pl.when(pl.program_id(1) == 0)
    def _():
        sr_ref[...] = jnp.zeros_like(sr_ref)
        si_ref[...] = jnp.zeros_like(si_ref)

    seg, c = S5_SEG, S5_CHUNK
    u = jnp.concatenate(
        [jnp.concatenate([jnp.concatenate([ref[bb, pl.ds(i, SUBLANES, stride=seg), :] for i in range(seg)], axis=0)
                          for ref in (u0_ref, u1_ref)], axis=1) for bb in range(S5_BATCH)], axis=0)
    bu = _bdot(u, wb_ref[...])
    xs = [_s5_scan(bu[bb * c:(bb + 1) * c], pw_ref, sr_ref.at[bb], si_ref.at[bb]) for bb in range(S5_BATCH)]
    x = jnp.concatenate(xs, axis=0)
    y = _bdot(x, wc_ref[...])
    y = y + d_ref[...] * u
    y = jax.nn.gelu(y)
    y = y * jax.nn.sigmoid(_bdot(y, wglu_ref[...]) + bglu_ref[...])
    y = _rms(y, on_ref[...])
    for bb in range(S5_BATCH):
        for slab in range(BRANCH_W // LANES):
            ys_ref[bb, slab] = y[bb * c:(bb + 1) * c, slab * LANES:(slab + 1) * LANES]
        for r in range(c // SUBLANES):
            start = (c // 2) * (r % 2) + r // 2
            for slab in range(BRANCH_W // LANES):
                o_ref[bb, r * SUBLANES:(r + 1) * SUBLANES, slab * LANES:(slab + 1) * LANES] = (
                    ys_ref[bb, slab, pl.ds(start, SUBLANES, stride=SUBLANES), :])


def _s5_scan(bu, pw_ref, sr_ref, si_ref):
    n, seg = S5_NSTATE, S5_SEG
    cmul = lambda ar, ai, xr, xi: (ar * xr - ai * xi, ar * xi + ai * xr)
    a_r, a_i = pw_ref[S5_POW_ROW:S5_POW_ROW + 1, :n], pw_ref[S5_POW_ROW:S5_POW_ROW + 1, n:]
    xr, xi = bu[0:SUBLANES, :n], bu[0:SUBLANES, n:]
    xrs, xis = [xr], [xi]
    for i in range(1, seg):
        rows = slice(i * SUBLANES, (i + 1) * SUBLANES)
        pr, pi = cmul(a_r, a_i, xr, xi)
        xr, xi = pr + bu[rows, :n], pi + bu[rows, n:]
        xrs.append(xr)
        xis.append(xi)
    fr, fi = xr, xi
    for s in range(int(math.log2(SUBLANES))):
        rows = slice(S5_SEG_ROW + s * SUBLANES, S5_SEG_ROW + (s + 1) * SUBLANES)
        pr, pi = cmul(pw_ref[rows, :n], pw_ref[rows, n:], pltpu.roll(fr, 1 << s, 0), pltpu.roll(fi, 1 << s, 0))
        fr, fi = fr + pr, fi + pi
    sbr = jnp.broadcast_to(sr_ref[...], (SUBLANES, n))
    sbi = jnp.broadcast_to(si_ref[...], (SUBLANES, n))
    rows = slice(S5_CARRY_ROW, S5_CARRY_ROW + SUBLANES)
    pr, pi = cmul(pw_ref[rows, :n], pw_ref[rows, n:], sbr, sbi)
    fr, fi = fr + pr, fi + pi
    sr_ref[...] = fr[SUBLANES - 1:, :]
    si_ref[...] = fi[SUBLANES - 1:, :]
    first = lax.broadcasted_iota(jnp.int32, (SUBLANES, n), 0) == 0
    cin_r = jnp.where(first, sbr, pltpu.roll(fr, 1, 0))
    cin_i = jnp.where(first, sbi, pltpu.roll(fi, 1, 0))
    for i in range(seg):
        row = S5_POW_ROW + i
        pr, pi = cmul(pw_ref[row:row + 1, :n], pw_ref[row:row + 1, n:], cin_r, cin_i)
        xrs[i], xis[i] = xrs[i] + pr, xis[i] + pi
    return jnp.concatenate([jnp.concatenate(xrs, axis=0), jnp.concatenate(xis, axis=0)], axis=1)


def _s5_params(lam_re, lam_im, log_step, b_re, b_im, c_re, c_im):
    g, n, p = S5_GROUPS, S5_STATE, S5_GROUP
    dt = jnp.exp(log_step)[:, None]
    mag = jnp.exp(lam_re * dt)
    ar, ai = mag * jnp.cos(lam_im * dt), mag * jnp.sin(lam_im * dt)
    den = lam_re * lam_re + lam_im * lam_im
    fr = ((ar - 1.0) * lam_re + ai * lam_im) / den
    fi = (ai * lam_re - (ar - 1.0) * lam_im) / den
    bbr = fr[..., None] * b_re - fi[..., None] * b_im
    bbi = fr[..., None] * b_im + fi[..., None] * b_re
    eye = jnp.eye(g, dtype=F32)
    wbr = jnp.einsum('gnp,gh->gphn', bbr, eye).reshape(g * p, g * n)
    wbi = jnp.einsum('gnp,gh->gphn', bbi, eye).reshape(g * p, g * n)
    wb = jnp.concatenate([wbr, wbi], axis=1)
    wcr = jnp.einsum('gpn,gh->gnhp', c_re, eye).reshape(g * n, g * p)
    wci = jnp.einsum('gpn,gh->gnhp', -c_im, eye).reshape(g * n, g * p)
    wc = jnp.concatenate([wcr, wci], axis=0)
    def powers(br, bi, count):
        rows_r, rows_i = [br], [bi]
        for _ in range(count - 1):
            qr, qi = rows_r[-1], rows_i[-1]
            rows_r.append(qr * br - qi * bi)
            rows_i.append(qr * bi + qi * br)
        return jnp.concatenate(rows_r, axis=0), jnp.concatenate(rows_i, axis=0)

    a1r, a1i = ar.reshape(1, g * n), ai.reshape(1, g * n)
    pos_r, pos_i = powers(a1r, a1i, S5_SEG)
    a16r, a16i = pos_r[S5_SEG - 1:], pos_i[S5_SEG - 1:]
    r_idx = jnp.arange(SUBLANES)[:, None]
    seg_r, seg_i = [], []
    pr, pi = a16r, a16i
    for s in range(int(math.log2(SUBLANES))):
        keep = r_idx >= (1 << s)
        seg_r.append(jnp.where(keep, pr, 0.0))
        seg_i.append(jnp.where(keep, pi, 0.0))
        pr, pi = pr * pr - pi * pi, 2.0 * pr * pi
    car_r, car_i = powers(a16r, a16i, SUBLANES)
    pw = jnp.concatenate([jnp.concatenate([pos_r] + seg_r + [car_r], axis=0),
                          jnp.concatenate([pos_i] + seg_i + [car_i], axis=0)], axis=1)
    return wb.astype(BF16), wc.astype(BF16), pw


def _s5_mixer(p3, wb, wc, pw, d_skip, w_glu, b_glu, out_norm):
    b, l, _ = p3.shape
    n2 = 2 * S5_NSTATE
    const = lambda shape: pl.BlockSpec(shape, lambda i, j: (0,) * len(shape))
    return pl.pallas_call(
        _s5_kernel,
        grid=(b // S5_BATCH, l // S5_CHUNK),
        in_specs=[pl.BlockSpec((S5_BATCH, S5_CHUNK, LANES), lambda i, j: (i, j, COL_S5 // LANES)),
                  pl.BlockSpec((S5_BATCH, S5_CHUNK, LANES), lambda i, j: (i, j, COL_S5 // LANES + 1)),
                  const((BRANCH_W, n2)), const((S5_TABLE_ROWS, n2)), const((n2, BRANCH_W)),
                  const((1, BRANCH_W)), const((BRANCH_W, BRANCH_W)), const((1, BRANCH_W)),
                  const((1, BRANCH_W))],
        out_specs=pl.BlockSpec((S5_BATCH, S5_CHUNK, BRANCH_W), lambda i, j: (i, j, 0)),
        out_shape=jax.ShapeDtypeStruct((b, l, BRANCH_W), F32),
        scratch_shapes=[pltpu.VMEM((S5_BATCH, 1, S5_NSTATE), F32), pltpu.VMEM((S5_BATCH, 1, S5_NSTATE), F32),
                        pltpu.VMEM((S5_BATCH, BRANCH_W // LANES, S5_CHUNK, LANES), F32)],
        compiler_params=_cparams(("parallel", "arbitrary"), 40),
        name="s5_mixer",
    )(p3, p3, wb, pw, wc, d_skip.reshape(1, -1), w_glu.astype(BF16), b_glu.reshape(1, -1),
      out_norm.reshape(1, -1))


def _gmlp_kernel(p_ref, lng_ref, lnb_ref, w_ref, bias_ref, on_ref, o_ref):
    z = jax.nn.gelu(p_ref[...])
    u, v = z[:, :BRANCH_W], z[:, BRANCH_W:]
    vc = v - jnp.mean(v, axis=-1, keepdims=True)
    v = vc * lax.rsqrt(jnp.mean(vc * vc, axis=-1, keepdims=True) + EPS) * lng_ref[...] + lnb_ref[...]
    ti = lax.broadcasted_iota(jnp.int32, (GMLP_CHUNK, GMLP_CHUNK), 0)
    si = lax.broadcasted_iota(jnp.int32, (GMLP_CHUNK, GMLP_CHUNK), 1)
    tril = ti >= si
    ws = jnp.concatenate([jnp.where(tril, w_ref[h], 0.0) for h in range(GMLP_HEADS)], axis=0).astype(BF16)
    lane = lax.broadcasted_iota(jnp.int32, (GMLP_CHUNK, BRANCH_W), 1)
    bias = bias_ref[...]
    outs = []
    for c in range(GM_TILE // GMLP_CHUNK):
        vb = v[c * GMLP_CHUNK:(c + 1) * GMLP_CHUNK, :].astype(BF16)
        mixed = jnp.dot(ws, vb, preferred_element_type=F32)
        s = mixed[(GMLP_HEADS - 1) * GMLP_CHUNK:]
        for h in range(GMLP_HEADS - 2, -1, -1):
            s = jnp.where(lane < (h + 1) * GMLP_HEAD_DIM, mixed[h * GMLP_CHUNK:(h + 1) * GMLP_CHUNK], s)
        outs.append(s + bias)
    s = jnp.concatenate(outs, axis=0)
    o_ref[...] = _rms(u * s, on_ref[...])


def _gmlp_mixer(p3, ln_g, ln_b, w_sp, b_sp, out_norm):
    b, l, _ = p3.shape
    bias = jnp.repeat(b_sp.T, GMLP_HEAD_DIM, axis=1)
    const = lambda shape: pl.BlockSpec(shape, lambda i, j: (0,) * len(shape))
    return pl.pallas_call(
        _gmlp_kernel,
        grid=(b, l // GM_TILE),
        in_specs=[pl.BlockSpec((None, GM_TILE, 2 * BRANCH_W), lambda i, j: (i, j, COL_GM // (2 * BRANCH_W))),
                  const((1, BRANCH_W)), const((1, BRANCH_W)),
                  const((GMLP_HEADS, GMLP_CHUNK, GMLP_CHUNK)), const((GMLP_CHUNK, BRANCH_W)),
                  const((1, BRANCH_W))],
        out_specs=pl.BlockSpec((None, GM_TILE, BRANCH_W), lambda i, j: (i, j, 0)),
        out_shape=jax.ShapeDtypeStruct((b, l, BRANCH_W), F32),
        compiler_params=_cparams(("parallel", "parallel"), 32),
        name="gmlp_mixer",
    )(p3, ln_g.reshape(1, -1), ln_b.reshape(1, -1), w_sp, bias, out_norm.reshape(1, -1))


def _shortconv_kernel(b_ref, c_ref, x_ref, w_ref, on_ref, o_ref, halo_ref):
    @pl.when(pl.program_id(1) == 0)
    def _():
        halo_ref[...] = jnp.zeros_like(halo_ref)

    cx = c_ref[...] * x_ref[...]
    ext = jnp.concatenate([halo_ref[...], cx], axis=0)
    halo_ref[...] = cx[SC_TILE - SUBLANES:, :]
    y = w_ref[SC_CONV - 1:SC_CONV, :] * cx
    for j in range(SC_CONV - 1):
        sh = SC_CONV - 1 - j
        y = y + w_ref[j:j + 1, :] * pltpu.roll(ext, sh, 0)[SUBLANES:, :]
    o_ref[...] = _rms(b_ref[...] * y, on_ref[...])


def _shortconv_mixer(p3, conv_w, out_norm):
    b, l, _ = p3.shape
    w = jnp.concatenate([conv_w, jnp.zeros((SUBLANES - SC_CONV, BRANCH_W), F32)], axis=0)
    col = lambda c: pl.BlockSpec((None, SC_TILE, BRANCH_W), lambda i, j: (i, j, c // BRANCH_W))
    const = lambda shape: pl.BlockSpec(shape, lambda i, j: (0,) * len(shape))
    return pl.pallas_call(
        _shortconv_kernel,
        grid=(b, l // SC_TILE),
        in_specs=[col(COL_SCB), col(COL_SCC), col(COL_SCX), const((SUBLANES, BRANCH_W)), const((1, BRANCH_W))],
        out_specs=pl.BlockSpec((None, SC_TILE, BRANCH_W), lambda i, j: (i, j, 0)),
        out_shape=jax.ShapeDtypeStruct((b, l, BRANCH_W), F32),
        scratch_shapes=[pltpu.VMEM((SUBLANES, BRANCH_W), F32)],
        compiler_params=_cparams(("parallel", "arbitrary"), 32),
        name="shortconv_mixer",
    )(p3, p3, p3, w, out_norm.reshape(1, -1))


def _gdn_kernel(q_ref, k_ref, v_ref, z_ref, ab_ref, cw_ref, alog_ref, dtb_ref, ng_ref, o_ref,
                halo_ref, state_ref):
    c_len, h_dim, nh, w = GDN_CHUNK, GDN_HEAD_DIM, GDN_HEADS, BRANCH_W
    n_chunks = GDN_TILE // c_len

    @pl.when(pl.program_id(1) == 0)
    def _():
        halo_ref[...] = jnp.zeros_like(halo_ref)
        state_ref[...] = jnp.zeros_like(state_ref)

    shift = int(math.log2(h_dim))
    r256 = lax.broadcasted_iota(jnp.int32, (w, w), 0)
    c256 = lax.broadcasted_iota(jnp.int32, (w, w), 1)
    same_head = (r256 >> shift) == (c256 >> shift)
    ebd = jnp.where(same_head, 1.0, 0.0).astype(BF16)
    causal = same_head & (r256 >= c256)
    strict = same_head & (r256 > c256)
    eye = jnp.where(r256 == c256, 1.0, 0.0)
    r128 = lax.broadcasted_iota(jnp.int32, (LANES, w), 0)
    c128 = lax.broadcasted_iota(jnp.int32, (LANES, w), 1)
    exp_a = jnp.where(r128 == (c128 >> shift), 1.0, 0.0).astype(BF16)
    exp_b = jnp.where(r128 == (c128 >> shift) + AB_B_LANE, 1.0, 0.0).astype(BF16)

    def stack(a):
        lane_head = (lax.broadcasted_iota(jnp.int32, a.shape, 1) >> shift) & (nh - 1)
        return jnp.concatenate([jnp.where(lane_head == h, a, 0.0) for h in range(nh)], axis=0)

    def unstack(a):
        return a[0:c_len] + a[c_len:2 * c_len] + a[2 * c_len:3 * c_len] + a[3 * c_len:4 * c_len]

    def conv_silu(ref, bb, col):
        cur = ref[bb]
        ext = jnp.concatenate([halo_ref[bb, :, col * w:(col + 1) * w], cur], axis=0)
        halo_ref[bb, :, col * w:(col + 1) * w] = cur[GDN_TILE - SUBLANES:, :]
        y = cw_ref[GDN_CONV - 1:GDN_CONV, col * w:(col + 1) * w] * cur
        for j in range(GDN_CONV - 1):
            sh = GDN_CONV - 1 - j
            y = y + cw_ref[j:j + 1, col * w:(col + 1) * w] * pltpu.roll(ext, sh, 0)[SUBLANES:, :]
        return _silu(y)

    rt = lax.broadcasted_iota(jnp.int32, (GDN_TILE, GDN_TILE), 0)
    ct = lax.broadcasted_iota(jnp.int32, (GDN_TILE, GDN_TILE), 1)
    ltri = jnp.where(((rt >> shift) == (ct >> shift)) & (rt >= ct), 1.0, 0.0).astype(BF16)
    tile4 = lambda a: jnp.concatenate([a] * nh, axis=0)

    q_decs, k_decs, g_tots, rhss, a_mats, qkds = [], [], [], [], [], []
    for bb in range(GDN_BATCH):
        q = conv_silu(q_ref, bb, 0)
        k = conv_silu(k_ref, bb, 1)
        v = conv_silu(v_ref, bb, 2)
        q = q * lax.rsqrt(jnp.dot((q * q).astype(BF16), ebd, preferred_element_type=F32) + EPS) * (h_dim ** -0.5)
        k = k * lax.rsqrt(jnp.dot((k * k).astype(BF16), ebd, preferred_element_type=F32) + EPS)
        ab = ab_ref[bb]
        beta = jax.nn.sigmoid(ab)
        xa = ab + dtb_ref[...]
        softplus = jnp.maximum(xa, 0.0) + jnp.log(1.0 + jnp.exp(-jnp.abs(xa)))
        g = -jnp.exp(alog_ref[...]) * softplus
        gcum = _split_dot(ltri, g, 3, data_on_left=False)
        gexp = _split_dot(gcum, exp_a, 2, data_on_left=True)
        bexp = _split_dot(beta, exp_b, 2, data_on_left=True)
        for c in range(n_chunks):
            sl = slice(c * c_len, (c + 1) * c_len)
            qc, kc, vc, gc, bc = q[sl], k[sl], v[sl], gexp[sl], bexp[sl]
            glast = gc[c_len - 1:c_len, :]
            eg = jnp.exp(gc)
            q_decs.append(qc * eg)
            k_decs.append(kc * jnp.exp(glast - gc))
            g_tots.append(jnp.exp(glast))
            kst = stack(kc).astype(BF16)
            kk = _bdot_nt(kst, kst)
            qk = _bdot_nt(stack(qc), kst)
            gcol = stack(gc)
            bcol = stack(bc)
            dec = jnp.exp(jnp.where(causal, gcol - gcol.T, -jnp.inf))
            a_mats.append(jnp.where(strict, bcol * kk * dec, 0.0))
            qkds.append((qk * dec).astype(BF16))
            rhss.append(jnp.concatenate([tile4(vc) * bcol, tile4(kc * eg) * bcol], axis=1).astype(BF16))
    t_invs = [eye - a for a in a_mats]
    pws = a_mats
    for _ in range(int(math.log2(c_len)) - 1):
        pws = [_bdot(p, p) for p in pws]
        t_invs = [t + _bdot(t, p) for t, p in zip(t_invs, pws)]
    sols = [unstack(_bdot(t, r)) for t, r in zip(t_invs, rhss)]
    states = [state_ref[bb] for bb in range(GDN_BATCH)]
    outs = [[] for _ in range(GDN_BATCH)]
    for c in range(n_chunks):
        for bb in range(GDN_BATCH):
            i = bb * n_chunks + c
            w_v, w_k = sols[i][:, :w], sols[i][:, w:]
            v_new = w_v - _bdot(w_k, states[bb])
            o = _bdot(q_decs[i], states[bb]) + unstack(_bdot(qkds[i], stack(v_new)))
            states[bb] = states[bb] * g_tots[i] + jnp.where(same_head, _bdot(k_decs[i].T, v_new), 0.0)
            outs[bb].append(o)
    for bb in range(GDN_BATCH):
        state_ref[bb] = states[bb]
        o = jnp.concatenate(outs[bb], axis=0)
        ms = jnp.dot((o * o).astype(BF16), ebd, preferred_element_type=F32) * (1.0 / h_dim)
        o = o * lax.rsqrt(ms + EPS) * ng_ref[...]
        o_ref[bb] = o * _silu(z_ref[bb])


def _gdn_mixer(p3, conv_w, a_log, dt_bias, norm_g):
    b, l, _ = p3.shape
    cw = jnp.concatenate([conv_w, jnp.zeros((SUBLANES - GDN_CONV, 3 * BRANCH_W), F32)], axis=0)
    lane_row = lambda vec: jnp.zeros((1, LANES), F32).at[0, :GDN_HEADS].set(vec)
    col = lambda c: pl.BlockSpec((GDN_BATCH, GDN_TILE, BRANCH_W), lambda i, j: (i, j, c // BRANCH_W))
    const = lambda shape: pl.BlockSpec(shape, lambda i, j: (0,) * len(shape))
    return pl.pallas_call(
        _gdn_kernel,
        grid=(b // GDN_BATCH, l // GDN_TILE),
        in_specs=[col(COL_Q), col(COL_K), col(COL_V), col(COL_Z),
                  pl.BlockSpec((GDN_BATCH, GDN_TILE, LANES), lambda i, j: (i, j, COL_AB // LANES)),
                  const((SUBLANES, 3 * BRANCH_W)), const((1, LANES)), const((1, LANES)), const((1, BRANCH_W))],
        out_specs=pl.BlockSpec((GDN_BATCH, GDN_TILE, BRANCH_W), lambda i, j: (i, j, 0)),
        out_shape=jax.ShapeDtypeStruct((b, l, BRANCH_W), F32),
        scratch_shapes=[pltpu.VMEM((GDN_BATCH, SUBLANES, 3 * BRANCH_W), F32),
                        pltpu.VMEM((GDN_BATCH, BRANCH_W, BRANCH_W), F32)],
        compiler_params=_cparams(("parallel", "arbitrary"), 48),
        name="gdn_mixer",
    )(p3, p3, p3, p3, p3, cw, lane_row(a_log), lane_row(dt_bias), jnp.tile(norm_g, GDN_HEADS).reshape(1, -1))


def _mix_residual(x_ref, y_refs, wo_ref):
    acc = x_ref[...]
    for i, y_ref in enumerate(y_refs):
        acc = acc + _bdot(y_ref[...], wo_ref[i * BRANCH_W:(i + 1) * BRANCH_W, :])
    return acc


def _ffn_kernel(x_ref, y0_ref, y1_ref, y2_ref, y3_ref, wo_ref, g_ref, wg_ref, wu_ref, wd_ref, o_ref):
    x = _mix_residual(x_ref, (y0_ref, y1_ref, y2_ref, y3_ref), wo_ref)
    hb = _rms(x, g_ref[...]).astype(BF16)
    acc = None
    for f in range(D_FF // FFN_TF):
        cols = slice(f * FFN_TF, (f + 1) * FFN_TF)
        a = jnp.dot(hb, wg_ref[:, cols], preferred_element_type=F32)
        u = jnp.dot(hb, wu_ref[:, cols], preferred_element_type=F32)
        y = jnp.dot((_silu(a) * u).astype(BF16), wd_ref[cols, :], preferred_element_type=F32)
        acc = y if acc is None else acc + y
    o_ref[...] = x + acc


def _ffn_dense(x2, ys, w_out, g, w_gate, w_up, w_down):
    t = x2.shape[0]
    row = lambda n: pl.BlockSpec((FFN_TM, n), lambda i: (i, 0))
    resident = lambda shape: pl.BlockSpec(shape, lambda i: (0, 0), pipeline_mode=pl.Buffered(1))
    return pl.pallas_call(
        _ffn_kernel,
        grid=(t // FFN_TM,),
        in_specs=[row(D_MODEL)] + [row(BRANCH_W)] * 4 + [pl.BlockSpec((D_MODEL, D_MODEL), lambda i: (0, 0)),
                  pl.BlockSpec((1, D_MODEL), lambda i: (0, 0)),
                  resident((D_MODEL, D_FF)), resident((D_MODEL, D_FF)), resident((D_FF, D_MODEL))],
        out_specs=row(D_MODEL),
        out_shape=jax.ShapeDtypeStruct((t, D_MODEL), F32),
        compiler_params=_cparams(("parallel",), 52),
        name="ffn_dense",
    )(x2, *ys, w_out.astype(BF16), g.reshape(1, -1), w_gate.astype(BF16), w_up.astype(BF16), w_down.astype(BF16))


def _tiles_to_rows(ref, m):
    return jnp.concatenate([ref[pl.ds(c, m, stride=ROW_TILE), :] for c in range(ROW_TILE)], axis=1)


def _rows_to_tiles(ref, value):
    for c in range(ROW_TILE):
        ref[pl.ds(c, value.shape[0], stride=ROW_TILE), :] = value[:, c * LANES:(c + 1) * LANES]


def _moe_kernel(be_ref, nu_ref, x_ref, g_ref, wg_ref, wu_ref, wd_ref, o_ref):
    del be_ref

    @pl.when(pl.program_id(0) < nu_ref[0])
    def _():
        xb = _rms(_tiles_to_rows(x_ref, MOE_TM), g_ref[...]).astype(BF16)
        acc = None
        for f in range(D_FF_EXPERT // MOE_TF):
            cols = slice(f * MOE_TF, (f + 1) * MOE_TF)
            a = jnp.dot(xb, wg_ref[:, cols], preferred_element_type=F32)
            u = jnp.dot(xb, wu_ref[:, cols], preferred_element_type=F32)
            y = jnp.dot((_silu(a) * u).astype(BF16), wd_ref[cols, :], preferred_element_type=F32)
            acc = y if acc is None else acc + y
        _rows_to_tiles(o_ref, acc)

    @pl.when(pl.program_id(0) >= nu_ref[0])
    def _():
        o_ref[...] = jnp.zeros_like(o_ref)


def _moe_experts(buf, g, block_e, n_used, wg, wu, wd):
    rows = buf.shape[0] // ROW_TILE
    expert = lambda shape, buffers=2: pl.BlockSpec((None,) + shape, lambda m, be, nu: (be[m], 0, 0),
                                                   pipeline_mode=pl.Buffered(buffers))
    grid_spec = pltpu.PrefetchScalarGridSpec(
        num_scalar_prefetch=2,
        grid=(rows // MOE_TM,),
        in_specs=[pl.BlockSpec((MOE_TM * ROW_TILE, LANES), lambda m, be, nu: (jnp.minimum(m, nu[0] - 1), 0)),
                  pl.BlockSpec((1, D_MODEL), lambda m, be, nu: (0, 0)),
                  expert((D_MODEL, D_FF_EXPERT), 1), expert((D_MODEL, D_FF_EXPERT)), expert((D_FF_EXPERT, D_MODEL))],
        out_specs=pl.BlockSpec((MOE_TM * ROW_TILE, LANES), lambda m, be, nu: (m, 0)),
    )
    return pl.pallas_call(
        _moe_kernel,
        grid_spec=grid_spec,
        out_shape=jax.ShapeDtypeStruct((rows * ROW_TILE, LANES), F32),
        compiler_params=_cparams(("arbitrary",), 56),
        name="moe_experts",
    )(block_e, n_used, buf, g.reshape(1, -1), wg, wu, wd)


def _route_kernel(x_ref, y0_ref, y1_ref, y2_ref, y3_ref, wo_ref, g_ref, wt_ref,
                  xo_ref, e_ref, r_ref, gc_ref, cnt_ref, run_ref):
    tm = ROUTE_TM

    @pl.when(pl.program_id(0) == 0)
    def _():
        run_ref[...] = jnp.zeros_like(run_ref)

    x = _mix_residual(x_ref, (y0_ref, y1_ref, y2_ref, y3_ref), wo_ref)
    _rows_to_tiles(xo_ref, x)
    h = _rms(x, g_ref[...])
    w = wt_ref[...]
    w_hi = w.astype(BF16).astype(F32)
    w_parts = jnp.concatenate([w_hi, w - w_hi], axis=0).astype(BF16)
    h_hi = h.astype(BF16)
    h_lo = (h - h_hi.astype(F32)).astype(BF16)
    nt = lambda a, b: lax.dot_general(a, b, (((1,), (1,)), ((), ())), preferred_element_type=F32)
    parts = nt(w_parts, h_hi) + nt(w_parts, h_lo)
    lt = parts[:N_EXPERTS] + parts[N_EXPERTS:]
    sub = lax.broadcasted_iota(jnp.int32, (N_EXPERTS, tm), 0)
    m1 = jnp.max(lt, axis=0, keepdims=True)
    i1 = jnp.min(jnp.where(lt == m1, sub, N_EXPERTS), axis=0, keepdims=True)
    lt2 = jnp.where(sub == i1, -jnp.inf, lt)
    m2 = jnp.max(lt2, axis=0, keepdims=True)
    i2 = jnp.min(jnp.where(lt2 == m2, sub, N_EXPERTS), axis=0, keepdims=True)
    oh0 = jnp.where(sub == i1, 1.0, 0.0)
    oh1 = jnp.where(sub == i2, 1.0, 0.0)
    cnt = oh0 + oh1
    ti = lax.broadcasted_iota(jnp.int32, (tm, tm), 0)
    tj = lax.broadcasted_iota(jnp.int32, (tm, tm), 1)
    upper = jnp.where(ti < tj, 1.0, 0.0).astype(BF16)
    pre = jnp.dot(cnt.astype(BF16), upper, preferred_element_type=F32) + run_ref[:, 0:1]
    r0 = jnp.sum(oh0 * pre, axis=0, keepdims=True)
    r1 = jnp.sum(oh1 * pre, axis=0, keepdims=True)
    e_ref[...] = jnp.concatenate([i1, i2], axis=0)
    r_ref[...] = jnp.concatenate([r0, r1], axis=0).astype(jnp.int32)
    run = run_ref[...] + jnp.sum(cnt, axis=1, keepdims=True)
    run_ref[...] = run
    cnt_ref[...] = run
    ex = jnp.exp(m2 - m1)
    g0 = 1.0 / (1.0 + ex)
    gates = jnp.where(sub == 0, g0, jnp.where(sub == 1, ex * g0, 0.0))
    er = lax.broadcasted_iota(jnp.int32, (N_EXPERTS, LANES), 0)
    ec = lax.broadcasted_iota(jnp.int32, (N_EXPERTS, LANES), 1)
    eye = jnp.where(er == ec, 1.0, 0.0).astype(BF16)
    acc = None
    for _ in range(3):
        piece = gates.astype(BF16)
        term = lax.dot_general(piece, eye, (((0,), (0,)), ((), ())), preferred_element_type=F32)
        acc = term if acc is None else acc + term
        gates = gates - piece.astype(F32)
    gc_ref[...] = acc


def _moe_route(x2, ys, w_out, ffn_norm, w_router):
    t = x2.shape[0]
    tm = ROUTE_TM
    row = lambda n: pl.BlockSpec((tm, n), lambda i: (i, 0))
    return pl.pallas_call(
        _route_kernel,
        grid=(t // tm,),
        in_specs=[row(D_MODEL)] + [row(BRANCH_W)] * 4 + [pl.BlockSpec((D_MODEL, D_MODEL), lambda i: (0, 0)),
                  pl.BlockSpec((1, D_MODEL), lambda i: (0, 0)),
                  pl.BlockSpec((N_EXPERTS, D_MODEL), lambda i: (0, 0))],
        out_specs=[pl.BlockSpec((tm * ROW_TILE, LANES), lambda i: (i, 0)),
                   pl.BlockSpec((TOP_K, tm), lambda i: (0, i)),
                   pl.BlockSpec((TOP_K, tm), lambda i: (0, i)),
                   row(LANES),
                   pl.BlockSpec((N_EXPERTS, LANES), lambda i: (0, 0))],
        out_shape=[jax.ShapeDtypeStruct((t * ROW_TILE, LANES), F32),
                   jax.ShapeDtypeStruct((TOP_K, t), jnp.int32),
                   jax.ShapeDtypeStruct((TOP_K, t), jnp.int32),
                   jax.ShapeDtypeStruct((t, LANES), F32),
                   jax.ShapeDtypeStruct((N_EXPERTS, LANES), F32)],
        scratch_shapes=[pltpu.VMEM((N_EXPERTS, LANES), F32)],
        compiler_params=_cparams(("arbitrary",), 40),
        name="moe_route",
    )(x2, *ys, w_out.astype(BF16), ffn_norm.reshape(1, -1), w_router.T)


def _row_copy(src_ref, src_row, dst_ref, dst_row, sem):
    tile = lambda row: pl.ds(row * ROW_TILE if isinstance(row, int) else pl.multiple_of(row * ROW_TILE, ROW_TILE),
                             ROW_TILE)
    return pltpu.make_async_copy(src_ref.at[tile(src_row), :], dst_ref.at[tile(dst_row), :], sem)


def _dispatch_kernel(plo_ref, pn_ref, dest_hbm, x_ref, out_hbm, idx_ref, zero_ref, idx_sem, row_sem, pad_sem):
    tm = ROUTE_TM
    load = pltpu.make_async_copy(dest_hbm.at[pl.program_id(0)], idx_ref, idx_sem)
    load.start()

    @pl.when(pl.program_id(0) == 0)
    def _():
        zero_ref[...] = jnp.zeros_like(zero_ref)
        for e in range(N_EXPERTS):
            def fill(r, carry, e=e):
                _row_copy(zero_ref, 0, out_hbm, plo_ref[e] + r, pad_sem).start()
                return carry
            lax.fori_loop(0, pn_ref[e], fill, 0)
        for e in range(N_EXPERTS):
            def done(r, carry, e=e):
                _row_copy(zero_ref, 0, out_hbm, plo_ref[e] + r, pad_sem).wait()
                return carry
            lax.fori_loop(0, pn_ref[e], done, 0)
        first_free = (plo_ref[N_EXPERTS - 1] + pn_ref[N_EXPERTS - 1]) // MOE_TM

        block_rows = MOE_TM * ROW_TILE

        def block_copy(b):
            return pltpu.make_async_copy(
                zero_ref, out_hbm.at[pl.ds(pl.multiple_of(b * block_rows, block_rows), block_rows), :], pad_sem)

        def fill_block(b, carry):
            block_copy(b).start()
            return carry

        def done_block(b, carry):
            block_copy(b).wait()
            return carry

        lax.fori_loop(first_free, out_hbm.shape[0] // block_rows, fill_block, 0)
        lax.fori_loop(first_free, out_hbm.shape[0] // block_rows, done_block, 0)

    load.wait()
    for t in range(tm):
        for k in range(TOP_K):
            _row_copy(x_ref, t, out_hbm, idx_ref[k * tm + t], row_sem).start(priority=k)
    for t in range(tm):
        for k in range(TOP_K):
            _row_copy(x_ref, t, out_hbm, idx_ref[k * tm + t], row_sem).wait()


def _moe_dispatch(xt, dest2, pad_lo, pad_n, n_rows):
    t = xt.shape[0] // ROW_TILE
    tm = ROUTE_TM
    grid_spec = pltpu.PrefetchScalarGridSpec(
        num_scalar_prefetch=2,
        grid=(t // tm,),
        in_specs=[pl.BlockSpec(memory_space=pl.ANY),
                  pl.BlockSpec((tm * ROW_TILE, LANES), lambda i, lo, n: (i, 0))],
        out_specs=pl.BlockSpec(memory_space=pl.ANY),
        scratch_shapes=[pltpu.SMEM((TOP_K * tm,), jnp.int32), pltpu.VMEM((MOE_TM * ROW_TILE, LANES), F32),
                        pltpu.SemaphoreType.DMA, pltpu.SemaphoreType.DMA, pltpu.SemaphoreType.DMA],
    )
    return pl.pallas_call(
        _dispatch_kernel,
        grid_spec=grid_spec,
        out_shape=jax.ShapeDtypeStruct((n_rows * ROW_TILE, LANES), F32),
        compiler_params=_cparams(("arbitrary",), 32),
        name="moe_dispatch",
    )(pad_lo, pad_n, dest2, xt)


def _combine_kernel(dest_hbm, x_ref, gc_ref, fn_ref, y_hbm, o_ref, idx0_ref, idx1_ref, ybuf_ref, idx_sem, row_sem):
    tm = ROUTE_TM
    i, n = pl.program_id(0), pl.num_programs(0)
    idx_refs = (idx0_ref, idx1_ref)

    def gather(idx_ref, slot, t, k):
        return _row_copy(y_hbm, idx_ref[k * tm + t], ybuf_ref.at[slot, k], t, row_sem.at[slot])

    @pl.when(i == 0)
    def _():
        first = pltpu.make_async_copy(dest_hbm.at[0], idx0_ref, idx_sem.at[0])
        first.start()
        first.wait()

        def issue(t, carry):
            for k in range(TOP_K):
                gather(idx0_ref, 0, t, k).start()
            return carry

        lax.fori_loop(0, tm, issue, 0)

    def step(slot):
        nxt = 1 - slot
        load = pltpu.make_async_copy(dest_hbm.at[jnp.minimum(i + 1, n - 1)], idx_refs[nxt], idx_sem.at[nxt])
        load.start()
        for t in range(tm):
            for k in range(TOP_K):
                gather(idx_refs[slot], slot, t, k).wait()
        load.wait()
        for t in range(tm):
            for k in range(TOP_K):
                gather(idx_refs[nxt], nxt, t, k).start(priority=k)
        gc = gc_ref[...]
        x = (_tiles_to_rows(x_ref, tm) + gc[:, 0:1] * _tiles_to_rows(ybuf_ref.at[slot, 0], tm)
             + gc[:, 1:2] * _tiles_to_rows(ybuf_ref.at[slot, 1], tm))
        o_ref[...] = _rms(x, fn_ref[...])

        @pl.when(i == n - 1)
        def _():
            for t in range(tm):
                for k in range(TOP_K):
                    gather(idx_refs[nxt], nxt, t, k).wait()

    for slot in range(2):
        pl.when(i % 2 == slot)(functools.partial(step, slot))


def _moe_combine_norm(xt, y_buf, dest2, gcol, final_norm):
    t = xt.shape[0] // ROW_TILE
    tm = ROUTE_TM
    return pl.pallas_call(
        _combine_kernel,
        grid=(t // tm,),
        in_specs=[pl.BlockSpec(memory_space=pl.ANY),
                  pl.BlockSpec((tm * ROW_TILE, LANES), lambda i: (i, 0)),
                  pl.BlockSpec((tm, LANES), lambda i: (i, 0)),
                  pl.BlockSpec((1, D_MODEL), lambda i: (0, 0)),
                  pl.BlockSpec(memory_space=pl.ANY)],
        out_specs=pl.BlockSpec((tm, D_MODEL), lambda i: (i, 0)),
        out_shape=jax.ShapeDtypeStruct((t, D_MODEL), F32),
        scratch_shapes=[pltpu.SMEM((TOP_K * tm,), jnp.int32), pltpu.SMEM((TOP_K * tm,), jnp.int32),
                        pltpu.VMEM((2, TOP_K, tm * ROW_TILE, LANES), F32),
                        pltpu.SemaphoreType.DMA((2,)), pltpu.SemaphoreType.DMA((2,))],
        compiler_params=_cparams(("arbitrary",), 40),
        name="moe_combine_norm",
    )(dest2, xt, gcol, final_norm.reshape(1, -1), y_buf)


def _moe_layer(x2, ys, w_out, ffn_norm, w_router, w_gate, w_up, w_down, final_norm):
    t = x2.shape[0]
    tm = ROUTE_TM
    xt, e01, r01, gcol, cnt = _moe_route(x2, ys, w_out, ffn_norm, w_router)
    counts = cnt[:, 0].astype(jnp.int32)
    padded = (counts + MOE_TM - 1) // MOE_TM * MOE_TM
    pad_ends = jnp.cumsum(padded)
    pad_starts = pad_ends - padded
    n_blocks = -(-t * TOP_K // MOE_TM) + N_EXPERTS
    blk_start = jnp.arange(n_blocks, dtype=jnp.int32) * MOE_TM
    block_e = jnp.minimum(jnp.sum(blk_start[:, None] >= pad_ends[None, :], axis=1), N_EXPERTS - 1).astype(jnp.int32)
    n_used = (pad_ends[-1] // MOE_TM).astype(jnp.int32).reshape(1)
    dest = r01 + jnp.sum(jnp.where(e01[None] == jnp.arange(N_EXPERTS)[:, None, None], pad_starts[:, None, None], 0),
                         axis=0)
    dest2 = dest.reshape(TOP_K, t // tm, tm).transpose(1, 0, 2).reshape(t // tm, TOP_K * tm)
    xs = _moe_dispatch(xt, dest2, pad_starts + counts, padded - counts, n_blocks * MOE_TM)
    y_buf = _moe_experts(xs, ffn_norm, block_e, n_used, w_gate.astype(BF16), w_up.astype(BF16),
                         w_down.astype(BF16))
    return _moe_combine_norm(xt, y_buf, dest2, gcol, final_norm)


def _permute_w_in(w_in):
    s5 = w_in[:, 0:256]
    gm = w_in[:, 256:768]
    qkv = w_in[:, 768:1536]
    z = w_in[:, 1536:1792]
    a = w_in[:, 1792:1796]
    b = w_in[:, 1796:1800]
    sc = w_in[:, 1800:2568]
    ab = jnp.zeros((D_MODEL, LANES), F32).at[:, 0:GDN_HEADS].set(a).at[:, AB_B_LANE:AB_B_LANE + GDN_HEADS].set(b)
    return jnp.concatenate([gm, s5, z, qkv, sc, ab], axis=1).astype(BF16)


def kernel(x, mix_norm, w_in, s5_lam_re, s5_lam_im, s5_log_step, s5_b_re, s5_b_im, s5_c_re, s5_c_im, s5_d, s5_w_glu, s5_b_glu, s5_out_norm, sgu_ln_g, sgu_ln_b, sgu_w, sgu_b, gmlp_out_norm, gdn_conv, gdn_a_log, gdn_dt_bias, gdn_norm, sc_conv, sc_out_norm, w_out, ffn_norm, ffn_w_gate, ffn_w_up, ffn_w_down, moe_router, moe_w_gate, moe_w_up, moe_w_down, final_norm):
    bsz, seqlen, d = x.shape
    t = bsz * seqlen
    x2 = x.reshape(t, d)
    out = None
    for l in range(DEPTH):
        p2 = _in_proj(x2, mix_norm[l].reshape(1, -1), _permute_w_in(w_in[l]))
        p3 = p2.reshape(bsz, seqlen, P_COLS)
        wb, wc, pw = _s5_params(s5_lam_re[l], s5_lam_im[l], s5_log_step[l], s5_b_re[l], s5_b_im[l],
                                s5_c_re[l], s5_c_im[l])
        y_s5 = _s5_mixer(p3, wb, wc, pw, s5_d[l], s5_w_glu[l], s5_b_glu[l], s5_out_norm[l])
        y_gm = _gmlp_mixer(p3, sgu_ln_g[l], sgu_ln_b[l], sgu_w[l], sgu_b[l], gmlp_out_norm[l])
        y_gdn = _gdn_mixer(p3, gdn_conv[l], gdn_a_log[l], gdn_dt_bias[l], gdn_norm[l])
        y_sc = _shortconv_mixer(p3, sc_conv[l], sc_out_norm[l])
        ys = [y.reshape(t, BRANCH_W) for y in (y_s5, y_gm, y_gdn, y_sc)]
        i = l // 2
        if l % 2 == 0:
            x2 = _ffn_dense(x2, ys, w_out[l], ffn_norm[l], ffn_w_gate[i], ffn_w_up[i], ffn_w_down[i])
        else:
            out = _moe_layer(x2, ys, w_out[l], ffn_norm[l], moe_router[i], moe_w_gate[i], moe_w_up[i],
                             moe_w_down[i], final_norm)
    return out.reshape(bsz, seqlen, d)
```
